```python
import math
import jax
import jax.numpy as jnp
from jax import lax
import numpy as np

D_MODEL = 1024
BATCH = 16
SEQ = 2048
DEPTH = 2

DN_HEAD_DIM = 64
DN_WIDTH = D_MODEL // 4
DN_HEADS = DN_WIDTH // DN_HEAD_DIM
DN_CONV = 4
DN_CHUNK = 64
AT_HEAD_DIM = 64
AT_WIDTH = D_MODEL // 2
AT_HEADS = AT_WIDTH // AT_HEAD_DIM
WINDOWS = ((128, 1), (512, 4), (2048, 16))
AT_BLOCK = 128
N_BUCKETS = 32
MAX_DISTANCE = 2048
CV_WIDTH = D_MODEL // 4
CV_KERNEL = 31
MIX_WIDTH = DN_WIDTH + AT_WIDTH + CV_WIDTH
N_IN = 4 * DN_WIDTH + 2 * DN_HEADS + 3 * AT_WIDTH + 2 * CV_WIDTH
N_GROUPS = 4
EXPERTS_PER_GROUP = 8
N_EXPERTS = N_GROUPS * EXPERTS_PER_GROUP
TOP_K = 2
D_EXPERT = D_MODEL // 2
MOE_BLOCK = 128
NEG_INF = -1e30
EPS = 1e-6

kernel_name = 'hymba_style_deltanet_dilated_conformer_hmoe'


def _rms_norm(x, g):
    xf = x.astype(jnp.float32)
    return (xf * lax.rsqrt(jnp.mean(xf * xf, axis=-1, keepdims=True) + EPS) * g).astype(x.dtype)


def _layer_norm(x, g, b):
    xf = x.astype(jnp.float32)
    mu = jnp.mean(xf, axis=-1, keepdims=True)
    var = jnp.mean(jnp.square(xf - mu), axis=-1, keepdims=True)
    return ((xf - mu) * lax.rsqrt(var + EPS) * g + b).astype(x.dtype)


def _l2norm(x):
    return x * lax.rsqrt(jnp.sum(x * x, axis=-1, keepdims=True) + EPS)


def _causal_depthwise_conv(x, w):
    width, ch = w.shape
    return lax.conv_general_dilated(x, w[:, None, :], window_strides=(1,), padding=[(width - 1, 0)],
                                    dimension_numbers=('NWC', 'WIO', 'NWC'), feature_group_count=ch)


def _chunked_gated_delta_rule(q, k, v, g, beta):
    bsz, heads, seq, dk = q.shape
    dv = v.shape[-1]
    n = seq // DN_CHUNK
    q, k, v = (t.reshape(bsz, heads, n, DN_CHUNK, t.shape[-1]) for t in (q, k, v))
    g, beta = (t.reshape(bsz, heads, n, DN_CHUNK) for t in (g, beta))
    G = jnp.cumsum(g, axis=-1)
    idx = jnp.arange(DN_CHUNK)
    causal = idx[:, None] >= idx[None, :]
    strict = idx[:, None] > idx[None, :]
    decay = jnp.exp(jnp.where(causal, G[..., :, None] - G[..., None, :], -jnp.inf))
    kb = k * beta[..., None]
    a_mat = jnp.where(strict, jnp.einsum('bhnid,bhnjd->bhnij', kb, k) * decay, 0.0)
    eye = jnp.eye(DN_CHUNK, dtype=a_mat.dtype)
    t_mat = lax.linalg.triangular_solve(a_mat + eye, jnp.broadcast_to(eye, a_mat.shape),
                                        left_side=True, lower=True, unit_diagonal=True)
    e_g = jnp.exp(G)[..., None]
    w_mat = t_mat @ (kb * e_g)
    u_mat = t_mat @ (v * beta[..., None])
    qk = jnp.einsum('bhnid,bhnjd->bhnij', q, k) * decay
    q_dec = q * e_g
    k_dec = k * jnp.exp(G[..., -1:] - G)[..., None]
    g_last = jnp.exp(G[..., -1])[..., None, None]

    def step(state, xs):
        w_c, u_c, qk_c, qd_c, kd_c, gl_c = xs
        v_new = u_c - w_c @ state
        o_c = qd_c @ state + qk_c @ v_new
        state = state * gl_c + jnp.swapaxes(kd_c, -1, -2) @ v_new
        return state, o_c

    xs = tuple(jnp.moveaxis(t, 2, 0) for t in (w_mat, u_mat, qk, q_dec, k_dec, g_last))
    state0 = jnp.zeros((bsz, heads, dk, dv), q.dtype)
    _, o = lax.scan(step, state0, xs)
    return jnp.moveaxis(o, 0, 2).reshape(bsz, heads, seq, dv)


def _gated_deltanet(q, k, v, z, a, b, conv_w, a_log, dt_bias, out_norm):
    out_dtype = q.dtype
    bsz, seq, _ = q.shape
    qkv = jax.nn.silu(_causal_depthwise_conv(jnp.concatenate([q, k, v], axis=-1), conv_w))
    q, k, v = jnp.split(qkv, 3, axis=-1)

    def heads(t):
        return t.reshape(bsz, seq, DN_HEADS, DN_HEAD_DIM).transpose(0, 2, 1, 3).astype(jnp.float32)

    q, k, v = heads(q), heads(k), heads(v)
    q = _l2norm(q) * (DN_HEAD_DIM ** -0.5)
    k = _l2norm(k)
    beta = jnp.swapaxes(jax.nn.sigmoid(b.astype(jnp.float32)), 1, 2)
    g = -jnp.exp(a_log.astype(jnp.float32)) * jax.nn.softplus(a.astype(jnp.float32) + dt_bias.astype(jnp.float32))
    g = jnp.swapaxes(g, 1, 2)
    o = jnp.swapaxes(_chunked_gated_delta_rule(q, k, v, g, beta), 1, 2)
    zf = z.reshape(bsz, seq, DN_HEADS, DN_HEAD_DIM).astype(jnp.float32)
    o = _rms_norm(o, out_norm.astype(jnp.float32)) * jax.nn.silu(zf)
    return o.reshape(bsz, seq, DN_WIDTH).astype(out_dtype)


def _t5_bucket(dist):
    max_exact = N_BUCKETS // 2
    d = jnp.maximum(dist, 1).astype(jnp.float32)
    log_bucket = max_exact + (jnp.log(d / max_exact) / math.log(MAX_DISTANCE / max_exact)
                              * (N_BUCKETS - max_exact)).astype(jnp.int32)
    return jnp.where(dist < max_exact, dist, jnp.minimum(log_bucket, N_BUCKETS - 1))


def _dilated_band_attention(q, k, v, rel_bias, window, dilation):
    bsz, seq, heads, hd = q.shape
    n_back = window // dilation
    length = seq // dilation
    padded = -(-length // AT_BLOCK) * AT_BLOCK
    nb = padded // AT_BLOCK
    bs = bsz * dilation

    def strided(t):
        t = t.reshape(bsz, length, dilation, heads, hd).transpose(0, 2, 1, 3, 4)
        return t.reshape(bs, length, heads, hd)

    qs = jnp.pad(strided(q), ((0, 0), (0, padded - length), (0, 0), (0, 0))).reshape(bs, nb, AT_BLOCK, heads, hd)

    def band(t):
        t = jnp.pad(strided(t), ((0, 0), (AT_BLOCK, padded - length), (0, 0), (0, 0)))
        t = t.reshape(bs, nb + 1, AT_BLOCK, heads, hd)
        return jnp.concatenate([t[:, :-1], t[:, 1:]], axis=2)

    kb, vb = band(k), band(v)
    rel = jnp.arange(AT_BLOCK)[:, None] + AT_BLOCK - jnp.arange(2 * AT_BLOCK)[None, :]
    key_idx = (jnp.arange(nb)[:, None] - 1) * AT_BLOCK + jnp.arange(2 * AT_BLOCK)[None, :]
    valid = ((rel >= 0) & (rel <= n_back))[None] & (key_idx >= 0)[:, None, :]
    bias = rel_bias[_t5_bucket(dilation * jnp.clip(rel, 0, n_back))]
    scores = jnp.einsum('bnqhd,bnkhd->bnhqk', qs, kb) + jnp.transpose(bias, (2, 0, 1))
    scores = jnp.where(valid[None, :, None], scores, NEG_INF)
    m = jnp.max(scores, axis=-1, keepdims=True)
    p = jnp.exp(scores - m)
    s = jnp.sum(p, axis=-1)
    o = jnp.einsum('bnhqk,bnkhd->bnqhd', p, vb) / jnp.swapaxes(s, 2, 3)[..., None]
    lse = jnp.swapaxes(m[..., 0] + jnp.log(s), 2, 3)

    def unstrided(t):
        t = t.reshape(bsz, dilation, padded, *t.shape[3:])[:, :, :length]
        t = jnp.swapaxes(t, 1, 2)
        return t.reshape(bsz, seq, *t.shape[3:])

    return unstrided(o), unstrided(lse)


def _dilated_attention(q, k, v, q_norm, k_norm, rel_bias):
    out_dtype = q.dtype
    bsz, seq, _ = q.shape

    def heads(t):
        return t.reshape(bsz, seq, AT_HEADS, AT_HEAD_DIM)

    q = _rms_norm(heads(q), q_norm).astype(jnp.float32) * (AT_HEAD_DIM ** -0.5)
    k = _rms_norm(heads(k), k_norm).astype(jnp.float32)
    v = heads(v).astype(jnp.float32)
    table = rel_bias.astype(jnp.float32)
    outs, lses = [], []
    for window, dilation in WINDOWS:
        o_g, lse_g = _dilated_band_attention(q, k, v, table, window, dilation)
        outs.append(o_g)
        lses.append(lse_g)
    weights = jax.nn.softmax(jnp.stack(lses), axis=0)
    o = jnp.einsum('gbsh,gbshd->bshd', weights, jnp.stack(outs))
    return o.reshape(bsz, seq, AT_WIDTH).astype(out_dtype)


def _conformer_conv(u, dw, dw_b, ln_g, ln_b):
    a, gate = jnp.split(u, 2, axis=-1)
    y = a * jax.nn.sigmoid(gate)
    y = _causal_depthwise_conv(y, dw) + dw_b
    y = _layer_norm(y, ln_g, ln_b)
    return jax.nn.silu(y)


def _hierarchical_moe(h, rg_w, rg_b, re_w, re_b, w_gate, w_up, w_down):
    bsz, seq, d = h.shape
    xt = h.reshape(-1, d)
    n_tok = xt.shape[0]
    g_logits = (xt @ rg_w).astype(jnp.float32) + rg_b.astype(jnp.float32)
    g_sel = jnp.argmax(g_logits, axis=-1)
    p_group = jnp.take_along_axis(jax.nn.softmax(g_logits, axis=-1), g_sel[:, None], axis=-1)
    e_logits = ((xt @ re_w).astype(jnp.float32) + re_b.astype(jnp.float32)).reshape(n_tok, N_GROUPS, EXPERTS_PER_GROUP)
    e_logits = jnp.take_along_axis(e_logits, g_sel[:, None, None], axis=1)[:, 0]
    top_v, top_i = lax.top_k(e_logits, TOP_K)
    gate = p_group * jax.nn.softmax(top_v, axis=-1)
    expert_id = g_sel[:, None] * EXPERTS_PER_GROUP + top_i

    n_assign = n_tok * TOP_K
    flat_e = expert_id.reshape(-1)
    order = jnp.argsort(flat_e)
    sorted_e = flat_e[order]
    tok = order // TOP_K
    counts = jnp.bincount(flat_e, length=N_EXPERTS)
    padded = (counts + MOE_BLOCK - 1) // MOE_BLOCK * MOE_BLOCK
    pad_end = jnp.cumsum(padded)
    pad_start = pad_end - padded
    start = jnp.cumsum(counts) - counts
    dest = pad_start[sorted_e] + jnp.arange(n_assign) - start[sorted_e]
    n_blocks = n_assign // MOE_BLOCK + N_EXPERTS
    buf = jnp.zeros((n_blocks * MOE_BLOCK, d), xt.dtype).at[dest].set(xt[tok])
    block_expert = jnp.minimum(jnp.searchsorted(pad_end, jnp.arange(n_blocks) * MOE_BLOCK, side='right'), N_EXPERTS - 1)

    def expert_block(args):
        xb, e = args
        hb = jax.nn.silu(xb @ w_gate[e]) * (xb @ w_up[e])
        return hb @ w_down[e]

    yb = lax.map(expert_block, (buf.reshape(n_blocks, MOE_BLOCK, d), block_expert))
    y_sorted = yb.reshape(-1, d)[dest] * gate.reshape(-1)[order][:, None].astype(xt.dtype)
    out = jnp.zeros((n_tok, d), xt.dtype).at[tok].add(y_sorted)
    return out.reshape(bsz, seq, d)


def setup_inputs(seed: int = 0) -> dict:
    key = jax.random.key(seed)
    ks = jax.random.split(key, 24)
    L = DEPTH

    def nrm(k, shape, scale):
        return jax.random.normal(k, shape, jnp.float32) * scale

    dt = jnp.exp(jax.random.uniform(ks[5], (L, DN_HEADS), minval=math.log(1e-3), maxval=math.log(1e-1)))
    return {
        'x': nrm(ks[0], (BATCH, SEQ, D_MODEL), 1.0),
        'norm_mix': 1.0 + nrm(ks[1], (L, D_MODEL), 0.02),
        'w_in': nrm(ks[2], (L, D_MODEL, N_IN), D_MODEL ** -0.5),
        'dn_conv': nrm(ks[3], (L, DN_CONV, 3 * DN_WIDTH), DN_CONV ** -0.5),
        'dn_a_log': jnp.log(jax.random.uniform(ks[4], (L, DN_HEADS), minval=1.0, maxval=16.0)),
        'dn_dt_bias': dt + jnp.log(-jnp.expm1(-dt)),
        'dn_out_norm': 1.0 + nrm(ks[6], (L, DN_HEAD_DIM), 0.02),
        'at_q_norm': 1.0 + nrm(ks[7], (L, AT_HEAD_DIM), 0.02),
        'at_k_norm': 1.0 + nrm(ks[8], (L, AT_HEAD_DIM), 0.02),
        'rel_bias': nrm(ks[9], (N_BUCKETS, AT_HEADS), 0.5),
        'cv_dw': nrm(ks[10], (L, CV_KERNEL, CV_WIDTH), CV_KERNEL ** -0.5),
        'cv_dw_bias': nrm(ks[11], (L, CV_WIDTH), 0.02),
        'cv_ln_g': 1.0 + nrm(ks[12], (L, CV_WIDTH), 0.02),
        'cv_ln_b': nrm(ks[13], (L, CV_WIDTH), 0.02),
        'w_out': nrm(ks[14], (L, MIX_WIDTH, D_MODEL), MIX_WIDTH ** -0.5),
        'norm_ffn': 1.0 + nrm(ks[15], (L, D_MODEL), 0.02),
        'router_group_w': nrm(ks[16], (L, D_MODEL, N_GROUPS), D_MODEL ** -0.5),
        'router_group_b': nrm(ks[17], (L, N_GROUPS), 0.01),
        'router_expert_w': nrm(ks[18], (L, D_MODEL, N_EXPERTS), D_MODEL ** -0.5),
        'router_expert_b': nrm(ks[19], (L, N_EXPERTS), 0.01),
        'ex_gate': nrm(ks[20], (L, N_EXPERTS, D_MODEL, D_EXPERT), D_MODEL ** -0.5),
        'ex_up': nrm(ks[21], (L, N_EXPERTS, D_MODEL, D_EXPERT), D_MODEL ** -0.5),
        'ex_down': nrm(ks[22], (L, N_EXPERTS, D_EXPERT, D_MODEL), D_EXPERT ** -0.5),
    }


def reference(x, norm_mix, w_in, dn_conv, dn_a_log, dn_dt_bias, dn_out_norm, at_q_norm, at_k_norm,
              rel_bias, cv_dw, cv_dw_bias, cv_ln_g, cv_ln_b, w_out, norm_ffn, router_group_w,
              router_group_b, router_expert_w, router_expert_b, ex_gate, ex_up, ex_down):
    splits = [DN_WIDTH] * 4 + [DN_HEADS] * 2 + [AT_WIDTH] * 3 + [2 * CV_WIDTH]
    offsets = [int(o) for o in np.cumsum(splits)[:-1]]
    for layer in range(DEPTH):
        h = _rms_norm(x, norm_mix[layer])
        proj = h @ w_in[layer]
        dq, dk, dv, dz, da, db, aq, ak, av, cu = jnp.split(proj, offsets, axis=-1)
        y_dn = _gated_deltanet(dq, dk, dv, dz, da, db, dn_conv[layer], dn_a_log[layer],
                               dn_dt_bias[layer], dn_out_norm[layer])
        y_at = _dilated_attention(aq, ak, av, at_q_norm[layer], at_k_norm[layer], rel_bias)
        y_cv = _conformer_conv(cu, cv_dw[layer], cv_dw_bias[layer], cv_ln_g[layer], cv_ln_b[layer])
        mixed = jnp.concatenate([y_dn, y_at, y_cv], axis=-1).astype(x.dtype)
        x = x + mixed @ w_out[layer]
        h = _rms_norm(x, norm_ffn[layer])
        x = x + _hierarchical_moe(h, router_group_w[layer], router_group_b[layer], router_expert_w[layer],
                                  router_expert_b[layer], ex_gate[layer], ex_up[layer], ex_down[layer])
    return x
```

```python
import functools
import math

import jax
import jax.numpy as jnp
from jax import lax
from jax.experimental import pallas as pl
from jax.experimental.pallas import tpu as pltpu

F32 = jnp.float32
BF16 = jnp.bfloat16
HIGHEST = lax.Precision.HIGHEST

EPS = 1e-6
NEG_INF = -1e30

HEAD_DIM = 64
DN_HEADS = 4
DN_WIDTH = DN_HEADS * HEAD_DIM
DN_CONV = 4
DN_CHUNK = 64
AT_HEADS = 8
AT_WIDTH = AT_HEADS * HEAD_DIM
AT_BLOCK = 128
WINDOWS = ((128, 1), (512, 4), (2048, 16))
N_BUCKETS = 32
MAX_DISTANCE = 2048
CV_WIDTH = 256
CV_KERNEL = 31
N_GROUPS = 4
EXPERTS_PER_GROUP = 8
N_EXPERTS = N_GROUPS * EXPERTS_PER_GROUP
TOP_K = 2

LANES = 128
SUBLANES = 8
VMEM_LIMIT = 52 * 1024 * 1024

ROW_TILE = 512
SEQ_TILE = 256
MOE_BLOCK = 256
MOE_TOKENS = 256


def _mm(a, b, precision=None):
    return jnp.dot(a, b, preferred_element_type=F32, precision=precision)


def _mm_nt(a, b, precision=None):
    return lax.dot_general(a, b, (((1,), (1,)), ((), ())), preferred_element_type=F32,
                           precision=precision)


def _sigmoid(x):
    return 1.0 / (1.0 + jnp.exp(-x))


def _silu(x):
    return x * _sigmoid(x)


def _full_spec(a):
    nd = a.ndim
    return pl.BlockSpec(a.shape, lambda *_: (0,) * nd)


def _block_diag_ones(width, block, dtype):
    r = jnp.arange(width)[:, None] // block
    c = jnp.arange(width)[None, :] // block
    return (r == c).astype(dtype)


def _proj_kernel(x_ref, g_ref, wdn_ref, wat_ref, wcv_ref, wab_ref, bd_ref, qn_ref, kn_ref,
                 dn_ref, aq_ref, ak_ref, av_ref, cv_ref, ab_ref):
    x = x_ref[...]
    ms = jnp.mean(x * x, axis=-1, keepdims=True)
    h = (x * lax.rsqrt(ms + EPS) * g_ref[...]).astype(BF16)
    dn_ref[...] = _mm(h, wdn_ref[...]).astype(BF16)
    cv_ref[...] = _mm(h, wcv_ref[...]).astype(BF16)
    ab_ref[...] = _mm(h, wab_ref[...])
    at = _mm(h, wat_ref[...])
    q = at[:, 0:AT_WIDTH]
    k = at[:, AT_WIDTH:2 * AT_WIDTH]
    bd = bd_ref[...]
    qms = _mm((q * q).astype(BF16), bd) * (1.0 / HEAD_DIM)
    kms = _mm((k * k).astype(BF16), bd) * (1.0 / HEAD_DIM)
    aq_ref[...] = (q * lax.rsqrt(qms + EPS) * qn_ref[...]).astype(BF16)
    ak_ref[...] = (k * lax.rsqrt(kms + EPS) * kn_ref[...]).astype(BF16)
    av_ref[...] = at[:, 2 * AT_WIDTH:3 * AT_WIDTH].astype(BF16)


def _proj(x2, g, wdn, wat, wcv, wab, bd, qn, kn):
    n, d = x2.shape
    row = lambda w: pl.BlockSpec((ROW_TILE, w), lambda i: (i, 0))
    widths = (wdn.shape[1], AT_WIDTH, AT_WIDTH, AT_WIDTH, wcv.shape[1], LANES)
    dtypes = (BF16, BF16, BF16, BF16, BF16, F32)
    return pl.pallas_call(
        _proj_kernel,
        grid=(n // ROW_TILE,),
        in_specs=[row(d)] + [_full_spec(a) for a in (g, wdn, wat, wcv, wab, bd, qn, kn)],
        out_specs=[row(w) for w in widths],
        out_shape=[jax.ShapeDtypeStruct((n, w), t) for w, t in zip(widths, dtypes)],
        compiler_params=pltpu.CompilerParams(dimension_semantics=("parallel",),
                                             vmem_limit_bytes=VMEM_LIMIT),
        name="proj",
    )(x2, g, wdn, wat, wcv, wab, bd, qn, kn)


def _unit_lower_inverse(a, eye):
    mm = functools.partial(_mm, precision=HIGHEST)
    p = eye - a
    pw = a
    for _ in range(5):
        pw = mm(pw, pw)
        p = p + mm(p, pw)
    return p


def _dn_kernel(dn_ref, ab_ref, cw_ref, alog_ref, dtb_ref, onorm_ref, bd_ref, y_ref,
               xpad, qh, kh, vh, gs, bs, oh, st):
    seq = dn_ref.shape[1]
    n_tiles = seq // SEQ_TILE
    cw3 = 3 * DN_WIDTH
    pad = SUBLANES

    ab = ab_ref[0]
    sp_in = ab + dtb_ref[...]
    softplus = jnp.maximum(sp_in, 0.0) + jnp.log(1.0 + jnp.exp(-jnp.abs(sp_in)))
    gs[...] = -jnp.exp(alog_ref[...]) * softplus
    bs[...] = _sigmoid(ab)

    xpad[0:pad, :] = jnp.zeros((pad, cw3), F32)
    for t in range(n_tiles):
        r0 = t * SEQ_TILE
        xpad[pad + r0:pad + r0 + SEQ_TILE, :] = dn_ref[0, r0:r0 + SEQ_TILE, 0:cw3].astype(F32)
    bd = bd_ref[...]
    for t in range(n_tiles):
        r0 = t * SEQ_TILE
        acc = jnp.zeros((SEQ_TILE, cw3), F32)
        for j in range(DN_CONV):
            off = pad + r0 - (DN_CONV - 1) + j
            acc = acc + xpad[off:off + SEQ_TILE, :] * cw_ref[j:j + 1, :]
        y = _silu(acc)
        q = y[:, 0:DN_WIDTH]
        k = y[:, DN_WIDTH:2 * DN_WIDTH]
        v = y[:, 2 * DN_WIDTH:3 * DN_WIDTH]
        qss = _mm((q * q).astype(BF16), bd)
        kss = _mm((k * k).astype(BF16), bd)
        q = q * lax.rsqrt(qss + EPS) * (HEAD_DIM ** -0.5)
        k = k * lax.rsqrt(kss + EPS)
        for h in range(DN_HEADS):
            sl = slice(h * HEAD_DIM, (h + 1) * HEAD_DIM)
            qh[h, r0:r0 + SEQ_TILE, :] = q[:, sl]
            kh[h, r0:r0 + SEQ_TILE, :] = k[:, sl]
            vh[h, r0:r0 + SEQ_TILE, :] = v[:, sl]

    c = DN_CHUNK
    ri = lax.broadcasted_iota(jnp.int32, (c, c), 0)
    ci = lax.broadcasted_iota(jnp.int32, (c, c), 1)
    causal = ri >= ci
    strict = ri > ci
    lower_ones = causal.astype(F32)
    eye = (ri == ci).astype(F32)
    eye_bf = eye.astype(BF16)
    pick = (lax.broadcasted_iota(jnp.int32, (SUBLANES, LANES), 0)
            == lax.broadcasted_iota(jnp.int32, (SUBLANES, LANES), 1)).astype(F32)
    st[...] = jnp.zeros(st.shape, F32)

    def chunk_step(n, carry):
        r0 = pl.multiple_of(n * c, c)
        g_col = _mm(lower_ones, gs[pl.ds(r0, c), :], HIGHEST)
        g_row = _mm_nt(pick, g_col, HIGHEST)
        b_c = bs[pl.ds(r0, c), :]
        for h in range(DN_HEADS):
            gc = g_col[:, h:h + 1]
            gr = g_row[h:h + 1, :]
            gl = g_col[c - 1:c, h:h + 1]
            beta = b_c[:, DN_HEADS + h:DN_HEADS + h + 1]
            qc = qh[h, pl.ds(r0, c), :]
            kc = kh[h, pl.ds(r0, c), :]
            vc = vh[h, pl.ds(r0, c), :]
            decay = jnp.exp(jnp.where(causal, gc - gr, NEG_INF))
            kb = kc * beta
            a_mat = jnp.where(strict, _mm_nt(kb, kc) * decay, 0.0)
            t_mat = _unit_lower_inverse(a_mat, eye)
            e_g = jnp.exp(gc)
            w_c = _mm(t_mat, kb * e_g)
            u_c = _mm(t_mat, vc * beta)
            qk = _mm_nt(qc, kc) * decay
            q_dec = qc * e_g
            k_dec = kc * jnp.exp(gl - gc)
            k_dec_t = _mm_nt(eye_bf, k_dec.astype(BF16)).astype(BF16)
            state = st[h]
            v_new = u_c - _mm(w_c, state)
            oh[h, pl.ds(r0, c), :] = _mm(q_dec, state) + _mm(qk, v_new)
            st[h] = state * jnp.exp(gl) + _mm(k_dec_t, v_new.astype(BF16))
        return carry

    lax.fori_loop(0, seq // c, chunk_step, 0)

    for t in range(n_tiles):
        r0 = t * SEQ_TILE
        z = dn_ref[0, r0:r0 + SEQ_TILE, cw3:cw3 + DN_WIDTH].astype(F32)
        outs = []
        for h in range(DN_HEADS):
            o = oh[h, r0:r0 + SEQ_TILE, :]
            r = lax.rsqrt(jnp.mean(o * o, axis=-1, keepdims=True) + EPS)
            zh = z[:, h * HEAD_DIM:(h + 1) * HEAD_DIM]
            outs.append(o * r * onorm_ref[...] * _silu(zh))
        y_ref[0, r0:r0 + SEQ_TILE, :] = jnp.concatenate(outs, axis=-1).astype(BF16)


def _deltanet(dn, ab, conv_w, alog, dtb, onorm, bd):
    b, seq, w = dn.shape
    per_head = pltpu.VMEM((DN_HEADS, seq, HEAD_DIM), F32)
    return pl.pallas_call(
        _dn_kernel,
        grid=(b,),
        in_specs=[pl.BlockSpec((1, seq, w), lambda i: (i, 0, 0)),
                  pl.BlockSpec((1, seq, LANES), lambda i: (i, 0, 0))]
                 + [_full_spec(a) for a in (conv_w, alog, dtb, onorm, bd)],
        out_specs=pl.BlockSpec((1, seq, DN_WIDTH), lambda i: (i, 0, 0)),
        out_shape=jax.ShapeDtypeStruct((b, seq, DN_WIDTH), BF16),
        scratch_shapes=[pltpu.VMEM((SUBLANES + seq, 3 * DN_WIDTH), F32),
                        per_head, per_head, per_head,
                        pltpu.VMEM((seq, LANES), F32), pltpu.VMEM((seq, LANES), F32),
                        per_head,
                        pltpu.VMEM((DN_HEADS, HEAD_DIM, HEAD_DIM), F32)],
        compiler_params=pltpu.CompilerParams(dimension_semantics=("parallel",),
                                             vmem_limit_bytes=VMEM_LIMIT),
        name="deltanet",
    )(dn, ab, conv_w, alog, dtb, onorm, bd)


def _attn_kernel(q_ref, kp_ref, kc_ref, vp_ref, vc_ref, bias_ref, o_ref, lse_ref):
    first = (pl.program_id(2) == 0).astype(jnp.int32)
    lane = lax.broadcasted_iota(jnp.int32, (1, LANES), 1)
    ones = jnp.ones((2 * AT_BLOCK, LANES), BF16)
    for p in range(AT_HEADS // 2):
        sl = slice(p * LANES, (p + 1) * LANES)
        q2 = q_ref[:, sl]
        k2 = jnp.concatenate([kp_ref[:, sl], kc_ref[:, sl]], axis=0)
        v2 = jnp.concatenate([vp_ref[:, sl], vc_ref[:, sl]], axis=0)
        v_ext = jnp.concatenate([v2, ones], axis=1)
        o_pair = None
        lse_pair = None
        for hh in range(2):
            mask = (lane // HEAD_DIM) == hh
            qm = jnp.where(mask, q2, jnp.zeros_like(q2))
            s = _mm_nt(qm, k2) + bias_ref[first, 2 * p + hh]
            m = jnp.max(s, axis=-1, keepdims=True)
            pexp = jnp.exp(s - m).astype(BF16)
            r = _mm(pexp, v_ext)
            denom = r[:, LANES:]
            o_h = r[:, :LANES] / denom
            lse_h = m + jnp.log(denom)
            if hh == 0:
                o_pair, lse_pair = o_h, lse_h
            else:
                o_pair = jnp.where(mask, o_h, o_pair)
                lse_pair = jnp.where(mask, lse_h, lse_pair)
        o_ref[:, sl] = o_pair.astype(BF16)
        lse_ref[:, sl] = lse_pair


def _band_attention(q, k, v, bias):
    b, d, length, w = q.shape
    blk = lambda f: pl.BlockSpec((None, None, AT_BLOCK, w), f)
    cur = lambda bi, r, i: (bi, r, i, 0)
    prev = lambda bi, r, i: (bi, r, jnp.maximum(i - 1, 0), 0)
    return pl.pallas_call(
        _attn_kernel,
        grid=(b, d, length // AT_BLOCK),
        in_specs=[blk(cur), blk(prev), blk(cur), blk(prev), blk(cur), _full_spec(bias)],
        out_specs=[blk(cur), blk(cur)],
        out_shape=[jax.ShapeDtypeStruct(q.shape, BF16), jax.ShapeDtypeStruct(q.shape, F32)],
        compiler_params=pltpu.CompilerParams(
            dimension_semantics=("parallel", "parallel", "arbitrary"),
            vmem_limit_bytes=VMEM_LIMIT),
        name="band_attention",
    )(q, k, k, v, v, bias)


def _t5_bucket(dist):
    max_exact = N_BUCKETS // 2
    d = jnp.maximum(dist, 1).astype(F32)
    log_bucket = max_exact + (jnp.log(d / max_exact) / math.log(MAX_DISTANCE / max_exact)
                              * (N_BUCKETS - max_exact)).astype(jnp.int32)
    return jnp.where(dist < max_exact, dist, jnp.minimum(log_bucket, N_BUCKETS - 1))


def _bias_table(rel_bias, window, dilation):
    n_back = window // dilation
    rel = jnp.arange(AT_BLOCK)[:, None] + AT_BLOCK - jnp.arange(2 * AT_BLOCK)[None, :]
    valid = (rel >= 0) & (rel <= n_back)
    in_block = (jnp.arange(2 * AT_BLOCK) >= AT_BLOCK)[None, :]
    bias = rel_bias.astype(F32)[_t5_bucket(dilation * jnp.clip(rel, 0, n_back))]
    bias = jnp.transpose(bias, (2, 0, 1))
    rest = jnp.where(valid[None], bias, NEG_INF)
    first = jnp.where((valid & in_block)[None], bias, NEG_INF)
    return jnp.stack([rest, first])


def _cv_kernel(cu_ref, dw_ref, dwb_ref, g_ref, b_ref, y_ref, ypad):
    seq = cu_ref.shape[1]
    n_tiles = seq // SEQ_TILE
    pad = 4 * SUBLANES
    ypad[0:pad, :] = jnp.zeros((pad, CV_WIDTH), F32)
    for t in range(n_tiles):
        r0 = t * SEQ_TILE
        u = cu_ref[0, r0:r0 + SEQ_TILE, :].astype(F32)
        ypad[pad + r0:pad + r0 + SEQ_TILE, :] = u[:, :CV_WIDTH] * _sigmoid(u[:, CV_WIDTH:])
    for t in range(n_tiles):
        r0 = t * SEQ_TILE
        acc = jnp.zeros((SEQ_TILE, CV_WIDTH), F32) + dwb_ref[...]
        for j in range(CV_KERNEL):
            off = pad + r0 - (CV_KERNEL - 1) + j
            acc = acc + ypad[off:off + SEQ_TILE, :] * dw_ref[j:j + 1, :]
        mu = jnp.mean(acc, axis=-1, keepdims=True)
        cen = acc - mu
        var = jnp.mean(cen * cen, axis=-1, keepdims=True)
        yn = cen * lax.rsqrt(var + EPS) * g_ref[...] + b_ref[...]
        y_ref[0, r0:r0 + SEQ_TILE, :] = _silu(yn).astype(BF16)


def _conformer_conv(cu, dw, dwb, ln_g, ln_b):
    b, seq, w = cu.shape
    return pl.pallas_call(
        _cv_kernel,
        grid=(b,),
        in_specs=[pl.BlockSpec((1, seq, w), lambda i: (i, 0, 0))]
                 + [_full_spec(a) for a in (dw, dwb, ln_g, ln_b)],
        out_specs=pl.BlockSpec((1, seq, CV_WIDTH), lambda i: (i, 0, 0)),
        out_shape=jax.ShapeDtypeStruct((b, seq, CV_WIDTH), BF16),
        scratch_shapes=[pltpu.VMEM((4 * SUBLANES + seq, CV_WIDTH), F32)],
        compiler_params=pltpu.CompilerParams(dimension_semantics=("parallel",),
                                             vmem_limit_bytes=VMEM_LIMIT),
        name="conformer_conv",
    )(cu, dw, dwb, ln_g, ln_b)


def _out_kernel(x_ref, ydn_ref, o1_ref, o2_ref, o3_ref, l1_ref, l2_ref, l3_ref, ycv_ref,
                wdn_ref, wat_ref, wcv_ref, g_ref, wr_ref, rb_ref,
                xo_ref, h_ref, route_ref):
    l1, l2, l3 = l1_ref[...], l2_ref[...], l3_ref[...]
    m = jnp.maximum(jnp.maximum(l1, l2), l3)
    e1, e2, e3 = jnp.exp(l1 - m), jnp.exp(l2 - m), jnp.exp(l3 - m)
    y_at = (e1 * o1_ref[...].astype(F32) + e2 * o2_ref[...].astype(F32)
            + e3 * o3_ref[...].astype(F32)) / (e1 + e2 + e3)
    x = (x_ref[...] + _mm(ydn_ref[...], wdn_ref[...]) + _mm(y_at.astype(BF16), wat_ref[...])
         + _mm(ycv_ref[...], wcv_ref[...]))
    xo_ref[...] = x
    ms = jnp.mean(x * x, axis=-1, keepdims=True)
    h = x * lax.rsqrt(ms + EPS) * g_ref[...]
    h_ref[...] = h

    logits = _mm(h.astype(BF16), wr_ref[...]) + rb_ref[...]
    lane = lax.broadcasted_iota(jnp.int32, logits.shape, 1)
    is_group = lane < N_GROUPS
    gl = jnp.where(is_group, logits, NEG_INF)
    gmax = jnp.max(gl, axis=-1, keepdims=True)
    gsel = jnp.min(jnp.where(gl == gmax, lane, LANES), axis=-1, keepdims=True)
    p_group = 1.0 / jnp.sum(jnp.where(is_group, jnp.exp(gl - gmax), 0.0), axis=-1, keepdims=True)
    lo = N_GROUPS + gsel * EXPERTS_PER_GROUP
    in_group = (lane >= lo) & (lane < lo + EXPERTS_PER_GROUP)
    el = jnp.where(in_group, logits, NEG_INF)
    v1 = jnp.max(el, axis=-1, keepdims=True)
    i1 = jnp.min(jnp.where(el == v1, lane, LANES), axis=-1, keepdims=True)
    el2 = jnp.where(lane == i1, NEG_INF, el)
    v2 = jnp.max(el2, axis=-1, keepdims=True)
    i2 = jnp.min(jnp.where(el2 == v2, lane, LANES), axis=-1, keepdims=True)
    t = jnp.exp(v2 - v1)
    g1 = p_group / (1.0 + t)
    g2 = g1 * t
    route = jnp.where(lane == 0, (i1 - N_GROUPS).astype(F32),
                      jnp.where(lane == 1, (i2 - N_GROUPS).astype(F32),
                                jnp.where(lane == 2, g1, jnp.where(lane == 3, g2, 0.0))))
    route_ref[...] = route


def _out_proj(x2, ydn, o1, o2, o3, l1, l2, l3, ycv, wdn, wat, wcv, g, wr, rb):
    n, d = x2.shape
    row = lambda w: pl.BlockSpec((ROW_TILE, w), lambda i: (i, 0))
    return pl.pallas_call(
        _out_kernel,
        grid=(n // ROW_TILE,),
        in_specs=[row(d), row(DN_WIDTH)] + [row(AT_WIDTH)] * 6 + [row(CV_WIDTH)]
                 + [_full_spec(a) for a in (wdn, wat, wcv, g, wr, rb)],
        out_specs=[row(d), row(d), row(LANES)],
        out_shape=[jax.ShapeDtypeStruct((n, d), F32), jax.ShapeDtypeStruct((n, d), F32),
                   jax.ShapeDtypeStruct((n, LANES), F32)],
        compiler_params=pltpu.CompilerParams(dimension_semantics=("parallel",),
                                             vmem_limit_bytes=VMEM_LIMIT),
        name="out_proj",
    )(x2, ydn, o1, o2, o3, l1, l2, l3, ycv, wdn, wat, wcv, g, wr, rb)


def _row_copy(src, src_row, dst, dst_row, sem):
    return pltpu.make_async_copy(src.at[pl.ds(src_row, 1), :], dst.at[pl.ds(dst_row, 1), :], sem)


def _dispatch_kernel(dest_ref, h_ref, buf_in_ref, buf_ref, sem):
    del buf_in_ref
    base = pl.program_id(0) * (TOP_K * MOE_TOKENS)

    def issue(j, carry):
        for kk in range(TOP_K):
            _row_copy(h_ref, j, buf_ref, dest_ref[base + TOP_K * j + kk], sem).start()
        return carry

    lax.fori_loop(0, MOE_TOKENS, issue, 0)

    def drain(j, carry):
        _row_copy(h_ref, 0, buf_ref, 0, sem).wait()
        return carry

    lax.fori_loop(0, TOP_K * MOE_TOKENS, drain, 0)


def _dispatch(dest, h, buf0):
    n, d = h.shape
    grid_spec = pltpu.PrefetchScalarGridSpec(
        num_scalar_prefetch=1,
        grid=(n // MOE_TOKENS,),
        in_specs=[pl.BlockSpec((MOE_TOKENS, d), lambda i, dest: (i, 0)),
                  pl.BlockSpec(memory_space=pl.ANY)],
        out_specs=pl.BlockSpec(memory_space=pl.ANY),
        scratch_shapes=[pltpu.SemaphoreType.DMA(())],
    )
    return pl.pallas_call(
        _dispatch_kernel,
        grid_spec=grid_spec,
        out_shape=jax.ShapeDtypeStruct(buf0.shape, buf0.dtype),
        input_output_aliases={2: 0},
        compiler_params=pltpu.CompilerParams(dimension_semantics=("arbitrary",),
                                             vmem_limit_bytes=VMEM_LIMIT),
        name="moe_dispatch",
    )(dest, h, buf0)


def _expert_kernel(be_ref, nu_ref, x_ref, wg_ref, wu_ref, wd_ref, y_ref):
    i = pl.program_id(0)

    @pl.when(i < nu_ref[0])
    def _():
        x = x_ref[...].astype(BF16)
        g = _mm(x, wg_ref[0])
        u = _mm(x, wu_ref[0])
        y_ref[...] = _mm((_silu(g) * u).astype(BF16), wd_ref[0])

    @pl.when(i >= nu_ref[0])
    def _():
        y_ref[...] = jnp.zeros(y_ref.shape, F32)


def _experts(block_expert, n_used, buf, wg, wu, wd):
    ns, d = buf.shape
    de = wg.shape[2]
    grid_spec = pltpu.PrefetchScalarGridSpec(
        num_scalar_prefetch=2,
        grid=(ns // MOE_BLOCK,),
        in_specs=[pl.BlockSpec((MOE_BLOCK, d), lambda i, be, nu: (i, 0)),
                  pl.BlockSpec((1, d, de), lambda i, be, nu: (be[i], 0, 0)),
                  pl.BlockSpec((1, d, de), lambda i, be, nu: (be[i], 0, 0)),
                  pl.BlockSpec((1, de, d), lambda i, be, nu: (be[i], 0, 0))],
        out_specs=pl.BlockSpec((MOE_BLOCK, d), lambda i, be, nu: (i, 0)),
    )
    return pl.pallas_call(
        _expert_kernel,
        grid_spec=grid_spec,
        out_shape=jax.ShapeDtypeStruct((ns, d), F32),
        compiler_params=pltpu.CompilerParams(dimension_semantics=("arbitrary",),
                                             vmem_limit_bytes=VMEM_LIMIT),
        name="moe_experts",
    )(block_expert, n_used, buf, wg, wu, wd)


def _combine_kernel(dest_ref, x_ref, route_ref, y_ref, o_ref, ybuf, sem):
    base = pl.program_id(0) * (TOP_K * MOE_TOKENS)

    def issue(j, carry):
        for kk in range(TOP_K):
            _row_copy(y_ref, dest_ref[base + TOP_K * j + kk], ybuf.at[kk], j, sem).start()
        return carry

    lax.fori_loop(0, MOE_TOKENS, issue, 0)

    def drain(j, carry):
        _row_copy(y_ref, 0, ybuf.at[0], 0, sem).wait()
        return carry

    lax.fori_loop(0, TOP_K * MOE_TOKENS, drain, 0)
    route = route_ref[...]
    o_ref[...] = x_ref[...] + route[:, 2:3] * ybuf[0] + route[:, 3:4] * ybuf[1]


def _combine(dest, x2, route, y):
    n, d = x2.shape
    grid_spec = pltpu.PrefetchScalarGridSpec(
        num_scalar_prefetch=1,
        grid=(n // MOE_TOKENS,),
        in_specs=[pl.BlockSpec((MOE_TOKENS, d), lambda i, dest: (i, 0)),
                  pl.BlockSpec((MOE_TOKENS, LANES), lambda i, dest: (i, 0)),
                  pl.BlockSpec(memory_space=pl.ANY)],
        out_specs=pl.BlockSpec((MOE_TOKENS, d), lambda i, dest: (i, 0)),
        scratch_shapes=[pltpu.VMEM((TOP_K, MOE_TOKENS, d), F32), pltpu.SemaphoreType.DMA(())],
    )
    return pl.pallas_call(
        _combine_kernel,
        grid_spec=grid_spec,
        out_shape=jax.ShapeDtypeStruct((n, d), F32),
        compiler_params=pltpu.CompilerParams(dimension_semantics=("arbitrary",),
                                             vmem_limit_bytes=VMEM_LIMIT),
        name="moe_combine",
    )(dest, x2, route, y)


def _moe_plan(route, n_tok):
    flat_e = route[:, :TOP_K].astype(jnp.int32).reshape(-1)
    onehot = (flat_e[:, None] == jnp.arange(N_EXPERTS)[None, :]).astype(jnp.int32)
    csum = jnp.cumsum(onehot, axis=0)
    rank = jnp.sum(csum * onehot, axis=1) - 1
    counts = csum[-1]
    padded = (counts + MOE_BLOCK - 1) // MOE_BLOCK * MOE_BLOCK
    pad_end = jnp.cumsum(padded)
    pad_start = pad_end - padded
    dest = (pad_start[flat_e] + rank).astype(jnp.int32)
    n_blocks = n_tok * TOP_K // MOE_BLOCK + N_EXPERTS
    block_expert = jnp.minimum(
        jnp.searchsorted(pad_end, jnp.arange(n_blocks) * MOE_BLOCK, side='right'),
        N_EXPERTS - 1).astype(jnp.int32)
    n_used = (pad_end[-1:] // MOE_BLOCK).astype(jnp.int32)
    return dest, block_expert, n_used, n_blocks


def _pad_lanes(a, width=LANES):
    return jnp.pad(a, [(0, 0)] * (a.ndim - 1) + [(0, width - a.shape[-1])])


def kernel(x, norm_mix, w_in, dn_conv, dn_a_log, dn_dt_bias, dn_out_norm, at_q_norm, at_k_norm,
           rel_bias, cv_dw, cv_dw_bias, cv_ln_g, cv_ln_b, w_out, norm_ffn, router_group_w,
           router_group_b, router_expert_w, router_expert_b, ex_gate, ex_up, ex_down):
    bsz, seq, d = x.shape
    n_tok = bsz * seq
    depth = w_in.shape[0]
    c_ab = 4 * DN_WIDTH
    c_at = c_ab + 2 * DN_HEADS
    c_cv = c_at + 3 * AT_WIDTH
    bd_at = _block_diag_ones(AT_WIDTH, HEAD_DIM, BF16)
    bd_dn = _block_diag_ones(DN_WIDTH, HEAD_DIM, BF16)
    biases = [_bias_table(rel_bias, w, dil) for w, dil in WINDOWS]

    x2 = x.reshape(n_tok, d)
    for layer in range(depth):
        w_l = w_in[layer]
        dn, aq, ak, av, cu, ab = _proj(
            x2, norm_mix[layer][None, :],
            w_l[:, :c_ab].astype(BF16), w_l[:, c_at:c_cv].astype(BF16), w_l[:, c_cv:].astype(BF16),
            _pad_lanes(w_l[:, c_ab:c_at]).astype(BF16), bd_at,
            jnp.tile(at_q_norm[layer], AT_HEADS)[None, :] * (HEAD_DIM ** -0.5),
            jnp.tile(at_k_norm[layer], AT_HEADS)[None, :])

        y_dn = _deltanet(dn.reshape(bsz, seq, -1), ab.reshape(bsz, seq, LANES), dn_conv[layer],
                         _pad_lanes(dn_a_log[layer][None, :]), _pad_lanes(dn_dt_bias[layer][None, :]),
                         dn_out_norm[layer][None, :], bd_dn)

        outs, lses = [], []
        for (window, dil), bias in zip(WINDOWS, biases):
            def streams(t):
                return t.reshape(bsz, seq // dil, dil, AT_WIDTH).transpose(0, 2, 1, 3)
            o_g, lse_g = _band_attention(streams(aq), streams(ak), streams(av), bias)
            outs.append(o_g.transpose(0, 2, 1, 3).reshape(n_tok, AT_WIDTH))
            lses.append(lse_g.transpose(0, 2, 1, 3).reshape(n_tok, AT_WIDTH))

        y_cv = _conformer_conv(cu.reshape(bsz, seq, -1), cv_dw[layer], cv_dw_bias[layer][None, :],
                               cv_ln_g[layer][None, :], cv_ln_b[layer][None, :])

        wo = w_out[layer].astype(BF16)
        w_r = _pad_lanes(jnp.concatenate([router_group_w[layer], router_expert_w[layer]], axis=1))
        b_r = _pad_lanes(jnp.concatenate([router_group_b[layer], router_expert_b[layer]])[None, :])
        x_mid, h_ffn, route = _out_proj(
            x2, y_dn.reshape(n_tok, DN_WIDTH), outs[0], outs[1], outs[2], lses[0], lses[1], lses[2],
            y_cv.reshape(n_tok, CV_WIDTH), wo[:DN_WIDTH], wo[DN_WIDTH:DN_WIDTH + AT_WIDTH],
            wo[DN_WIDTH + AT_WIDTH:], norm_ffn[layer][None, :], w_r.astype(BF16), b_r)

        dest, block_expert, n_used, n_blocks = _moe_plan(route, n_tok)
        buf = _dispatch(dest, h_ffn, jnp.zeros((n_blocks * MOE_BLOCK, d), F32))
        y = _experts(block_expert, n_used, buf, ex_gate[layer].astype(BF16),
                     ex_up[layer].astype(BF16), ex_down[layer].astype(BF16))
        x2 = _combine(dest, x_mid, route, y)
    return x2.reshape(bsz, seq, d)
```

```python
import math

import jax
import jax.numpy as jnp
from jax import lax
from jax.experimental import pallas as pl
from jax.experimental.pallas import tpu as pltpu

F32 = jnp.float32
BF16 = jnp.bfloat16
HIGHEST = lax.Precision.HIGHEST

EPS = 1e-6
NEG_INF = -1e30

HEAD_DIM = 64
DN_HEADS = 4
DN_WIDTH = DN_HEADS * HEAD_DIM
DN_CONV = 4
DN_CHUNK = 64
AT_HEADS = 8
AT_WIDTH = AT_HEADS * HEAD_DIM
AT_BLOCK = 128
WINDOWS = ((128, 1), (512, 4), (2048, 16))
N_BUCKETS = 32
MAX_DISTANCE = 2048
CV_WIDTH = 256
CV_KERNEL = 31
N_GROUPS = 4
EXPERTS_PER_GROUP = 8
N_EXPERTS = N_GROUPS * EXPERTS_PER_GROUP
TOP_K = 2

LANES = 128
SUBLANES = 8
VMEM_LIMIT = 52 * 1024 * 1024

ROW_TILE = 512
SEQ_TILE = 256
DN_TILE = 64
DN_PREP_CHUNKS = 4
MOE_BLOCK = 256
MOE_TOKENS = 256


def _mm(a, b, precision=None):
    return jnp.dot(a, b, preferred_element_type=F32, precision=precision)


def _mm_nt(a, b):
    return lax.dot_general(a, b, (((1,), (1,)), ((), ())), preferred_element_type=F32)


def _mm_tn(a, b):
    return lax.dot_general(a, b, (((0,), (0,)), ((), ())), preferred_element_type=F32)


def _sigmoid(x):
    return 1.0 / (1.0 + jnp.exp(-x))


def _silu(x):
    return x * _sigmoid(x)


def _split3(x):
    p0 = x.astype(BF16)
    r1 = x - p0.astype(F32)
    p1 = r1.astype(BF16)
    p2 = (r1 - p1.astype(F32)).astype(BF16)
    return p0, p1, p2


def _full_spec(a):
    nd = a.ndim
    return pl.BlockSpec(a.shape, lambda *_: (0,) * nd)


def _block_diag_ones(width, block, dtype):
    r = jnp.arange(width)[:, None] // block
    c = jnp.arange(width)[None, :] // block
    return (r == c).astype(dtype)


def _proj_kernel(x_ref, g_ref, wdn_ref, wat_ref, wcv_ref, wab_ref, bd_ref, qn_ref, kn_ref,
                 dn_ref, aq_ref, ak_ref, av_ref, cv_ref, ab_ref):
    x = x_ref[...]
    ms = jnp.mean(x * x, axis=-1, keepdims=True)
    h = (x * lax.rsqrt(ms + EPS) * g_ref[...]).astype(BF16)
    dn_ref[...] = _mm(h, wdn_ref[...]).astype(BF16)
    cv_ref[...] = _mm(h, wcv_ref[...]).astype(BF16)
    ab_ref[...] = _mm(h, wab_ref[...])
    at = _mm(h, wat_ref[...])
    q = at[:, 0:AT_WIDTH]
    k = at[:, AT_WIDTH:2 * AT_WIDTH]
    bd = bd_ref[...]
    qms = _mm((q * q).astype(BF16), bd) * (1.0 / HEAD_DIM)
    kms = _mm((k * k).astype(BF16), bd) * (1.0 / HEAD_DIM)
    aq_ref[...] = (q * lax.rsqrt(qms + EPS) * qn_ref[...]).astype(BF16)
    ak_ref[...] = (k * lax.rsqrt(kms + EPS) * kn_ref[...]).astype(BF16)
    av_ref[...] = at[:, 2 * AT_WIDTH:3 * AT_WIDTH].astype(BF16)


def _proj(x2, g, wdn, wat, wcv, wab, bd, qn, kn):
    n, d = x2.shape
    row = lambda w: pl.BlockSpec((ROW_TILE, w), lambda i: (i, 0))
    widths = (wdn.shape[1], AT_WIDTH, AT_WIDTH, AT_WIDTH, wcv.shape[1], LANES)
    dtypes = (BF16, BF16, BF16, BF16, BF16, F32)
    return pl.pallas_call(
        _proj_kernel,
        grid=(n // ROW_TILE,),
        in_specs=[row(d)] + [_full_spec(a) for a in (g, wdn, wat, wcv, wab, bd, qn, kn)],
        out_specs=[row(w) for w in widths],
        out_shape=[jax.ShapeDtypeStruct((n, w), t) for w, t in zip(widths, dtypes)],
        compiler_params=pltpu.CompilerParams(dimension_semantics=("parallel",),
                                             vmem_limit_bytes=VMEM_LIMIT),
        name="proj",
    )(x2, g, wdn, wat, wcv, wab, bd, qn, kn)


def _per_head(x, block_mask):
    return jnp.where(block_mask, jnp.concatenate([x] * DN_HEADS, axis=0), jnp.zeros((), x.dtype))


def _dn_kernel(dn_ref, ab_ref, cw_ref, alog_ref, dtb_ref, onorm_ref, bd_ref, y_ref,
               xpad, qs, ks, vs, gs, bs, ws, us, qks, qds, kds, os_, st):
    seq = dn_ref.shape[1]
    n_tiles = seq // DN_TILE
    cw3 = 3 * DN_WIDTH
    pad = SUBLANES
    c = DN_CHUNK
    bd = bd_ref[...]
    hid = lax.broadcasted_iota(jnp.int32, (1, DN_WIDTH), 1) // HEAD_DIM

    def expand(cols, first):
        out = cols[:, first + DN_HEADS - 1:first + DN_HEADS]
        for h in range(DN_HEADS - 2, -1, -1):
            out = jnp.where(hid == h, cols[:, first + h:first + h + 1], out)
        return out

    xpad[0:pad, :] = jnp.zeros((pad, cw3), F32)
    for t in range(n_tiles):
        r0 = t * DN_TILE
        xpad[pad + r0:pad + r0 + DN_TILE, :] = dn_ref[0, r0:r0 + DN_TILE, 0:cw3].astype(F32)
    for t in range(n_tiles):
        r0 = t * DN_TILE
        ab = ab_ref[0, r0:r0 + DN_TILE, :]
        sp_in = expand(ab, 0) + dtb_ref[...]
        softplus = jnp.maximum(sp_in, 0.0) + jnp.log(1.0 + jnp.exp(-jnp.abs(sp_in)))
        gs[r0:r0 + DN_TILE, :] = -jnp.exp(alog_ref[...]) * softplus
        bs[r0:r0 + DN_TILE, :] = _sigmoid(expand(ab, DN_HEADS))
        acc = jnp.zeros((DN_TILE, cw3), F32)
        for j in range(DN_CONV):
            off = pad + r0 - (DN_CONV - 1) + j
            acc = acc + xpad[off:off + DN_TILE, :] * cw_ref[j:j + 1, :]
        y = _silu(acc)
        q = y[:, 0:DN_WIDTH]
        k = y[:, DN_WIDTH:2 * DN_WIDTH]
        qss = _mm((q * q).astype(BF16), bd)
        kss = _mm((k * k).astype(BF16), bd)
        qs[r0:r0 + DN_TILE, :] = q * lax.rsqrt(qss + EPS) * (HEAD_DIM ** -0.5)
        ks[r0:r0 + DN_TILE, :] = k * lax.rsqrt(kss + EPS)
        vs[r0:r0 + DN_TILE, :] = y[:, 2 * DN_WIDTH:3 * DN_WIDTH]

    ri = lax.broadcasted_iota(jnp.int32, (c, DN_WIDTH), 0)
    ci = lax.broadcasted_iota(jnp.int32, (c, DN_WIDTH), 1) % HEAD_DIM
    causal = ri >= ci
    strict = ri > ci
    eye_cat = (ri == ci).astype(F32)
    r2 = lax.broadcasted_iota(jnp.int32, (c, c), 0)
    c2 = lax.broadcasted_iota(jnp.int32, (c, c), 1)
    lower_ones = (r2 >= c2).astype(BF16)
    all_ones = jnp.ones((c, c), BF16)
    block_mask = (lax.broadcasted_iota(jnp.int32, (DN_WIDTH, DN_WIDTH), 0) // HEAD_DIM
                  == lax.broadcasted_iota(jnp.int32, (DN_WIDTH, DN_WIDTH), 1) // HEAD_DIM)

    def mm_exact_rhs(lhs_bf, x):
        p0, p1, p2 = _split3(x)
        return _mm(lhs_bf, p0) + _mm(lhs_bf, p1) + _mm(lhs_bf, p2)

    def mm_bd(lhs, rhs_cat):
        return _mm(lhs.astype(BF16), _per_head(rhs_cat.astype(BF16), block_mask))

    def prep(m, carry):
        rows = [pl.multiple_of((DN_PREP_CHUNKS * m + cc) * c, c) for cc in range(DN_PREP_CHUNKS)]
        g_cum = [mm_exact_rhs(lower_ones, gs[pl.ds(r, c), :]) for r in rows]
        g_row = [mm_exact_rhs(all_ones, g * eye_cat) for g in g_cum]
        decay = [jnp.exp(jnp.where(causal, g - gr, NEG_INF)) for g, gr in zip(g_cum, g_row)]
        kc = [ks[pl.ds(r, c), :] for r in rows]
        qc = [qs[pl.ds(r, c), :] for r in rows]
        beta = [bs[pl.ds(r, c), :] for r in rows]
        kb = [k * b for k, b in zip(kc, beta)]
        aq = [_mm_nt(jnp.concatenate([b_, q_], axis=0).astype(BF16),
                     _per_head(k_.astype(BF16), block_mask))
              for b_, q_, k_ in zip(kb, qc, kc)]
        a_mat = [jnp.where(strict, x[:c] * d, 0.0) for x, d in zip(aq, decay)]
        qk = [x[c:] * d for x, d in zip(aq, decay)]
        p = [eye_cat - a for a in a_mat]
        pw = [mm_bd(a, a) for a in a_mat]
        for _ in range(4):
            both = [mm_bd(jnp.concatenate([p_, x], axis=0), x) for p_, x in zip(p, pw)]
            p = [p_ + b_[:c] for p_, b_ in zip(p, both)]
            pw = [b_[c:] for b_ in both]
        p = [p_ + mm_bd(p_, x) for p_, x in zip(p, pw)]
        for i, r in enumerate(rows):
            e_g = jnp.exp(g_cum[i])
            ws[pl.ds(r, c), :] = mm_bd(p[i], kb[i] * e_g).astype(BF16)
            us[pl.ds(r, c), :] = mm_bd(p[i], vs[pl.ds(r, c), :] * beta[i])
            qks[pl.ds(r, c), :] = qk[i].astype(BF16)
            qds[pl.ds(r, c), :] = (qc[i] * e_g).astype(BF16)
            kds[pl.ds(r, c), :] = (kc[i] * jnp.exp(g_cum[i][c - 1:c, :] - g_cum[i])).astype(BF16)
            gs[pl.ds(r, c), :] = g_cum[i]
        return carry

    lax.fori_loop(0, seq // (DN_PREP_CHUNKS * c), prep, 0)

    st[...] = jnp.zeros(st.shape, F32)

    def scan(n, carry):
        r = pl.multiple_of(n * c, c)
        state = st[...]
        both = _mm(jnp.concatenate([ws[pl.ds(r, c), :], qds[pl.ds(r, c), :]], axis=0),
                   _per_head(state.astype(BF16), block_mask))
        v_new = (us[pl.ds(r, c), :] - both[:c]).astype(BF16)
        os_[pl.ds(r, c), :] = both[c:] + _mm(qks[pl.ds(r, c), :], _per_head(v_new, block_mask))
        kv = _mm_tn(kds[pl.ds(r, c), :], v_new)
        upd = kv[(DN_HEADS - 1) * c:DN_HEADS * c, :]
        for h in range(DN_HEADS - 2, -1, -1):
            upd = jnp.where(hid == h, kv[h * c:(h + 1) * c, :], upd)
        g_last = gs[pl.ds(r + c - SUBLANES, SUBLANES), :][SUBLANES - 1:SUBLANES, :]
        st[...] = state * jnp.exp(g_last) + upd
        return carry

    lax.fori_loop(0, seq // c, scan, 0)

    for t in range(n_tiles):
        r0 = t * DN_TILE
        o = os_[r0:r0 + DN_TILE, :]
        z = dn_ref[0, r0:r0 + DN_TILE, cw3:cw3 + DN_WIDTH].astype(F32)
        ms = _mm((o * o).astype(BF16), bd) * (1.0 / HEAD_DIM)
        y_ref[0, r0:r0 + DN_TILE, :] = (o * lax.rsqrt(ms + EPS) * onorm_ref[...] * _silu(z)).astype(BF16)


def _deltanet(dn, ab, conv_w, alog_cat, dtb_cat, onorm_cat, bd):
    b, seq, w = dn.shape
    f32buf = pltpu.VMEM((seq, DN_WIDTH), F32)
    bf16buf = pltpu.VMEM((seq, DN_WIDTH), BF16)
    return pl.pallas_call(
        _dn_kernel,
        grid=(b,),
        in_specs=[pl.BlockSpec((1, seq, w), lambda i: (i, 0, 0)),
                  pl.BlockSpec((1, seq, LANES), lambda i: (i, 0, 0))]
                 + [_full_spec(a) for a in (conv_w, alog_cat, dtb_cat, onorm_cat, bd)],
        out_specs=pl.BlockSpec((1, seq, DN_WIDTH), lambda i: (i, 0, 0)),
        out_shape=jax.ShapeDtypeStruct((b, seq, DN_WIDTH), BF16),
        scratch_shapes=[pltpu.VMEM((SUBLANES + seq, 3 * DN_WIDTH), F32),
                        f32buf, f32buf, f32buf, f32buf, f32buf,
                        bf16buf, f32buf, bf16buf, bf16buf, bf16buf,
                        f32buf,
                        pltpu.VMEM((DN_CHUNK, DN_WIDTH), F32)],
        compiler_params=pltpu.CompilerParams(dimension_semantics=("parallel",),
                                             vmem_limit_bytes=VMEM_LIMIT),
        name="deltanet",
    )(dn, ab, conv_w, alog_cat, dtb_cat, onorm_cat, bd)


def _attn_kernel(q_ref, k_ref, v_ref, bias_ref, y_ref, qf, kf, vf, acc, ms, ls):
    seq = q_ref.shape[1]
    pairs = AT_HEADS // 2
    for t in range(seq // SEQ_TILE):
        r0 = t * SEQ_TILE
        for p in range(pairs):
            sl = slice(p * LANES, (p + 1) * LANES)
            qf[p, r0:r0 + SEQ_TILE, :] = q_ref[0, r0:r0 + SEQ_TILE, sl].astype(F32)
            kf[p, r0:r0 + SEQ_TILE, :] = k_ref[0, r0:r0 + SEQ_TILE, sl].astype(F32)
            vf[p, r0:r0 + SEQ_TILE, :] = v_ref[0, r0:r0 + SEQ_TILE, sl].astype(F32)

    lane = lax.broadcasted_iota(jnp.int32, (1, LANES), 1)
    key_col = lax.broadcasted_iota(jnp.int32, (1, 2 * AT_BLOCK), 1)
    order = sorted(range(len(WINDOWS)), key=lambda g: -WINDOWS[g][1])
    for step, grp in enumerate(order):
        dil = WINDOWS[grp][1]
        nb = seq // dil // AT_BLOCK
        is_first = step == 0
        is_last = step == len(order) - 1

        def rows(start, dil=dil):
            return pl.ds(start, AT_BLOCK, stride=dil) if dil > 1 else pl.ds(start, AT_BLOCK)

        def unit(u, carry, dil=dil, nb=nb, grp=grp, is_first=is_first, is_last=is_last, rows=rows):
            r = u // nb
            i = u % nb
            q_rows = rows(r + dil * AT_BLOCK * i)
            p_rows = rows(r + dil * AT_BLOCK * jnp.maximum(i - 1, 0))
            no_prev = jnp.where(jnp.logical_and(i == 0, key_col < AT_BLOCK), NEG_INF, 0.0)
            for p in range(pairs):
                q2 = qf[p, q_rows, :].astype(BF16)
                if nb > 1:
                    k2 = jnp.concatenate([kf[p, p_rows, :], kf[p, q_rows, :]], axis=0).astype(BF16)
                    v2 = jnp.concatenate([vf[p, p_rows, :], vf[p, q_rows, :]], axis=0).astype(BF16)
                else:
                    k2 = kf[p, q_rows, :].astype(BF16)
                    v2 = vf[p, q_rows, :].astype(BF16)
                v_ext = jnp.concatenate([v2, jnp.ones(v2.shape, BF16)], axis=1)
                m_new = l_new = o_new = None
                for hh in range(2):
                    mask = (lane // HEAD_DIM) == hh
                    qm = jnp.where(mask, q2, jnp.zeros_like(q2))
                    if nb > 1:
                        s = _mm_nt(qm, k2) + bias_ref[grp, 2 * p + hh] + no_prev
                    else:
                        s = _mm_nt(qm, k2) + bias_ref[grp, 2 * p + hh, :, AT_BLOCK:]
                    m_h = jnp.max(s, axis=-1, keepdims=True)
                    pexp = jnp.exp(s - m_h).astype(BF16)
                    res = _mm(pexp, v_ext)
                    m_b = jnp.broadcast_to(m_h, (AT_BLOCK, LANES))
                    if hh == 0:
                        m_new, o_new, l_new = m_b, res[:, :LANES], res[:, LANES:]
                    else:
                        m_new = jnp.where(mask, m_b, m_new)
                        o_new = jnp.where(mask, res[:, :LANES], o_new)
                        l_new = jnp.where(mask, res[:, LANES:], l_new)
                if not is_first:
                    m_old = ms[p, q_rows, :]
                    m_tot = jnp.maximum(m_old, m_new)
                    a_old = jnp.exp(m_old - m_tot)
                    a_new = jnp.exp(m_new - m_tot)
                    l_new = a_old * ls[p, q_rows, :] + a_new * l_new
                    o_new = a_old * acc[p, q_rows, :] + a_new * o_new
                    m_new = m_tot
                if is_last:
                    y_ref[0, pl.ds(pl.multiple_of(u * AT_BLOCK, AT_BLOCK), AT_BLOCK),
                          p * LANES:(p + 1) * LANES] = (o_new / l_new).astype(BF16)
                else:
                    ms[p, q_rows, :] = m_new
                    ls[p, q_rows, :] = l_new
                    acc[p, q_rows, :] = o_new
            return carry

        lax.fori_loop(0, seq // AT_BLOCK, unit, 0)


def _attention(aq, ak, av, bias):
    b, seq, w = aq.shape
    assert WINDOWS[0][1] == 1, "the last group processed must be the undilated one (token order)"
    slab = pltpu.VMEM((AT_HEADS // 2, seq, LANES), F32)
    tok = pl.BlockSpec((1, seq, w), lambda i: (i, 0, 0))
    return pl.pallas_call(
        _attn_kernel,
        grid=(b,),
        in_specs=[tok, tok, tok, _full_spec(bias)],
        out_specs=tok,
        out_shape=jax.ShapeDtypeStruct((b, seq, w), BF16),
        scratch_shapes=[slab] * 6,
        compiler_params=pltpu.CompilerParams(dimension_semantics=("parallel",),
                                             vmem_limit_bytes=VMEM_LIMIT),
        name="dilated_attention",
    )(aq, ak, av, bias)


def _t5_bucket(dist):
    max_exact = N_BUCKETS // 2
    d = jnp.maximum(dist, 1).astype(F32)
    log_bucket = max_exact + (jnp.log(d / max_exact) / math.log(MAX_DISTANCE / max_exact)
                              * (N_BUCKETS - max_exact)).astype(jnp.int32)
    return jnp.where(dist < max_exact, dist, jnp.minimum(log_bucket, N_BUCKETS - 1))


def _bias_tables(rel_bias):
    rel = jnp.arange(AT_BLOCK)[:, None] + AT_BLOCK - jnp.arange(2 * AT_BLOCK)[None, :]
    tabs = []
    for window, dil in WINDOWS:
        n_back = window // dil
        valid = (rel >= 0) & (rel <= n_back)
        bucket = _t5_bucket(dil * jnp.clip(rel, 0, n_back))
        onehot = (bucket[..., None] == jnp.arange(N_BUCKETS)).astype(F32)
        bias = jnp.einsum('qkn,nh->hqk', onehot, rel_bias.astype(F32), precision=HIGHEST)
        tabs.append(jnp.where(valid[None], bias, NEG_INF))
    return jnp.stack(tabs)


def _cv_kernel(cu_ref, dw_ref, dwb_ref, g_ref, b_ref, y_ref, ypad):
    seq = cu_ref.shape[1]
    n_tiles = seq // SEQ_TILE
    pad = 4 * SUBLANES
    ypad[0:pad, :] = jnp.zeros((pad, CV_WIDTH), F32)
    for t in range(n_tiles):
        r0 = t * SEQ_TILE
        u = cu_ref[0, r0:r0 + SEQ_TILE, :].astype(F32)
        ypad[pad + r0:pad + r0 + SEQ_TILE, :] = u[:, :CV_WIDTH] * _sigmoid(u[:, CV_WIDTH:])
    for t in range(n_tiles):
        r0 = t * SEQ_TILE
        acc = jnp.zeros((SEQ_TILE, CV_WIDTH), F32) + dwb_ref[...]
        for j in range(CV_KERNEL):
            off = pad + r0 - (CV_KERNEL - 1) + j
            acc = acc + ypad[off:off + SEQ_TILE, :] * dw_ref[j:j + 1, :]
        mu = jnp.mean(acc, axis=-1, keepdims=True)
        cen = acc - mu
        var = jnp.mean(cen * cen, axis=-1, keepdims=True)
        yn = cen * lax.rsqrt(var + EPS) * g_ref[...] + b_ref[...]
        y_ref[0, r0:r0 + SEQ_TILE, :] = _silu(yn).astype(BF16)


def _conformer_conv(cu, dw, dwb, ln_g, ln_b):
    b, seq, w = cu.shape
    return pl.pallas_call(
        _cv_kernel,
        grid=(b,),
        in_specs=[pl.BlockSpec((1, seq, w), lambda i: (i, 0, 0))]
                 + [_full_spec(a) for a in (dw, dwb, ln_g, ln_b)],
        out_specs=pl.BlockSpec((1, seq, CV_WIDTH), lambda i: (i, 0, 0)),
        out_shape=jax.ShapeDtypeStruct((b, seq, CV_WIDTH), BF16),
        scratch_shapes=[pltpu.VMEM((4 * SUBLANES + seq, CV_WIDTH), F32)],
        compiler_params=pltpu.CompilerParams(dimension_semantics=("parallel",),
                                             vmem_limit_bytes=VMEM_LIMIT),
        name="conformer_conv",
    )(cu, dw, dwb, ln_g, ln_b)


def _out_kernel(x_ref, ydn_ref, yat_ref, ycv_ref, wdn_ref, wat_ref, wcv_ref, g_ref, wr_ref, rb_ref,
                tri_ref, xo_ref, h_ref, route_ref, cnt_ref):
    x = (x_ref[...] + _mm(ydn_ref[...], wdn_ref[...]) + _mm(yat_ref[...], wat_ref[...])
         + _mm(ycv_ref[...], wcv_ref[...]))
    xo_ref[...] = x
    ms = jnp.mean(x * x, axis=-1, keepdims=True)
    h = x * lax.rsqrt(ms + EPS) * g_ref[...]
    h_ref[...] = h

    logits = _mm(h.astype(BF16), wr_ref[...]) + rb_ref[...]
    lane = lax.broadcasted_iota(jnp.int32, logits.shape, 1)
    is_group = (lane >= N_EXPERTS) & (lane < N_EXPERTS + N_GROUPS)
    gl = jnp.where(is_group, logits, NEG_INF)
    gmax = jnp.max(gl, axis=-1, keepdims=True)
    gsel = jnp.min(jnp.where(gl == gmax, lane, 2 * LANES), axis=-1, keepdims=True) - N_EXPERTS
    p_group = 1.0 / jnp.sum(jnp.where(is_group, jnp.exp(gl - gmax), 0.0), axis=-1, keepdims=True)
    lo = gsel * EXPERTS_PER_GROUP
    in_group = (lane >= lo) & (lane < lo + EXPERTS_PER_GROUP)
    el = jnp.where(in_group, logits, NEG_INF)
    v1 = jnp.max(el, axis=-1, keepdims=True)
    i1 = jnp.min(jnp.where(el == v1, lane, LANES), axis=-1, keepdims=True)
    el2 = jnp.where(lane == i1, NEG_INF, el)
    v2 = jnp.max(el2, axis=-1, keepdims=True)
    i2 = jnp.min(jnp.where(el2 == v2, lane, LANES), axis=-1, keepdims=True)
    t = jnp.exp(v2 - v1)
    g1 = p_group / (1.0 + t)
    g2 = g1 * t

    oh1 = lane == i1
    oh2 = lane == i2
    both = jnp.where(oh1 | oh2, 1.0, 0.0).astype(BF16)
    before = _mm(tri_ref[...], both)
    rank1 = jnp.sum(jnp.where(oh1, before, 0.0), axis=-1, keepdims=True)
    rank2 = jnp.sum(jnp.where(oh2, before, 0.0), axis=-1, keepdims=True)
    cnt_ref[0] = _mm(jnp.ones((SUBLANES, both.shape[0]), BF16), both)

    route = jnp.where(lane == 0, i1.astype(F32), 0.0)
    route = jnp.where(lane == 1, i2.astype(F32), route)
    route = jnp.where(lane == 2, g1, route)
    route = jnp.where(lane == 3, g2, route)
    route = jnp.where(lane == 4, rank1, route)
    route = jnp.where(lane == 5, rank2, route)
    route_ref[...] = route


def _out_proj(x2, ydn, yat, ycv, wdn, wat, wcv, g, wr, rb, tri):
    n, d = x2.shape
    row = lambda w: pl.BlockSpec((ROW_TILE, w), lambda i: (i, 0))
    n_tiles = n // ROW_TILE
    return pl.pallas_call(
        _out_kernel,
        grid=(n_tiles,),
        in_specs=[row(d), row(DN_WIDTH), row(AT_WIDTH), row(CV_WIDTH)]
                 + [_full_spec(a) for a in (wdn, wat, wcv, g, wr, rb, tri)],
        out_specs=[row(d), row(d), row(LANES), pl.BlockSpec((1, SUBLANES, LANES), lambda i: (i, 0, 0))],
        out_shape=[jax.ShapeDtypeStruct((n, d), F32), jax.ShapeDtypeStruct((n, d), F32),
                   jax.ShapeDtypeStruct((n, LANES), F32),
                   jax.ShapeDtypeStruct((n_tiles, SUBLANES, LANES), F32)],
        compiler_params=pltpu.CompilerParams(dimension_semantics=("parallel",),
                                             vmem_limit_bytes=VMEM_LIMIT),
        name="out_proj",
    )(x2, ydn, yat, ycv, wdn, wat, wcv, g, wr, rb, tri)


def _row_copy(src, src_row, dst, dst_row, sem):
    return pltpu.make_async_copy(src.at[pl.ds(src_row, 1), :], dst.at[pl.ds(dst_row, 1), :], sem)


def _dispatch_kernel(dest_ref, h_ref, buf_in_ref, buf_ref, sem):
    del buf_in_ref
    base = pl.program_id(0) * (TOP_K * MOE_TOKENS)

    def issue(j, carry):
        for kk in range(TOP_K):
            _row_copy(h_ref, j, buf_ref, dest_ref[base + TOP_K * j + kk], sem).start()
        return carry

    lax.fori_loop(0, MOE_TOKENS, issue, 0)

    def drain(j, carry):
        _row_copy(h_ref, 0, buf_ref, 0, sem).wait()
        return carry

    lax.fori_loop(0, TOP_K * MOE_TOKENS, drain, 0)


def _dispatch(dest, h, buf0):
    n, d = h.shape
    grid_spec = pltpu.PrefetchScalarGridSpec(
        num_scalar_prefetch=1,
        grid=(n // MOE_TOKENS,),
        in_specs=[pl.BlockSpec((MOE_TOKENS, d), lambda i, dest: (i, 0)),
                  pl.BlockSpec(memory_space=pl.ANY)],
        out_specs=pl.BlockSpec(memory_space=pl.ANY),
        scratch_shapes=[pltpu.SemaphoreType.DMA(())],
    )
    return pl.pallas_call(
        _dispatch_kernel,
        grid_spec=grid_spec,
        out_shape=jax.ShapeDtypeStruct(buf0.shape, buf0.dtype),
        input_output_aliases={2: 0},
        compiler_params=pltpu.CompilerParams(dimension_semantics=("arbitrary",),
                                             vmem_limit_bytes=VMEM_LIMIT),
        name="moe_dispatch",
    )(dest, h, buf0)


def _expert_kernel(be_ref, nu_ref, x_ref, wg_ref, wu_ref, wd_ref, y_ref, wg_bf, wu_bf, wd_bf):
    i = pl.program_id(0)
    changed = jnp.logical_or(i == 0, be_ref[i] != be_ref[jnp.maximum(i - 1, 0)])

    @pl.when(jnp.logical_and(changed, i < nu_ref[0]))
    def _():
        wg_bf[...] = wg_ref[0].astype(BF16)
        wu_bf[...] = wu_ref[0].astype(BF16)
        wd_bf[...] = wd_ref[0].astype(BF16)

    @pl.when(i < nu_ref[0])
    def _():
        x = x_ref[...].astype(BF16)
        g = _mm(x, wg_bf[...])
        u = _mm(x, wu_bf[...])
        y_ref[...] = _mm((_silu(g) * u).astype(BF16), wd_bf[...])

    @pl.when(i >= nu_ref[0])
    def _():
        y_ref[...] = jnp.zeros(y_ref.shape, F32)


def _experts(block_expert, n_used, buf, wg, wu, wd):
    ns, d = buf.shape
    de = wg.shape[2]
    grid_spec = pltpu.PrefetchScalarGridSpec(
        num_scalar_prefetch=2,
        grid=(ns // MOE_BLOCK,),
        in_specs=[pl.BlockSpec((MOE_BLOCK, d), lambda i, be, nu: (i, 0)),
                  pl.BlockSpec((1, d, de), lambda i, be, nu: (be[i], 0, 0)),
                  pl.BlockSpec((1, d, de), lambda i, be, nu: (be[i], 0, 0)),
                  pl.BlockSpec((1, de, d), lambda i, be, nu: (be[i], 0, 0))],
        out_specs=pl.BlockSpec((MOE_BLOCK, d), lambda i, be, nu: (i, 0)),
        scratch_shapes=[pltpu.VMEM((d, de), BF16), pltpu.VMEM((d, de), BF16),
                        pltpu.VMEM((de, d), BF16)],
    )
    return pl.pallas_call(
        _expert_kernel,
        grid_spec=grid_spec,
        out_shape=jax.ShapeDtypeStruct((ns, d), F32),
        compiler_params=pltpu.CompilerParams(dimension_semantics=("arbitrary",),
                                             vmem_limit_bytes=VMEM_LIMIT),
        name="moe_experts",
    )(block_expert, n_used, buf, wg, wu, wd)


def _combine_kernel(dest_ref, x_ref, route_ref, y_ref, o_ref, ybuf, sem):
    base = pl.program_id(0) * (TOP_K * MOE_TOKENS)

    def issue(j, carry):
        for kk in range(TOP_K):
            _row_copy(y_ref, dest_ref[base + TOP_K * j + kk], ybuf.at[kk], j, sem).start()
        return carry

    lax.fori_loop(0, MOE_TOKENS, issue, 0)

    def drain(j, carry):
        _row_copy(y_ref, 0, ybuf.at[0], 0, sem).wait()
        return carry

    lax.fori_loop(0, TOP_K * MOE_TOKENS, drain, 0)
    for t in range(MOE_TOKENS // SEQ_TILE):
        sl = slice(t * SEQ_TILE, (t + 1) * SEQ_TILE)
        route = route_ref[sl, :]
        o_ref[sl, :] = x_ref[sl, :] + route[:, 2:3] * ybuf[0, sl, :] + route[:, 3:4] * ybuf[1, sl, :]


def _combine(dest, x2, route, y):
    n, d = x2.shape
    grid_spec = pltpu.PrefetchScalarGridSpec(
        num_scalar_prefetch=1,
        grid=(n // MOE_TOKENS,),
        in_specs=[pl.BlockSpec((MOE_TOKENS, d), lambda i, dest: (i, 0)),
                  pl.BlockSpec((MOE_TOKENS, LANES), lambda i, dest: (i, 0)),
                  pl.BlockSpec(memory_space=pl.ANY)],
        out_specs=pl.BlockSpec((MOE_TOKENS, d), lambda i, dest: (i, 0)),
        scratch_shapes=[pltpu.VMEM((TOP_K, MOE_TOKENS, d), F32), pltpu.SemaphoreType.DMA(())],
    )
    return pl.pallas_call(
        _combine_kernel,
        grid_spec=grid_spec,
        out_shape=jax.ShapeDtypeStruct((n, d), F32),
        compiler_params=pltpu.CompilerParams(dimension_semantics=("arbitrary",),
                                             vmem_limit_bytes=VMEM_LIMIT),
        name="moe_combine",
    )(dest, x2, route, y)


def _moe_plan(route, counts, n_tok):
    n_tiles = counts.shape[0]
    cnt = counts[:, 0, :N_EXPERTS].astype(jnp.int32)
    totals = jnp.sum(cnt, axis=0)
    padded = (totals + MOE_BLOCK - 1) // MOE_BLOCK * MOE_BLOCK
    pad_end = jnp.cumsum(padded)
    pad_start = pad_end - padded
    tile_base = (pad_start[None, :] + jnp.cumsum(cnt, axis=0) - cnt).astype(F32)
    base_tok = jnp.broadcast_to(tile_base[:, None, :], (n_tiles, n_tok // n_tiles, N_EXPERTS))
    base_tok = base_tok.reshape(n_tok, N_EXPERTS)
    experts = jnp.arange(N_EXPERTS, dtype=F32)[None, :]
    dest = []
    for kk in range(TOP_K):
        hit = route[:, kk:kk + 1] == experts
        dest.append(jnp.sum(jnp.where(hit, base_tok, 0.0), axis=1) + route[:, 4 + kk])
    dest = jnp.stack(dest, axis=1).reshape(-1).astype(jnp.int32)
    n_blocks = n_tok * TOP_K // MOE_BLOCK + N_EXPERTS
    block_expert = jnp.minimum(
        jnp.searchsorted(pad_end, jnp.arange(n_blocks) * MOE_BLOCK, side='right'),
        N_EXPERTS - 1).astype(jnp.int32)
    n_used = (pad_end[-1:] // MOE_BLOCK).astype(jnp.int32)
    return dest, block_expert, n_used, n_blocks


def _pad_lanes(a, width=LANES):
    return jnp.pad(a, [(0, 0)] * (a.ndim - 1) + [(0, width - a.shape[-1])])


def kernel(x, norm_mix, w_in, dn_conv, dn_a_log, dn_dt_bias, dn_out_norm, at_q_norm, at_k_norm,
           rel_bias, cv_dw, cv_dw_bias, cv_ln_g, cv_ln_b, w_out, norm_ffn, router_group_w,
           router_group_b, router_expert_w, router_expert_b, ex_gate, ex_up, ex_down):
    bsz, seq, d = x.shape
    n_tok = bsz * seq
    depth = w_in.shape[0]
    c_ab = 4 * DN_WIDTH
    c_at = c_ab + 2 * DN_HEADS
    c_cv = c_at + 3 * AT_WIDTH
    bd_at = _block_diag_ones(AT_WIDTH, HEAD_DIM, BF16)
    bd_dn = _block_diag_ones(DN_WIDTH, HEAD_DIM, BF16)
    bias = _bias_tables(rel_bias)
    tri = (jnp.arange(ROW_TILE)[:, None] > jnp.arange(ROW_TILE)[None, :]).astype(BF16)
    per_head_lanes = lambda v: jnp.repeat(v, HEAD_DIM)[None, :]

    x2 = x.reshape(n_tok, d)
    for layer in range(depth):
        w_l = w_in[layer]
        dn, aq, ak, av, cu, ab = _proj(
            x2, norm_mix[layer][None, :],
            w_l[:, :c_ab].astype(BF16), w_l[:, c_at:c_cv].astype(BF16), w_l[:, c_cv:].astype(BF16),
            _pad_lanes(w_l[:, c_ab:c_at]).astype(BF16), bd_at,
            jnp.tile(at_q_norm[layer], AT_HEADS)[None, :] * (HEAD_DIM ** -0.5),
            jnp.tile(at_k_norm[layer], AT_HEADS)[None, :])

        y_dn = _deltanet(dn.reshape(bsz, seq, -1), ab.reshape(bsz, seq, LANES), dn_conv[layer],
                         per_head_lanes(dn_a_log[layer]), per_head_lanes(dn_dt_bias[layer]),
                         jnp.tile(dn_out_norm[layer], DN_HEADS)[None, :], bd_dn)
        y_at = _attention(aq.reshape(bsz, seq, -1), ak.reshape(bsz, seq, -1),
                          av.reshape(bsz, seq, -1), bias)
        y_cv = _conformer_conv(cu.reshape(bsz, seq, -1), cv_dw[layer], cv_dw_bias[layer][None, :],
                               cv_ln_g[layer][None, :], cv_ln_b[layer][None, :])

        wo = w_out[layer].astype(BF16)
        w_r = _pad_lanes(jnp.concatenate([router_expert_w[layer], router_group_w[layer]], axis=1))
        b_r = _pad_lanes(jnp.concatenate([router_expert_b[layer], router_group_b[layer]])[None, :])
        x_mid, h_ffn, route, counts = _out_proj(
            x2, y_dn.reshape(n_tok, DN_WIDTH), y_at.reshape(n_tok, AT_WIDTH),
            y_cv.reshape(n_tok, CV_WIDTH), wo[:DN_WIDTH], wo[DN_WIDTH:DN_WIDTH + AT_WIDTH],
            wo[DN_WIDTH + AT_WIDTH:], norm_ffn[layer][None, :], w_r.astype(BF16), b_r, tri)

        dest, block_expert, n_used, n_blocks = _moe_plan(route, counts, n_tok)
        buf = _dispatch(dest, h_ffn, jnp.zeros((n_blocks * MOE_BLOCK, d), F32))
        y = _experts(block_expert, n_used, buf, ex_gate[layer], ex_up[layer], ex_down[layer])
        x2 = _combine(dest, x_mid, route, y)
    return x2.reshape(bsz, seq, d)
```

```python
import math

import jax
import jax.numpy as jnp
import numpy as np
from jax import lax
from jax.experimental import pallas as pl
from jax.experimental.pallas import tpu as pltpu

F32 = jnp.float32
BF16 = jnp.bfloat16
HIGHEST = lax.Precision.HIGHEST

EPS = 1e-6
NEG_INF = -1e30

HEAD_DIM = 64
DN_HEADS = 4
DN_WIDTH = DN_HEADS * HEAD_DIM
DN_CONV = 4
DN_CHUNK = 64
AT_HEADS = 8
AT_WIDTH = AT_HEADS * HEAD_DIM
AT_BLOCK = 128
WINDOWS = ((128, 1), (512, 4), (2048, 16))
N_BUCKETS = 32
MAX_DISTANCE = 2048
CV_WIDTH = 256
CV_KERNEL = 31
N_GROUPS = 4
EXPERTS_PER_GROUP = 8
N_EXPERTS = N_GROUPS * EXPERTS_PER_GROUP
TOP_K = 2

LANES = 128
SUBLANES = 8
VMEM_LIMIT = 52 * 1024 * 1024

ROW_TILE = 512
SEQ_TILE = 256
DN_TILE = 64
DN_PREP_CHUNKS = 4
MOE_BLOCK = 256
MOE_TILE = ROW_TILE
MOE_PIECE = 16
MOE_CHUNK = 256
TILE_SLOTS = -(-(TOP_K * MOE_TILE + N_EXPERTS * (MOE_PIECE - 1)) // MOE_CHUNK) * MOE_CHUNK
PIECES_MAX = TILE_SLOTS // MOE_PIECE
ZERO_PIECES = MOE_BLOCK // MOE_PIECE - 1


def _mm(a, b, precision=None):
    return jnp.dot(a, b, preferred_element_type=F32, precision=precision)


def _mm_nt(a, b):
    return lax.dot_general(a, b, (((1,), (1,)), ((), ())), preferred_element_type=F32)


def _mm_tn(a, b):
    return lax.dot_general(a, b, (((0,), (0,)), ((), ())), preferred_element_type=F32)


def _sigmoid(x):
    return 1.0 / (1.0 + jnp.exp(-x))


def _silu(x):
    return x * _sigmoid(x)


def _split3(x):
    p0 = x.astype(BF16)
    r1 = x - p0.astype(F32)
    p1 = r1.astype(BF16)
    p2 = (r1 - p1.astype(F32)).astype(BF16)
    return p0, p1, p2


def _full_spec(a):
    nd = a.ndim
    return pl.BlockSpec(a.shape, lambda *_: (0,) * nd)


def _block_diag_ones(width, block, dtype):
    r = jnp.arange(width)[:, None] // block
    c = jnp.arange(width)[None, :] // block
    return (r == c).astype(dtype)


def _proj_kernel(x_ref, g_ref, wdn_ref, wat_ref, wcv_ref, wab_ref, bd_ref, qn_ref, kn_ref,
                 dn_ref, aq_ref, ak_ref, av_ref, cv_ref, ab_ref):
    x = x_ref[...]
    ms = jnp.mean(x * x, axis=-1, keepdims=True)
    h = (x * lax.rsqrt(ms + EPS) * g_ref[...]).astype(BF16)
    dn_ref[...] = _mm(h, wdn_ref[...]).astype(BF16)
    cv_ref[...] = _mm(h, wcv_ref[...]).astype(BF16)
    ab_ref[...] = _mm(h, wab_ref[...])
    at = _mm(h, wat_ref[...])
    q = at[:, 0:AT_WIDTH]
    k = at[:, AT_WIDTH:2 * AT_WIDTH]
    bd = bd_ref[...]
    qms = _mm((q * q).astype(BF16), bd) * (1.0 / HEAD_DIM)
    kms = _mm((k * k).astype(BF16), bd) * (1.0 / HEAD_DIM)
    aq_ref[...] = (q * lax.rsqrt(qms + EPS) * qn_ref[...]).astype(BF16)
    ak_ref[...] = (k * lax.rsqrt(kms + EPS) * kn_ref[...]).astype(BF16)
    av_ref[...] = at[:, 2 * AT_WIDTH:3 * AT_WIDTH].astype(BF16)


def _proj(x2, g, wdn, wat, wcv, wab, bd, qn, kn):
    n, d = x2.shape
    row = lambda w: pl.BlockSpec((ROW_TILE, w), lambda i: (i, 0))
    widths = (wdn.shape[1], AT_WIDTH, AT_WIDTH, AT_WIDTH, wcv.shape[1], LANES)
    dtypes = (BF16, BF16, BF16, BF16, BF16, F32)
    return pl.pallas_call(
        _proj_kernel,
        grid=(n // ROW_TILE,),
        in_specs=[row(d)] + [_full_spec(a) for a in (g, wdn, wat, wcv, wab, bd, qn, kn)],
        out_specs=[row(w) for w in widths],
        out_shape=[jax.ShapeDtypeStruct((n, w), t) for w, t in zip(widths, dtypes)],
        compiler_params=pltpu.CompilerParams(dimension_semantics=("parallel",),
                                             vmem_limit_bytes=VMEM_LIMIT),
        name="proj",
    )(x2, g, wdn, wat, wcv, wab, bd, qn, kn)


def _per_head(x, block_mask):
    return jnp.where(block_mask, jnp.concatenate([x] * DN_HEADS, axis=0), jnp.zeros((), x.dtype))


def _dn_kernel(dn_ref, ab_ref, cw_ref, alog_ref, dtb_ref, onorm_ref, bd_ref, y_ref,
               xpad, qs, ks, vs, gs, bs, ws, us, qks, qds, kds, os_, st):
    seq = dn_ref.shape[1]
    n_tiles = seq // DN_TILE
    cw3 = 3 * DN_WIDTH
    pad = SUBLANES
    c = DN_CHUNK
    bd = bd_ref[...]
    hid = lax.broadcasted_iota(jnp.int32, (1, DN_WIDTH), 1) // HEAD_DIM

    def expand(cols, first):
        out = cols[:, first + DN_HEADS - 1:first + DN_HEADS]
        for h in range(DN_HEADS - 2, -1, -1):
            out = jnp.where(hid == h, cols[:, first + h:first + h + 1], out)
        return out

    xpad[0:pad, :] = jnp.zeros((pad, cw3), F32)
    for t in range(n_tiles):
        r0 = t * DN_TILE
        xpad[pad + r0:pad + r0 + DN_TILE, :] = dn_ref[0, r0:r0 + DN_TILE, 0:cw3].astype(F32)
    for t in range(n_tiles):
        r0 = t * DN_TILE
        ab = ab_ref[0, r0:r0 + DN_TILE, :]
        sp_in = expand(ab, 0) + dtb_ref[...]
        softplus = jnp.maximum(sp_in, 0.0) + jnp.log(1.0 + jnp.exp(-jnp.abs(sp_in)))
        gs[r0:r0 + DN_TILE, :] = -jnp.exp(alog_ref[...]) * softplus
        bs[r0:r0 + DN_TILE, :] = _sigmoid(expand(ab, DN_HEADS))
        acc = jnp.zeros((DN_TILE, cw3), F32)
        for j in range(DN_CONV):
            off = pad + r0 - (DN_CONV - 1) + j
            acc = acc + xpad[off:off + DN_TILE, :] * cw_ref[j:j + 1, :]
        y = _silu(acc)
        q = y[:, 0:DN_WIDTH]
        k = y[:, DN_WIDTH:2 * DN_WIDTH]
        qss = _mm((q * q).astype(BF16), bd)
        kss = _mm((k * k).astype(BF16), bd)
        qs[r0:r0 + DN_TILE, :] = q * lax.rsqrt(qss + EPS) * (HEAD_DIM ** -0.5)
        ks[r0:r0 + DN_TILE, :] = k * lax.rsqrt(kss + EPS)
        vs[r0:r0 + DN_TILE, :] = y[:, 2 * DN_WIDTH:3 * DN_WIDTH]

    ri = lax.broadcasted_iota(jnp.int32, (c, DN_WIDTH), 0)
    ci = lax.broadcasted_iota(jnp.int32, (c, DN_WIDTH), 1) % HEAD_DIM
    causal = ri >= ci
    strict = ri > ci
    eye_cat = (ri == ci).astype(F32)
    r2 = lax.broadcasted_iota(jnp.int32, (c, c), 0)
    c2 = lax.broadcasted_iota(jnp.int32, (c, c), 1)
    lower_ones = (r2 >= c2).astype(BF16)
    all_ones = jnp.ones((c, c), BF16)
    block_mask = (lax.broadcasted_iota(jnp.int32, (DN_WIDTH, DN_WIDTH), 0) // HEAD_DIM
                  == lax.broadcasted_iota(jnp.int32, (DN_WIDTH, DN_WIDTH), 1) // HEAD_DIM)

    def mm_exact_rhs(lhs_bf, x):
        p0, p1, p2 = _split3(x)
        return _mm(lhs_bf, p0) + _mm(lhs_bf, p1) + _mm(lhs_bf, p2)

    def mm_bd(lhs, rhs_cat):
        return _mm(lhs.astype(BF16), _per_head(rhs_cat.astype(BF16), block_mask))

    def prep(m, carry):
        rows = [pl.multiple_of((DN_PREP_CHUNKS * m + cc) * c, c) for cc in range(DN_PREP_CHUNKS)]
        g_cum = [mm_exact_rhs(lower_ones, gs[pl.ds(r, c), :]) for r in rows]
        g_row = [mm_exact_rhs(all_ones, g * eye_cat) for g in g_cum]
        decay = [jnp.exp(jnp.where(causal, g - gr, NEG_INF)) for g, gr in zip(g_cum, g_row)]
        kc = [ks[pl.ds(r, c), :] for r in rows]
        qc = [qs[pl.ds(r, c), :] for r in rows]
        beta = [bs[pl.ds(r, c), :] for r in rows]
        kb = [k * b for k, b in zip(kc, beta)]
        aq = [_mm_nt(jnp.concatenate([b_, q_], axis=0).astype(BF16),
                     _per_head(k_.astype(BF16), block_mask))
              for b_, q_, k_ in zip(kb, qc, kc)]
        a_mat = [jnp.where(strict, x[:c] * d, 0.0) for x, d in zip(aq, decay)]
        qk = [x[c:] * d for x, d in zip(aq, decay)]
        p = [eye_cat - a for a in a_mat]
        pw = [mm_bd(a, a) for a in a_mat]
        for _ in range(4):
            both = [mm_bd(jnp.concatenate([p_, x], axis=0), x) for p_, x in zip(p, pw)]
            p = [p_ + b_[:c] for p_, b_ in zip(p, both)]
            pw = [b_[c:] for b_ in both]
        p = [p_ + mm_bd(p_, x) for p_, x in zip(p, pw)]
        for i, r in enumerate(rows):
            e_g = jnp.exp(g_cum[i])
            ws[pl.ds(r, c), :] = mm_bd(p[i], kb[i] * e_g).astype(BF16)
            us[pl.ds(r, c), :] = mm_bd(p[i], vs[pl.ds(r, c), :] * beta[i])
            qks[pl.ds(r, c), :] = qk[i].astype(BF16)
            qds[pl.ds(r, c), :] = (qc[i] * e_g).astype(BF16)
            kds[pl.ds(r, c), :] = (kc[i] * jnp.exp(g_cum[i][c - 1:c, :] - g_cum[i])).astype(BF16)
            gs[pl.ds(r, c), :] = g_cum[i]
        return carry

    lax.fori_loop(0, seq // (DN_PREP_CHUNKS * c), prep, 0)

    st[...] = jnp.zeros(st.shape, F32)

    def scan(n, carry):
        r = pl.multiple_of(n * c, c)
        state = st[...]
        both = _mm(jnp.concatenate([ws[pl.ds(r, c), :], qds[pl.ds(r, c), :]], axis=0),
                   _per_head(state.astype(BF16), block_mask))
        v_new = (us[pl.ds(r, c), :] - both[:c]).astype(BF16)
        os_[pl.ds(r, c), :] = both[c:] + _mm(qks[pl.ds(r, c), :], _per_head(v_new, block_mask))
        kv = _mm_tn(kds[pl.ds(r, c), :], v_new)
        upd = kv[(DN_HEADS - 1) * c:DN_HEADS * c, :]
        for h in range(DN_HEADS - 2, -1, -1):
            upd = jnp.where(hid == h, kv[h * c:(h + 1) * c, :], upd)
        g_last = gs[pl.ds(r + c - SUBLANES, SUBLANES), :][SUBLANES - 1:SUBLANES, :]
        st[...] = state * jnp.exp(g_last) + upd
        return carry

    lax.fori_loop(0, seq // c, scan, 0)

    for t in range(n_tiles):
        r0 = t * DN_TILE
        o = os_[r0:r0 + DN_TILE, :]
        z = dn_ref[0, r0:r0 + DN_TILE, cw3:cw3 + DN_WIDTH].astype(F32)
        ms = _mm((o * o).astype(BF16), bd) * (1.0 / HEAD_DIM)
        y_ref[0, r0:r0 + DN_TILE, :] = (o * lax.rsqrt(ms + EPS) * onorm_ref[...] * _silu(z)).astype(BF16)


def _deltanet(dn, ab, conv_w, alog_cat, dtb_cat, onorm_cat, bd):
    b, seq, w = dn.shape
    f32buf = pltpu.VMEM((seq, DN_WIDTH), F32)
    bf16buf = pltpu.VMEM((seq, DN_WIDTH), BF16)
    return pl.pallas_call(
        _dn_kernel,
        grid=(b,),
        in_specs=[pl.BlockSpec((1, seq, w), lambda i: (i, 0, 0)),
                  pl.BlockSpec((1, seq, LANES), lambda i: (i, 0, 0))]
                 + [_full_spec(a) for a in (conv_w, alog_cat, dtb_cat, onorm_cat, bd)],
        out_specs=pl.BlockSpec((1, seq, DN_WIDTH), lambda i: (i, 0, 0)),
        out_shape=jax.ShapeDtypeStruct((b, seq, DN_WIDTH), BF16),
        scratch_shapes=[pltpu.VMEM((SUBLANES + seq, 3 * DN_WIDTH), F32),
                        f32buf, f32buf, f32buf, f32buf, f32buf,
                        bf16buf, f32buf, bf16buf, bf16buf, bf16buf,
                        f32buf,
                        pltpu.VMEM((DN_CHUNK, DN_WIDTH), F32)],
        compiler_params=pltpu.CompilerParams(dimension_semantics=("parallel",),
                                             vmem_limit_bytes=VMEM_LIMIT),
        name="deltanet",
    )(dn, ab, conv_w, alog_cat, dtb_cat, onorm_cat, bd)


def _attn_kernel(q_ref, k_ref, v_ref, bias_ref, y_ref, qf, kf, vf, acc, ms, ls):
    seq = q_ref.shape[1]
    pairs = AT_HEADS // 2
    for t in range(seq // SEQ_TILE):
        r0 = t * SEQ_TILE
        for p in range(pairs):
            sl = slice(p * LANES, (p + 1) * LANES)
            qf[p, r0:r0 + SEQ_TILE, :] = q_ref[0, r0:r0 + SEQ_TILE, sl].astype(F32)
            kf[p, r0:r0 + SEQ_TILE, :] = k_ref[0, r0:r0 + SEQ_TILE, sl].astype(F32)
            vf[p, r0:r0 + SEQ_TILE, :] = v_ref[0, r0:r0 + SEQ_TILE, sl].astype(F32)

    lane = lax.broadcasted_iota(jnp.int32, (1, LANES), 1)
    key_col = lax.broadcasted_iota(jnp.int32, (1, 2 * AT_BLOCK), 1)
    order = sorted(range(len(WINDOWS)), key=lambda g: -WINDOWS[g][1])
    for step, grp in enumerate(order):
        dil = WINDOWS[grp][1]
        nb = seq // dil // AT_BLOCK
        is_first = step == 0
        is_last = step == len(order) - 1

        def rows(start, dil=dil):
            return pl.ds(start, AT_BLOCK, stride=dil) if dil > 1 else pl.ds(start, AT_BLOCK)

        def unit(u, carry, dil=dil, nb=nb, grp=grp, is_first=is_first, is_last=is_last, rows=rows):
            r = u // nb
            i = u % nb
            q_rows = rows(r + dil * AT_BLOCK * i)
            p_rows = rows(r + dil * AT_BLOCK * jnp.maximum(i - 1, 0))
            no_prev = jnp.where(jnp.logical_and(i == 0, key_col < AT_BLOCK), NEG_INF, 0.0)
            for p in range(pairs):
                q2 = qf[p, q_rows, :].astype(BF16)
                if nb > 1:
                    k2 = jnp.concatenate([kf[p, p_rows, :], kf[p, q_rows, :]], axis=0).astype(BF16)
                    v2 = jnp.concatenate([vf[p, p_rows, :], vf[p, q_rows, :]], axis=0).astype(BF16)
                else:
                    k2 = kf[p, q_rows, :].astype(BF16)
                    v2 = vf[p, q_rows, :].astype(BF16)
                v_ext = jnp.concatenate([v2, jnp.ones(v2.shape, BF16)], axis=1)
                m_new = l_new = o_new = None
                for hh in range(2):
                    mask = (lane // HEAD_DIM) == hh
                    qm = jnp.where(mask, q2, jnp.zeros_like(q2))
                    if nb > 1:
                        s = _mm_nt(qm, k2) + bias_ref[grp, 2 * p + hh] + no_prev
                    else:
                        s = _mm_nt(qm, k2) + bias_ref[grp, 2 * p + hh, :, AT_BLOCK:]
                    m_h = jnp.max(s, axis=-1, keepdims=True)
                    pexp = jnp.exp(s - m_h).astype(BF16)
                    res = _mm(pexp, v_ext)
                    m_b = jnp.broadcast_to(m_h, (AT_BLOCK, LANES))
                    if hh == 0:
                        m_new, o_new, l_new = m_b, res[:, :LANES], res[:, LANES:]
                    else:
                        m_new = jnp.where(mask, m_b, m_new)
                        o_new = jnp.where(mask, res[:, :LANES], o_new)
                        l_new = jnp.where(mask, res[:, LANES:], l_new)
                if not is_first:
                    m_old = ms[p, q_rows, :]
                    m_tot = jnp.maximum(m_old, m_new)
                    a_old = jnp.exp(m_old - m_tot)
                    a_new = jnp.exp(m_new - m_tot)
                    l_new = a_old * ls[p, q_rows, :] + a_new * l_new
                    o_new = a_old * acc[p, q_rows, :] + a_new * o_new
                    m_new = m_tot
                if is_last:
                    y_ref[0, pl.ds(pl.multiple_of(u * AT_BLOCK, AT_BLOCK), AT_BLOCK),
                          p * LANES:(p + 1) * LANES] = (o_new / l_new).astype(BF16)
                else:
                    ms[p, q_rows, :] = m_new
                    ls[p, q_rows, :] = l_new
                    acc[p, q_rows, :] = o_new
            return carry

        lax.fori_loop(0, seq // AT_BLOCK, unit, 0)


def _attention(aq, ak, av, bias):
    b, seq, w = aq.shape
    assert WINDOWS[0][1] == 1, "the last group processed must be the undilated one (token order)"
    slab = pltpu.VMEM((AT_HEADS // 2, seq, LANES), F32)
    tok = pl.BlockSpec((1, seq, w), lambda i: (i, 0, 0))
    return pl.pallas_call(
        _attn_kernel,
        grid=(b,),
        in_specs=[tok, tok, tok, _full_spec(bias)],
        out_specs=tok,
        out_shape=jax.ShapeDtypeStruct((b, seq, w), BF16),
        scratch_shapes=[slab] * 6,
        compiler_params=pltpu.CompilerParams(dimension_semantics=("parallel",),
                                             vmem_limit_bytes=VMEM_LIMIT),
        name="dilated_attention",
    )(aq, ak, av, bias)


def _t5_bucket(dist):
    max_exact = N_BUCKETS // 2
    d = np.maximum(dist, 1).astype(np.float32)
    log_bucket = max_exact + (np.log(d / np.float32(max_exact))
                              / np.float32(math.log(MAX_DISTANCE / max_exact))
                              * np.float32(N_BUCKETS - max_exact)).astype(np.int32)
    return np.where(dist < max_exact, dist, np.minimum(log_bucket, N_BUCKETS - 1))


def _bias_tables(rel_bias):
    rel = np.arange(AT_BLOCK)[:, None] + AT_BLOCK - np.arange(2 * AT_BLOCK)[None, :]
    tabs = []
    for window, dil in WINDOWS:
        n_back = window // dil
        valid = (rel >= 0) & (rel <= n_back)
        bucket = _t5_bucket(dil * np.clip(rel, 0, n_back)).reshape(-1)
        onehot = (np.arange(N_BUCKETS)[:, None] == bucket[None, :]).astype(np.float32)
        bias = jnp.dot(rel_bias.astype(F32).T, jnp.asarray(onehot, BF16).astype(F32), precision=HIGHEST)
        bias = bias.reshape(AT_HEADS, AT_BLOCK, 2 * AT_BLOCK)
        tabs.append(jnp.where(jnp.asarray(valid)[None], bias, NEG_INF))
    return jnp.stack(tabs)


def _cv_kernel(cu_ref, dw_ref, dwb_ref, g_ref, b_ref, y_ref, ypad):
    seq = cu_ref.shape[1]
    n_tiles = seq // SEQ_TILE
    pad = 4 * SUBLANES
    ypad[0:pad, :] = jnp.zeros((pad, CV_WIDTH), F32)
    for t in range(n_tiles):
        r0 = t * SEQ_TILE
        u = cu_ref[0, r0:r0 + SEQ_TILE, :].astype(F32)
        ypad[pad + r0:pad + r0 + SEQ_TILE, :] = u[:, :CV_WIDTH] * _sigmoid(u[:, CV_WIDTH:])
    for t in range(n_tiles):
        r0 = t * SEQ_TILE
        acc = jnp.zeros((SEQ_TILE, CV_WIDTH), F32) + dwb_ref[...]
        for j in range(CV_KERNEL):
            off = pad + r0 - (CV_KERNEL - 1) + j
            acc = acc + ypad[off:off + SEQ_TILE, :] * dw_ref[j:j + 1, :]
        mu = jnp.mean(acc, axis=-1, keepdims=True)
        cen = acc - mu
        var = jnp.mean(cen * cen, axis=-1, keepdims=True)
        yn = cen * lax.rsqrt(var + EPS) * g_ref[...] + b_ref[...]
        y_ref[0, r0:r0 + SEQ_TILE, :] = _silu(yn).astype(BF16)


def _conformer_conv(cu, dw, dwb, ln_g, ln_b):
    b, seq, w = cu.shape
    return pl.pallas_call(
        _cv_kernel,
        grid=(b,),
        in_specs=[pl.BlockSpec((1, seq, w), lambda i: (i, 0, 0))]
                 + [_full_spec(a) for a in (dw, dwb, ln_g, ln_b)],
        out_specs=pl.BlockSpec((1, seq, CV_WIDTH), lambda i: (i, 0, 0)),
        out_shape=jax.ShapeDtypeStruct((b, seq, CV_WIDTH), BF16),
        scratch_shapes=[pltpu.VMEM((4 * SUBLANES + seq, CV_WIDTH), F32)],
        compiler_params=pltpu.CompilerParams(dimension_semantics=("parallel",),
                                             vmem_limit_bytes=VMEM_LIMIT),
        name="conformer_conv",
    )(cu, dw, dwb, ln_g, ln_b)


def _out_kernel(x_ref, ydn_ref, yat_ref, ycv_ref, wdn_ref, wat_ref, wcv_ref, g_ref, wr_ref, rb_ref,
                tri_ref, xo_ref, h_ref, route_ref, cnt_ref):
    x = (x_ref[...] + _mm(ydn_ref[...], wdn_ref[...]) + _mm(yat_ref[...], wat_ref[...])
         + _mm(ycv_ref[...], wcv_ref[...]))
    xo_ref[...] = x
    ms = jnp.mean(x * x, axis=-1, keepdims=True)
    h = x * lax.rsqrt(ms + EPS) * g_ref[...]
    h_ref[...] = h.astype(BF16)

    logits = _mm(h.astype(BF16), wr_ref[...]) + rb_ref[...]
    lane = lax.broadcasted_iota(jnp.int32, logits.shape, 1)
    is_group = (lane >= N_EXPERTS) & (lane < N_EXPERTS + N_GROUPS)
    gl = jnp.where(is_group, logits, NEG_INF)
    gmax = jnp.max(gl, axis=-1, keepdims=True)
    gsel = jnp.min(jnp.where(gl == gmax, lane, 2 * LANES), axis=-1, keepdims=True) - N_EXPERTS
    p_group = 1.0 / jnp.sum(jnp.where(is_group, jnp.exp(gl - gmax), 0.0), axis=-1, keepdims=True)
    lo = gsel * EXPERTS_PER_GROUP
    in_group = (lane >= lo) & (lane < lo + EXPERTS_PER_GROUP)
    el = jnp.where(in_group, logits, NEG_INF)
    v1 = jnp.max(el, axis=-1, keepdims=True)
    i1 = jnp.min(jnp.where(el == v1, lane, LANES), axis=-1, keepdims=True)
    el2 = jnp.where(lane == i1, NEG_INF, el)
    v2 = jnp.max(el2, axis=-1, keepdims=True)
    i2 = jnp.min(jnp.where(el2 == v2, lane, LANES), axis=-1, keepdims=True)
    t = jnp.exp(v2 - v1)
    g1 = p_group / (1.0 + t)
    g2 = g1 * t

    oh1 = lane == i1
    oh2 = lane == i2
    both = jnp.where(oh1 | oh2, 1.0, 0.0).astype(BF16)
    before = _mm(tri_ref[...], both)
    rank1 = jnp.sum(jnp.where(oh1, before, 0.0), axis=-1, keepdims=True)
    rank2 = jnp.sum(jnp.where(oh2, before, 0.0), axis=-1, keepdims=True)
    cnt_ref[0] = _mm(jnp.ones((SUBLANES, both.shape[0]), BF16), both)

    route = jnp.where(lane == 0, i1.astype(F32), 0.0)
    route = jnp.where(lane == 1, i2.astype(F32), route)
    route = jnp.where(lane == 2, g1, route)
    route = jnp.where(lane == 3, g2, route)
    route = jnp.where(lane == 4, rank1, route)
    route = jnp.where(lane == 5, rank2, route)
    route_ref[...] = route


def _out_proj(x2, ydn, yat, ycv, wdn, wat, wcv, g, wr, rb, tri):
    n, d = x2.shape
    row = lambda w: pl.BlockSpec((ROW_TILE, w), lambda i: (i, 0))
    n_tiles = n // ROW_TILE
    return pl.pallas_call(
        _out_kernel,
        grid=(n_tiles,),
        in_specs=[row(d), row(DN_WIDTH), row(AT_WIDTH), row(CV_WIDTH)]
                 + [_full_spec(a) for a in (wdn, wat, wcv, g, wr, rb, tri)],
        out_specs=[row(d), row(d), row(LANES), pl.BlockSpec((1, SUBLANES, LANES), lambda i: (i, 0, 0))],
        out_shape=[jax.ShapeDtypeStruct((n, d), F32), jax.ShapeDtypeStruct((n, d), BF16),
                   jax.ShapeDtypeStruct((n, LANES), F32),
                   jax.ShapeDtypeStruct((n_tiles, SUBLANES, LANES), F32)],
        compiler_params=pltpu.CompilerParams(dimension_semantics=("parallel",),
                                             vmem_limit_bytes=VMEM_LIMIT),
        name="out_proj",
    )(x2, ydn, yat, ycv, wdn, wat, wcv, g, wr, rb, tri)


def _tile_slots(route, loc_row):
    lane = lax.broadcasted_iota(jnp.int32, route.shape, 1).astype(F32)
    slots = []
    for kk in range(TOP_K):
        base = jnp.sum(jnp.where(lane == route[:, kk:kk + 1], loc_row, 0.0), axis=-1, keepdims=True)
        slots.append(base + route[:, 4 + kk:5 + kk])
    return slots


def _chunk_cols(j):
    return (j * MOE_CHUNK + lax.broadcasted_iota(jnp.int32, (1, MOE_CHUNK), 1)).astype(F32)


def _piece(ref, row):
    return ref.at[pl.ds(pl.multiple_of(row, MOE_PIECE), MOE_PIECE), :]


def _dispatch_kernel(dst_ref, np_ref, zdst_ref, zvalid_ref, nu_ref, route_ref, loc_ref, h_ref, xs_ref,
                     srt, zeros, sem, zsem):
    tile = pl.program_id(0)
    n_pieces = np_ref[tile]
    n_blocks = xs_ref.shape[0] // MOE_BLOCK

    def zero_piece(z):
        return pltpu.make_async_copy(_piece(zeros, 0), _piece(xs_ref, zdst_ref[z]), zsem)

    def zero_block(b):
        return pltpu.make_async_copy(
            zeros, xs_ref.at[pl.ds(pl.multiple_of(b * MOE_BLOCK, MOE_BLOCK), MOE_BLOCK), :], zsem)

    @pl.when(tile == 0)
    def _():
        zeros[...] = jnp.zeros(zeros.shape, BF16)
        for wait in (False, True):
            def piece_body(z, carry, wait=wait):
                @pl.when(zvalid_ref[z] != 0)
                def _():
                    zero_piece(z).wait() if wait else zero_piece(z).start()
                return carry

            def block_body(b, carry, wait=wait):
                zero_block(b).wait() if wait else zero_block(b).start()
                return carry

            lax.fori_loop(0, N_EXPERTS * ZERO_PIECES, piece_body, 0)
            lax.fori_loop(nu_ref[0], n_blocks, block_body, 0)

    slot1, slot2 = _tile_slots(route_ref[...], loc_ref[0, 0:1, :])
    h = h_ref[...]

    def chunk(j, carry):
        col = _chunk_cols(j)
        onehot = jnp.where(col == slot1, 1.0, jnp.where(col == slot2, 1.0, 0.0)).astype(BF16)
        srt[pl.ds(pl.multiple_of(j * MOE_CHUNK, MOE_CHUNK), MOE_CHUNK), :] = _mm_tn(onehot, h).astype(BF16)
        return carry

    pieces_per_chunk = MOE_CHUNK // MOE_PIECE
    lax.fori_loop(0, (n_pieces + pieces_per_chunk - 1) // pieces_per_chunk, chunk, 0)

    def piece_copy(p):
        return pltpu.make_async_copy(_piece(srt, p * MOE_PIECE),
                                     _piece(xs_ref, dst_ref[tile * PIECES_MAX + p]), sem)

    def start(p, carry):
        piece_copy(p).start()
        return carry

    def wait(p, carry):
        piece_copy(p).wait()
        return carry

    lax.fori_loop(0, n_pieces, start, 0)
    lax.fori_loop(0, n_pieces, wait, 0)


def _dispatch(plan, route, h, n_slots):
    n, d = h.shape
    n_tiles = n // MOE_TILE
    grid_spec = pltpu.PrefetchScalarGridSpec(
        num_scalar_prefetch=5,
        grid=(n_tiles,),
        in_specs=[pl.BlockSpec((MOE_TILE, LANES), lambda i, *_: (i, 0)),
                  pl.BlockSpec((1, SUBLANES, LANES), lambda i, *_: (i, 0, 0)),
                  pl.BlockSpec((MOE_TILE, d), lambda i, *_: (i, 0))],
        out_specs=pl.BlockSpec(memory_space=pl.ANY),
        scratch_shapes=[pltpu.VMEM((TILE_SLOTS, d), BF16), pltpu.VMEM((MOE_BLOCK, d), BF16),
                        pltpu.SemaphoreType.DMA(()), pltpu.SemaphoreType.DMA(())],
    )
    return pl.pallas_call(
        _dispatch_kernel,
        grid_spec=grid_spec,
        out_shape=jax.ShapeDtypeStruct((n_slots, d), BF16),
        compiler_params=pltpu.CompilerParams(dimension_semantics=("arbitrary",),
                                             vmem_limit_bytes=VMEM_LIMIT),
        name="moe_dispatch",
    )(plan["dst"], plan["n_pieces"], plan["zdst"], plan["zvalid"], plan["n_used"], route,
      plan["loc"], h)


def _expert_kernel(be_ref, nu_ref, x_ref, wg_ref, wu_ref, wd_ref, y_ref, wg_bf, wu_bf, wd_bf):
    i = pl.program_id(0)
    changed = jnp.logical_or(i == 0, be_ref[i] != be_ref[jnp.maximum(i - 1, 0)])

    @pl.when(jnp.logical_and(changed, i < nu_ref[0]))
    def _():
        wg_bf[...] = wg_ref[0].astype(BF16)
        wu_bf[...] = wu_ref[0].astype(BF16)
        wd_bf[...] = wd_ref[0].astype(BF16)

    @pl.when(i < nu_ref[0])
    def _():
        x = x_ref[...]
        g = _mm(x, wg_bf[...])
        u = _mm(x, wu_bf[...])
        y_ref[...] = _mm((_silu(g) * u).astype(BF16), wd_bf[...]).astype(BF16)

    @pl.when(i >= nu_ref[0])
    def _():
        y_ref[...] = jnp.zeros(y_ref.shape, BF16)


def _experts(plan, xs, wg, wu, wd, layer):
    ns, d = xs.shape
    de = wg.shape[3]
    rows = lambda i, be, nu: (jnp.minimum(i, nu[0] - 1), 0)
    grid_spec = pltpu.PrefetchScalarGridSpec(
        num_scalar_prefetch=2,
        grid=(ns // MOE_BLOCK,),
        in_specs=[pl.BlockSpec((MOE_BLOCK, d), rows),
                  pl.BlockSpec((None, 1, d, de), lambda i, be, nu: (layer, be[i], 0, 0)),
                  pl.BlockSpec((None, 1, d, de), lambda i, be, nu: (layer, be[i], 0, 0)),
                  pl.BlockSpec((None, 1, de, d), lambda i, be, nu: (layer, be[i], 0, 0))],
        out_specs=pl.BlockSpec((MOE_BLOCK, d), lambda i, be, nu: (i, 0)),
        scratch_shapes=[pltpu.VMEM((d, de), BF16), pltpu.VMEM((d, de), BF16),
                        pltpu.VMEM((de, d), BF16)],
    )
    return pl.pallas_call(
        _expert_kernel,
        grid_spec=grid_spec,
        out_shape=jax.ShapeDtypeStruct((ns, d), BF16),
        compiler_params=pltpu.CompilerParams(dimension_semantics=("arbitrary",),
                                             vmem_limit_bytes=VMEM_LIMIT),
        name="moe_experts",
    )(plan["block_expert"], plan["n_used"], xs, wg, wu, wd)


def _combine_kernel(dst_ref, np_ref, route_ref, loc_ref, x_ref, y_ref, o_ref, ysrt, sem):
    tile = pl.program_id(0)
    n_pieces = np_ref[tile]

    @pl.when(tile == 0)
    def _():
        ysrt[...] = jnp.zeros(ysrt.shape, BF16)

    def piece_copy(p):
        return pltpu.make_async_copy(_piece(y_ref, dst_ref[tile * PIECES_MAX + p]),
                                     _piece(ysrt, p * MOE_PIECE), sem)

    def start(p, carry):
        piece_copy(p).start()
        return carry

    def wait(p, carry):
        piece_copy(p).wait()
        return carry

    lax.fori_loop(0, n_pieces, start, 0)
    route = route_ref[...]
    slot1, slot2 = _tile_slots(route, loc_ref[0, 0:1, :])
    g1 = route[:, 2:3]
    g2 = route[:, 3:4]
    o_ref[...] = x_ref[...]
    lax.fori_loop(0, n_pieces, wait, 0)

    def chunk(j, carry):
        col = _chunk_cols(j)
        gates = jnp.where(col == slot1, g1, jnp.where(col == slot2, g2, 0.0)).astype(BF16)
        o_ref[...] += _mm(gates, ysrt[pl.ds(pl.multiple_of(j * MOE_CHUNK, MOE_CHUNK), MOE_CHUNK), :])
        return carry

    pieces_per_chunk = MOE_CHUNK // MOE_PIECE
    lax.fori_loop(0, (n_pieces + pieces_per_chunk - 1) // pieces_per_chunk, chunk, 0)


def _combine(plan, route, x2, y):
    n, d = x2.shape
    grid_spec = pltpu.PrefetchScalarGridSpec(
        num_scalar_prefetch=2,
        grid=(n // MOE_TILE,),
        in_specs=[pl.BlockSpec((MOE_TILE, LANES), lambda i, *_: (i, 0)),
                  pl.BlockSpec((1, SUBLANES, LANES), lambda i, *_: (i, 0, 0)),
                  pl.BlockSpec((MOE_TILE, d), lambda i, *_: (i, 0)),
                  pl.BlockSpec(memory_space=pl.ANY)],
        out_specs=pl.BlockSpec((MOE_TILE, d), lambda i, *_: (i, 0)),
        scratch_shapes=[pltpu.VMEM((TILE_SLOTS, d), BF16), pltpu.SemaphoreType.DMA(())],
    )
    return pl.pallas_call(
        _combine_kernel,
        grid_spec=grid_spec,
        out_shape=jax.ShapeDtypeStruct((n, d), F32),
        compiler_params=pltpu.CompilerParams(dimension_semantics=("arbitrary",),
                                             vmem_limit_bytes=VMEM_LIMIT),
        name="moe_combine",
    )(plan["dst"], plan["n_pieces"], route, plan["loc"], x2, y)


def _round_up(v, m):
    return (v + m - 1) // m * m


def _moe_plan(counts, n_slots):
    n_tiles = counts.shape[0]
    cnt = counts[:, 0, :N_EXPERTS].astype(jnp.int32)
    seg = _round_up(cnt, MOE_PIECE)
    loc_end = jnp.cumsum(seg, axis=1)
    loc_start = loc_end - seg
    totals = jnp.sum(seg, axis=0)
    padded = _round_up(totals, MOE_BLOCK)
    pad_end = jnp.cumsum(padded)
    pad_start = pad_end - padded
    seg_start = pad_start[None, :] + jnp.cumsum(seg, axis=0) - seg
    piece_off = jnp.arange(PIECES_MAX, dtype=jnp.int32) * MOE_PIECE
    piece_e = jnp.sum(loc_end[:, None, :] <= piece_off[None, :, None], axis=2)
    hit = piece_e[..., None] == jnp.arange(N_EXPERTS)
    shift = jnp.sum(jnp.where(hit, (seg_start - loc_start)[:, None, :], 0), axis=2)
    valid = piece_off[None, :] < loc_end[:, -1:]
    dst = jnp.where(valid, shift + piece_off[None, :], 0).astype(jnp.int32).reshape(-1)
    zk = jnp.arange(ZERO_PIECES, dtype=jnp.int32)[None, :] * MOE_PIECE
    zvalid = zk < (padded - totals)[:, None]
    zdst = jnp.where(zvalid, (pad_start + totals)[:, None] + zk, 0)
    blk_start = jnp.arange(n_slots // MOE_BLOCK, dtype=jnp.int32) * MOE_BLOCK
    block_expert = jnp.minimum(jnp.sum(pad_end[None, :] <= blk_start[:, None], axis=1), N_EXPERTS - 1)
    loc = jnp.pad(loc_start.astype(F32), ((0, 0), (0, LANES - N_EXPERTS)))
    return {
        "dst": dst,
        "n_pieces": (loc_end[:, -1] // MOE_PIECE).astype(jnp.int32),
        "zdst": zdst.astype(jnp.int32).reshape(-1),
        "zvalid": zvalid.astype(jnp.int32).reshape(-1),
        "loc": jnp.broadcast_to(loc[:, None, :], (n_tiles, SUBLANES, LANES)),
        "block_expert": block_expert.astype(jnp.int32),
        "n_used": (pad_end[-1:] // MOE_BLOCK).astype(jnp.int32),
    }


def _pad_lanes(a, width=LANES):
    return jnp.pad(a, [(0, 0)] * (a.ndim - 1) + [(0, width - a.shape[-1])])


def kernel(x, norm_mix, w_in, dn_conv, dn_a_log, dn_dt_bias, dn_out_norm, at_q_norm, at_k_norm,
           rel_bias, cv_dw, cv_dw_bias, cv_ln_g, cv_ln_b, w_out, norm_ffn, router_group_w,
           router_group_b, router_expert_w, router_expert_b, ex_gate, ex_up, ex_down):
    bsz, seq, d = x.shape
    n_tok = bsz * seq
    depth = w_in.shape[0]
    c_ab = 4 * DN_WIDTH
    c_at = c_ab + 2 * DN_HEADS
    c_cv = c_at + 3 * AT_WIDTH
    bd_at = _block_diag_ones(AT_WIDTH, HEAD_DIM, BF16)
    bd_dn = _block_diag_ones(DN_WIDTH, HEAD_DIM, BF16)
    bias = _bias_tables(rel_bias)
    tri = (jnp.arange(ROW_TILE)[:, None] > jnp.arange(ROW_TILE)[None, :]).astype(BF16)
    per_head_lanes = lambda v: jnp.repeat(v, HEAD_DIM)[None, :]
    n_tiles = n_tok // MOE_TILE
    n_slots = _round_up(TOP_K * n_tok + n_tiles * N_EXPERTS * (MOE_PIECE - 1)
                        + N_EXPERTS * (MOE_BLOCK - 1), MOE_BLOCK)

    x2 = x.reshape(n_tok, d)
    for layer in range(depth):
        w_l = w_in[layer]
        dn, aq, ak, av, cu, ab = _proj(
            x2, norm_mix[layer][None, :],
            w_l[:, :c_ab].astype(BF16), w_l[:, c_at:c_cv].astype(BF16), w_l[:, c_cv:].astype(BF16),
            _pad_lanes(w_l[:, c_ab:c_at]).astype(BF16), bd_at,
            jnp.tile(at_q_norm[layer], AT_HEADS)[None, :] * (HEAD_DIM ** -0.5),
            jnp.tile(at_k_norm[layer], AT_HEADS)[None, :])

        y_dn = _deltanet(dn.reshape(bsz, seq, -1), ab.reshape(bsz, seq, LANES), dn_conv[layer],
                         per_head_lanes(dn_a_log[layer]), per_head_lanes(dn_dt_bias[layer]),
                         jnp.tile(dn_out_norm[layer], DN_HEADS)[None, :], bd_dn)
        y_at = _attention(aq.reshape(bsz, seq, -1), ak.reshape(bsz, seq, -1),
                          av.reshape(bsz, seq, -1), bias)
        y_cv = _conformer_conv(cu.reshape(bsz, seq, -1), cv_dw[layer], cv_dw_bias[layer][None, :],
                               cv_ln_g[layer][None, :], cv_ln_b[layer][None, :])

        wo = w_out[layer].astype(BF16)
        w_r = _pad_lanes(jnp.concatenate([router_expert_w[layer], router_group_w[layer]], axis=1))
        b_r = _pad_lanes(jnp.concatenate([router_expert_b[layer], router_group_b[layer]])[None, :])
        x_mid, h_ffn, route, counts = _out_proj(
            x2, y_dn.reshape(n_tok, DN_WIDTH), y_at.reshape(n_tok, AT_WIDTH),
            y_cv.reshape(n_tok, CV_WIDTH), wo[:DN_WIDTH], wo[DN_WIDTH:DN_WIDTH + AT_WIDTH],
            wo[DN_WIDTH + AT_WIDTH:], norm_ffn[layer][None, :], w_r.astype(BF16), b_r, tri)

        plan = _moe_plan(counts, n_slots)
        xs = _dispatch(plan, route, h_ffn, n_slots)
        y = _experts(plan, xs, ex_gate, ex_up, ex_down, layer)
        x2 = _combine(plan, route, x_mid, y)
    return x2.reshape(bsz, seq, d)
```

```python
import math

import jax
import jax.numpy as jnp
import numpy as np
from jax import lax
from jax.experimental import pallas as pl
from jax.experimental.pallas import tpu as pltpu

F32 = jnp.float32
BF16 = jnp.bfloat16
HIGHEST = lax.Precision.HIGHEST

EPS = 1e-6
NEG_INF = -1e30

HEAD_DIM = 64
DN_HEADS = 4
DN_WIDTH = DN_HEADS * HEAD_DIM
DN_CONV = 4
DN_CHUNK = 64
AT_HEADS = 8
AT_WIDTH = AT_HEADS * HEAD_DIM
AT_BLOCK = 128
WINDOWS = ((128, 1), (512, 4), (2048, 16))
MAX_DILATION = 16
AT_TILE = 256
N_BUCKETS = 32
MAX_DISTANCE = 2048
CV_WIDTH = 256
CV_KERNEL = 31
N_GROUPS = 4
EXPERTS_PER_GROUP = 8
N_EXPERTS = N_GROUPS * EXPERTS_PER_GROUP
TOP_K = 2

LANES = 128
SUBLANES = 8
VMEM_LIMIT = 52 * 1024 * 1024

ROW_TILE = 512
SEQ_TILE = 256
DN_TILE = 64
DN_PREP_CHUNKS = 4
MOE_BLOCK = 512
MOE_TILE = ROW_TILE
MOE_PIECE = 16
MOE_CHUNK = 256
TILE_SLOTS = -(-(TOP_K * MOE_TILE + N_EXPERTS * (MOE_PIECE - 1)) // MOE_CHUNK) * MOE_CHUNK
PIECES_MAX = TILE_SLOTS // MOE_PIECE
ZERO_PIECES = MOE_BLOCK // MOE_PIECE - 1


def _mm(a, b, precision=None):
    return jnp.dot(a, b, preferred_element_type=F32, precision=precision)


def _mm_nt(a, b):
    return lax.dot_general(a, b, (((1,), (1,)), ((), ())), preferred_element_type=F32)


def _mm_tn(a, b):
    return lax.dot_general(a, b, (((0,), (0,)), ((), ())), preferred_element_type=F32)


def _sigmoid(x):
    return 1.0 / (1.0 + jnp.exp(-x))


def _silu(x):
    return x * _sigmoid(x)


def _split3(x):
    p0 = x.astype(BF16)
    r1 = x - p0.astype(F32)
    p1 = r1.astype(BF16)
    p2 = (r1 - p1.astype(F32)).astype(BF16)
    return p0, p1, p2


def _full_spec(a):
    nd = a.ndim
    return pl.BlockSpec(a.shape, lambda *_: (0,) * nd)


def _block_diag_ones(width, block, dtype):
    r = jnp.arange(width)[:, None] // block
    c = jnp.arange(width)[None, :] // block
    return (r == c).astype(dtype)


def _proj_kernel(x_ref, g_ref, wdn_ref, wat_ref, wcv_ref, wab_ref, bd_ref, qn_ref, kn_ref,
                 dn_ref, aq_ref, ak_ref, av_ref, cv_ref, ab_ref):
    x = x_ref[...]
    ms = jnp.mean(x * x, axis=-1, keepdims=True)
    h = (x * lax.rsqrt(ms + EPS) * g_ref[...]).astype(BF16)
    dn_ref[...] = _mm(h, wdn_ref[...]).astype(BF16)
    cv_ref[...] = _mm(h, wcv_ref[...]).astype(BF16)
    ab_ref[...] = _mm(h, wab_ref[...])
    at = _mm(h, wat_ref[...])
    q = at[:, 0:AT_WIDTH]
    k = at[:, AT_WIDTH:2 * AT_WIDTH]
    bd = bd_ref[...]
    qms = _mm((q * q).astype(BF16), bd) * (1.0 / HEAD_DIM)
    kms = _mm((k * k).astype(BF16), bd) * (1.0 / HEAD_DIM)
    aq_ref[...] = (q * lax.rsqrt(qms + EPS) * qn_ref[...]).astype(BF16)
    ak_ref[...] = (k * lax.rsqrt(kms + EPS) * kn_ref[...]).astype(BF16)
    av_ref[...] = at[:, 2 * AT_WIDTH:3 * AT_WIDTH].astype(BF16)


def _proj(x2, g, wdn, wat, wcv, wab, bd, qn, kn):
    n, d = x2.shape
    row = lambda w: pl.BlockSpec((ROW_TILE, w), lambda i: (i, 0))
    widths = (wdn.shape[1], AT_WIDTH, AT_WIDTH, AT_WIDTH, wcv.shape[1], LANES)
    dtypes = (BF16, BF16, BF16, BF16, BF16, F32)
    return pl.pallas_call(
        _proj_kernel,
        grid=(n // ROW_TILE,),
        in_specs=[row(d)] + [_full_spec(a) for a in (g, wdn, wat, wcv, wab, bd, qn, kn)],
        out_specs=[row(w) for w in widths],
        out_shape=[jax.ShapeDtypeStruct((n, w), t) for w, t in zip(widths, dtypes)],
        compiler_params=pltpu.CompilerParams(dimension_semantics=("parallel",),
                                             vmem_limit_bytes=VMEM_LIMIT),
        name="proj",
    )(x2, g, wdn, wat, wcv, wab, bd, qn, kn)


def _per_head(x, block_mask):
    return jnp.where(block_mask, jnp.concatenate([x] * DN_HEADS, axis=0), jnp.zeros((), x.dtype))


def _dn_kernel(dn_ref, ab_ref, cw_ref, alog_ref, dtb_ref, onorm_ref, bd_ref, y_ref,
               xpad, qs, ks, vs, gs, bs, ws, us, qks, qds, kds, os_, st):
    seq = dn_ref.shape[1]
    n_tiles = seq // DN_TILE
    cw3 = 3 * DN_WIDTH
    pad = SUBLANES
    c = DN_CHUNK
    bd = bd_ref[...]
    hid = lax.broadcasted_iota(jnp.int32, (1, DN_WIDTH), 1) // HEAD_DIM

    def expand(cols, first):
        out = cols[:, first + DN_HEADS - 1:first + DN_HEADS]
        for h in range(DN_HEADS - 2, -1, -1):
            out = jnp.where(hid == h, cols[:, first + h:first + h + 1], out)
        return out

    xpad[0:pad, :] = jnp.zeros((pad, cw3), F32)
    for t in range(n_tiles):
        r0 = t * DN_TILE
        xpad[pad + r0:pad + r0 + DN_TILE, :] = dn_ref[0, r0:r0 + DN_TILE, 0:cw3].astype(F32)
    for t in range(n_tiles):
        r0 = t * DN_TILE
        ab = ab_ref[0, r0:r0 + DN_TILE, :]
        sp_in = expand(ab, 0) + dtb_ref[...]
        softplus = jnp.maximum(sp_in, 0.0) + jnp.log(1.0 + jnp.exp(-jnp.abs(sp_in)))
        gs[r0:r0 + DN_TILE, :] = -jnp.exp(alog_ref[...]) * softplus
        bs[r0:r0 + DN_TILE, :] = _sigmoid(expand(ab, DN_HEADS))
        acc = jnp.zeros((DN_TILE, cw3), F32)
        for j in range(DN_CONV):
            off = pad + r0 - (DN_CONV - 1) + j
            acc = acc + xpad[off:off + DN_TILE, :] * cw_ref[j:j + 1, :]
        y = _silu(acc)
        q = y[:, 0:DN_WIDTH]
        k = y[:, DN_WIDTH:2 * DN_WIDTH]
        qss = _mm((q * q).astype(BF16), bd)
        kss = _mm((k * k).astype(BF16), bd)
        qs[r0:r0 + DN_TILE, :] = q * lax.rsqrt(qss + EPS) * (HEAD_DIM ** -0.5)
        ks[r0:r0 + DN_TILE, :] = k * lax.rsqrt(kss + EPS)
        vs[r0:r0 + DN_TILE, :] = y[:, 2 * DN_WIDTH:3 * DN_WIDTH]

    ri = lax.broadcasted_iota(jnp.int32, (c, DN_WIDTH), 0)
    ci = lax.broadcasted_iota(jnp.int32, (c, DN_WIDTH), 1) % HEAD_DIM
    causal = ri >= ci
    strict = ri > ci
    eye_cat = (ri == ci).astype(F32)
    r2 = lax.broadcasted_iota(jnp.int32, (c, c), 0)
    c2 = lax.broadcasted_iota(jnp.int32, (c, c), 1)
    lower_ones = (r2 >= c2).astype(BF16)
    all_ones = jnp.ones((c, c), BF16)
    block_mask = (lax.broadcasted_iota(jnp.int32, (DN_WIDTH, DN_WIDTH), 0) // HEAD_DIM
                  == lax.broadcasted_iota(jnp.int32, (DN_WIDTH, DN_WIDTH), 1) // HEAD_DIM)

    def mm_exact_rhs(lhs_bf, x):
        p0, p1, p2 = _split3(x)
        return _mm(lhs_bf, p0) + _mm(lhs_bf, p1) + _mm(lhs_bf, p2)

    def mm_bd(lhs, rhs_cat):
        return _mm(lhs.astype(BF16), _per_head(rhs_cat.astype(BF16), block_mask))

    def prep(m, carry):
        rows = [pl.multiple_of((DN_PREP_CHUNKS * m + cc) * c, c) for cc in range(DN_PREP_CHUNKS)]
        g_cum = [mm_exact_rhs(lower_ones, gs[pl.ds(r, c), :]) for r in rows]
        g_row = [mm_exact_rhs(all_ones, g * eye_cat) for g in g_cum]
        decay = [jnp.exp(jnp.where(causal, g - gr, NEG_INF)) for g, gr in zip(g_cum, g_row)]
        kc = [ks[pl.ds(r, c), :] for r in rows]
        qc = [qs[pl.ds(r, c), :] for r in rows]
        beta = [bs[pl.ds(r, c), :] for r in rows]
        kb = [k * b for k, b in zip(kc, beta)]
        aq = [_mm_nt(jnp.concatenate([b_, q_], axis=0).astype(BF16),
                     _per_head(k_.astype(BF16), block_mask))
              for b_, q_, k_ in zip(kb, qc, kc)]
        a_mat = [jnp.where(strict, x[:c] * d, 0.0) for x, d in zip(aq, decay)]
        qk = [x[c:] * d for x, d in zip(aq, decay)]
        p = [eye_cat - a for a in a_mat]
        pw = [mm_bd(a, a) for a in a_mat]
        for _ in range(4):
            both = [mm_bd(jnp.concatenate([p_, x], axis=0), x) for p_, x in zip(p, pw)]
            p = [p_ + b_[:c] for p_, b_ in zip(p, both)]
            pw = [b_[c:] for b_ in both]
        p = [p_ + mm_bd(p_, x) for p_, x in zip(p, pw)]
        for i, r in enumerate(rows):
            e_g = jnp.exp(g_cum[i])
            ws[pl.ds(r, c), :] = mm_bd(p[i], kb[i] * e_g).astype(BF16)
            us[pl.ds(r, c), :] = mm_bd(p[i], vs[pl.ds(r, c), :] * beta[i])
            qks[pl.ds(r, c), :] = qk[i].astype(BF16)
            qds[pl.ds(r, c), :] = (qc[i] * e_g).astype(BF16)
            kds[pl.ds(r, c), :] = (kc[i] * jnp.exp(g_cum[i][c - 1:c, :] - g_cum[i])).astype(BF16)
            gs[pl.ds(r, c), :] = g_cum[i]
        return carry

    lax.fori_loop(0, seq // (DN_PREP_CHUNKS * c), prep, 0)

    st[...] = jnp.zeros(st.shape, F32)

    def scan(n, carry):
        r = pl.multiple_of(n * c, c)
        state = st[...]
        both = _mm(jnp.concatenate([ws[pl.ds(r, c), :], qds[pl.ds(r, c), :]], axis=0),
                   _per_head(state.astype(BF16), block_mask))
        v_new = (us[pl.ds(r, c), :] - both[:c]).astype(BF16)
        os_[pl.ds(r, c), :] = both[c:] + _mm(qks[pl.ds(r, c), :], _per_head(v_new, block_mask))
        kv = _mm_tn(kds[pl.ds(r, c), :], v_new)
        upd = kv[(DN_HEADS - 1) * c:DN_HEADS * c, :]
        for h in range(DN_HEADS - 2, -1, -1):
            upd = jnp.where(hid == h, kv[h * c:(h + 1) * c, :], upd)
        g_last = gs[pl.ds(r + c - SUBLANES, SUBLANES), :][SUBLANES - 1:SUBLANES, :]
        st[...] = state * jnp.exp(g_last) + upd
        return carry

    lax.fori_loop(0, seq // c, scan, 0)

    for t in range(n_tiles):
        r0 = t * DN_TILE
        o = os_[r0:r0 + DN_TILE, :]
        z = dn_ref[0, r0:r0 + DN_TILE, cw3:cw3 + DN_WIDTH].astype(F32)
        ms = _mm((o * o).astype(BF16), bd) * (1.0 / HEAD_DIM)
        y_ref[0, r0:r0 + DN_TILE, :] = (o * lax.rsqrt(ms + EPS) * onorm_ref[...] * _silu(z)).astype(BF16)


def _deltanet(dn, ab, conv_w, alog_cat, dtb_cat, onorm_cat, bd):
    b, seq, w = dn.shape
    f32buf = pltpu.VMEM((seq, DN_WIDTH), F32)
    bf16buf = pltpu.VMEM((seq, DN_WIDTH), BF16)
    return pl.pallas_call(
        _dn_kernel,
        grid=(b,),
        in_specs=[pl.BlockSpec((1, seq, w), lambda i: (i, 0, 0)),
                  pl.BlockSpec((1, seq, LANES), lambda i: (i, 0, 0))]
                 + [_full_spec(a) for a in (conv_w, alog_cat, dtb_cat, onorm_cat, bd)],
        out_specs=pl.BlockSpec((1, seq, DN_WIDTH), lambda i: (i, 0, 0)),
        out_shape=jax.ShapeDtypeStruct((b, seq, DN_WIDTH), BF16),
        scratch_shapes=[pltpu.VMEM((SUBLANES + seq, 3 * DN_WIDTH), F32),
                        f32buf, f32buf, f32buf, f32buf, f32buf,
                        bf16buf, f32buf, bf16buf, bf16buf, bf16buf,
                        f32buf,
                        pltpu.VMEM((DN_CHUNK, DN_WIDTH), F32)],
        compiler_params=pltpu.CompilerParams(dimension_semantics=("parallel",),
                                             vmem_limit_bytes=VMEM_LIMIT),
        name="deltanet",
    )(dn, ab, conv_w, alog_cat, dtb_cat, onorm_cat, bd)


def _run_starts(dil, r, i):
    if dil == 16:
        return [(tt * AT_TILE + r * 16, 16) for tt in range(AT_BLOCK * dil // AT_TILE)]
    if dil == 4:
        return [((2 * i + th) * AT_TILE + (4 * s + r) * 16, 16) for th in range(2) for s in range(4)]
    assert dil == 1
    return [((i // 2) * AT_TILE + rr * 16 + 8 * (i % 2), 8) for rr in range(16)]


def _run_order(dil):
    if dil == 16:
        return np.arange(AT_BLOCK)
    if dil == 4:
        th, s, ml = np.meshgrid(np.arange(2), np.arange(4), np.arange(16), indexing="ij")
        return (64 * th + 4 * ml + s).reshape(-1)
    rr, m8 = np.meshgrid(np.arange(16), np.arange(8), indexing="ij")
    return (16 * m8 + rr).reshape(-1)


def _load_runs(ref, p, runs):
    return jnp.concatenate([ref[p, pl.ds(pl.multiple_of(s, n), n), :] for s, n in runs], axis=0)


def _store_runs(ref, p, runs, val):
    off = 0
    for s, n in runs:
        ref[p, pl.ds(pl.multiple_of(s, n), n), :] = val[off:off + n]
        off += n


def _attn_kernel(q_ref, k_ref, v_ref, perm_ref, perm_t_ref, bias_ref, y_ref, qf, kf, vf, acc, ms, ls):
    seq = q_ref.shape[1]
    pairs = AT_HEADS // 2
    perm = perm_ref[...]
    for t in range(seq // AT_TILE):
        r0 = t * AT_TILE
        for src, dst in ((q_ref, qf), (k_ref, kf), (v_ref, vf)):
            rows = _mm(perm, src[0, r0:r0 + AT_TILE, :])
            for p in range(pairs):
                dst[p, r0:r0 + AT_TILE, :] = rows[:, p * LANES:(p + 1) * LANES]

    lane = lax.broadcasted_iota(jnp.int32, (1, LANES), 1)
    key_col = lax.broadcasted_iota(jnp.int32, (1, 2 * AT_BLOCK), 1)
    order = sorted(range(len(WINDOWS)), key=lambda g: -WINDOWS[g][1])
    for step, grp in enumerate(order):
        dil = WINDOWS[grp][1]
        nb = seq // dil // AT_BLOCK
        is_first = step == 0
        is_last = step == len(order) - 1

        def unit(u, carry, dil=dil, nb=nb, grp=grp, is_first=is_first, is_last=is_last):
            r = u // nb
            i = u % nb
            q_runs = _run_starts(dil, r, i)
            p_runs = _run_starts(dil, r, jnp.maximum(i - 1, 0))
            no_prev = jnp.where(jnp.logical_and(i == 0, key_col < AT_BLOCK), NEG_INF, 0.0)
            for p in range(pairs):
                q2 = _load_runs(qf, p, q_runs).astype(BF16)
                k2 = jnp.concatenate([_load_runs(kf, p, p_runs), _load_runs(kf, p, q_runs)],
                                     axis=0).astype(BF16)
                v2 = jnp.concatenate([_load_runs(vf, p, p_runs), _load_runs(vf, p, q_runs)],
                                     axis=0).astype(BF16)
                v_ext = jnp.concatenate([v2, jnp.ones(v2.shape, BF16)], axis=1)
                m_new = l_new = o_new = None
                for hh in range(2):
                    mask = (lane // HEAD_DIM) == hh
                    qm = jnp.where(mask, q2, jnp.zeros_like(q2))
                    s = _mm_nt(qm, k2) + bias_ref[grp, 2 * p + hh] + no_prev
                    m_h = jnp.max(s, axis=-1, keepdims=True)
                    pexp = jnp.exp(s - m_h).astype(BF16)
                    res = _mm(pexp, v_ext)
                    m_b = jnp.broadcast_to(m_h, (AT_BLOCK, LANES))
                    if hh == 0:
                        m_new, o_new, l_new = m_b, res[:, :LANES], res[:, LANES:]
                    else:
                        m_new = jnp.where(mask, m_b, m_new)
                        o_new = jnp.where(mask, res[:, :LANES], o_new)
                        l_new = jnp.where(mask, res[:, LANES:], l_new)
                if not is_first:
                    m_old = _load_runs(ms, p, q_runs)
                    m_tot = jnp.maximum(m_old, m_new)
                    a_old = jnp.exp(m_old - m_tot)
                    a_new = jnp.exp(m_new - m_tot)
                    l_new = a_old * _load_runs(ls, p, q_runs) + a_new * l_new
                    o_new = a_old * _load_runs(acc, p, q_runs) + a_new * o_new
                    m_new = m_tot
                if is_last:
                    _store_runs(acc, p, q_runs, o_new / l_new)
                else:
                    _store_runs(ms, p, q_runs, m_new)
                    _store_runs(ls, p, q_runs, l_new)
                    _store_runs(acc, p, q_runs, o_new)
            return carry

        lax.fori_loop(0, seq // AT_BLOCK, unit, 0)

    perm_t = perm_t_ref[...]
    for t in range(seq // AT_TILE):
        r0 = t * AT_TILE
        for p in range(pairs):
            y_ref[0, r0:r0 + AT_TILE, p * LANES:(p + 1) * LANES] = _mm(
                perm_t, acc[p, r0:r0 + AT_TILE, :].astype(BF16)).astype(BF16)


def _tile_permutation():
    t = np.arange(AT_TILE)
    row = (t % MAX_DILATION) * (AT_TILE // MAX_DILATION) + t // MAX_DILATION
    perm = np.zeros((AT_TILE, AT_TILE), np.float32)
    perm[row, t] = 1.0
    return perm


def _attention(aq, ak, av, bias):
    b, seq, w = aq.shape
    assert seq == AT_BLOCK * MAX_DILATION and [d for _, d in WINDOWS] == [1, 4, 16]
    slab = pltpu.VMEM((AT_HEADS // 2, seq, LANES), F32)
    tok = pl.BlockSpec((1, seq, w), lambda i: (i, 0, 0))
    perm = _tile_permutation()
    perm_in = jnp.asarray(perm, BF16)
    perm_out = jnp.asarray(perm.T, BF16)
    return pl.pallas_call(
        _attn_kernel,
        grid=(b,),
        in_specs=[tok, tok, tok, _full_spec(perm_in), _full_spec(perm_out), _full_spec(bias)],
        out_specs=tok,
        out_shape=jax.ShapeDtypeStruct((b, seq, w), BF16),
        scratch_shapes=[slab] * 6,
        compiler_params=pltpu.CompilerParams(dimension_semantics=("parallel",),
                                             vmem_limit_bytes=VMEM_LIMIT),
        name="dilated_attention",
    )(aq, ak, av, perm_in, perm_out, bias)


def _t5_bucket(dist):
    max_exact = N_BUCKETS // 2
    d = np.maximum(dist, 1).astype(np.float32)
    log_bucket = max_exact + (np.log(d / np.float32(max_exact))
                              / np.float32(math.log(MAX_DISTANCE / max_exact))
                              * np.float32(N_BUCKETS - max_exact)).astype(np.int32)
    return np.where(dist < max_exact, dist, np.minimum(log_bucket, N_BUCKETS - 1))


def _bias_tables(rel_bias):
    tabs = []
    for window, dil in WINDOWS:
        n_back = window // dil
        j = _run_order(dil)
        rel = j[:, None] + AT_BLOCK - np.concatenate([j, AT_BLOCK + j])[None, :]
        valid = (rel >= 0) & (rel <= n_back)
        bucket = _t5_bucket(dil * np.clip(rel, 0, n_back)).reshape(-1)
        onehot = (np.arange(N_BUCKETS)[:, None] == bucket[None, :]).astype(np.float32)
        bias = jnp.dot(rel_bias.astype(F32).T, jnp.asarray(onehot, BF16).astype(F32), precision=HIGHEST)
        bias = bias.reshape(AT_HEADS, AT_BLOCK, 2 * AT_BLOCK)
        tabs.append(jnp.where(jnp.asarray(valid)[None], bias, NEG_INF))
    return jnp.stack(tabs)


def _cv_kernel(cu_ref, dw_ref, dwb_ref, g_ref, b_ref, y_ref, ypad):
    seq = cu_ref.shape[1]
    n_tiles = seq // SEQ_TILE
    pad = 4 * SUBLANES
    ypad[0:pad, :] = jnp.zeros((pad, CV_WIDTH), F32)
    for t in range(n_tiles):
        r0 = t * SEQ_TILE
        u = cu_ref[0, r0:r0 + SEQ_TILE, :].astype(F32)
        ypad[pad + r0:pad + r0 + SEQ_TILE, :] = u[:, :CV_WIDTH] * _sigmoid(u[:, CV_WIDTH:])
    for t in range(n_tiles):
        r0 = t * SEQ_TILE
        acc = jnp.zeros((SEQ_TILE, CV_WIDTH), F32) + dwb_ref[...]
        for j in range(CV_KERNEL):
            off = pad + r0 - (CV_KERNEL - 1) + j
            acc = acc + ypad[off:off + SEQ_TILE, :] * dw_ref[j:j + 1, :]
        mu = jnp.mean(acc, axis=-1, keepdims=True)
        cen = acc - mu
        var = jnp.mean(cen * cen, axis=-1, keepdims=True)
        yn = cen * lax.rsqrt(var + EPS) * g_ref[...] + b_ref[...]
        y_ref[0, r0:r0 + SEQ_TILE, :] = _silu(yn).astype(BF16)


def _conformer_conv(cu, dw, dwb, ln_g, ln_b):
    b, seq, w = cu.shape
    return pl.pallas_call(
        _cv_kernel,
        grid=(b,),
        in_specs=[pl.BlockSpec((1, seq, w), lambda i: (i, 0, 0))]
                 + [_full_spec(a) for a in (dw, dwb, ln_g, ln_b)],
        out_specs=pl.BlockSpec((1, seq, CV_WIDTH), lambda i: (i, 0, 0)),
        out_shape=jax.ShapeDtypeStruct((b, seq, CV_WIDTH), BF16),
        scratch_shapes=[pltpu.VMEM((4 * SUBLANES + seq, CV_WIDTH), F32)],
        compiler_params=pltpu.CompilerParams(dimension_semantics=("parallel",),
                                             vmem_limit_bytes=VMEM_LIMIT),
        name="conformer_conv",
    )(cu, dw, dwb, ln_g, ln_b)


def _out_kernel(x_ref, ydn_ref, yat_ref, ycv_ref, wdn_ref, wat_ref, wcv_ref, g_ref, wr_ref, rb_ref,
                tri_ref, xo_ref, h_ref, route_ref, cnt_ref):
    x = (x_ref[...] + _mm(ydn_ref[...], wdn_ref[...]) + _mm(yat_ref[...], wat_ref[...])
         + _mm(ycv_ref[...], wcv_ref[...]))
    xo_ref[...] = x
    ms = jnp.mean(x * x, axis=-1, keepdims=True)
    h = x * lax.rsqrt(ms + EPS) * g_ref[...]
    h_ref[...] = h.astype(BF16)

    logits = _mm(h.astype(BF16), wr_ref[...]) + rb_ref[...]
    lane = lax.broadcasted_iota(jnp.int32, logits.shape, 1)
    is_group = (lane >= N_EXPERTS) & (lane < N_EXPERTS + N_GROUPS)
    gl = jnp.where(is_group, logits, NEG_INF)
    gmax = jnp.max(gl, axis=-1, keepdims=True)
    gsel = jnp.min(jnp.where(gl == gmax, lane, 2 * LANES), axis=-1, keepdims=True) - N_EXPERTS
    p_group = 1.0 / jnp.sum(jnp.where(is_group, jnp.exp(gl - gmax), 0.0), axis=-1, keepdims=True)
    lo = gsel * EXPERTS_PER_GROUP
    in_group = (lane >= lo) & (lane < lo + EXPERTS_PER_GROUP)
    el = jnp.where(in_group, logits, NEG_INF)
    v1 = jnp.max(el, axis=-1, keepdims=True)
    i1 = jnp.min(jnp.where(el == v1, lane, LANES), axis=-1, keepdims=True)
    el2 = jnp.where(lane == i1, NEG_INF, el)
    v2 = jnp.max(el2, axis=-1, keepdims=True)
    i2 = jnp.min(jnp.where(el2 == v2, lane, LANES), axis=-1, keepdims=True)
    t = jnp.exp(v2 - v1)
    g1 = p_group / (1.0 + t)
    g2 = g1 * t

    oh1 = lane == i1
    oh2 = lane == i2
    both = jnp.where(oh1 | oh2, 1.0, 0.0).astype(BF16)
    before = _mm(tri_ref[...], both)
    rank1 = jnp.sum(jnp.where(oh1, before, 0.0), axis=-1, keepdims=True)
    rank2 = jnp.sum(jnp.where(oh2, before, 0.0), axis=-1, keepdims=True)
    cnt_ref[0] = _mm(jnp.ones((SUBLANES, both.shape[0]), BF16), both)

    route = jnp.where(lane == 0, i1.astype(F32), 0.0)
    route = jnp.where(lane == 1, i2.astype(F32), route)
    route = jnp.where(lane == 2, g1, route)
    route = jnp.where(lane == 3, g2, route)
    route = jnp.where(lane == 4, rank1, route)
    route = jnp.where(lane == 5, rank2, route)
    route_ref[...] = route


def _out_proj(x2, ydn, yat, ycv, wdn, wat, wcv, g, wr, rb, tri):
    n, d = x2.shape
    row = lambda w: pl.BlockSpec((ROW_TILE, w), lambda i: (i, 0))
    n_tiles = n // ROW_TILE
    return pl.pallas_call(
        _out_kernel,
        grid=(n_tiles,),
        in_specs=[row(d), row(DN_WIDTH), row(AT_WIDTH), row(CV_WIDTH)]
                 + [_full_spec(a) for a in (wdn, wat, wcv, g, wr, rb, tri)],
        out_specs=[row(d), row(d), row(LANES), pl.BlockSpec((1, SUBLANES, LANES), lambda i: (i, 0, 0))],
        out_shape=[jax.ShapeDtypeStruct((n, d), F32), jax.ShapeDtypeStruct((n, d), BF16),
                   jax.ShapeDtypeStruct((n, LANES), F32),
                   jax.ShapeDtypeStruct((n_tiles, SUBLANES, LANES), F32)],
        compiler_params=pltpu.CompilerParams(dimension_semantics=("parallel",),
                                             vmem_limit_bytes=VMEM_LIMIT),
        name="out_proj",
    )(x2, ydn, yat, ycv, wdn, wat, wcv, g, wr, rb, tri)


def _tile_slots(route, loc_row):
    lane = lax.broadcasted_iota(jnp.int32, route.shape, 1).astype(F32)
    slots = []
    for kk in range(TOP_K):
        base = jnp.sum(jnp.where(lane == route[:, kk:kk + 1], loc_row, 0.0), axis=-1, keepdims=True)
        slots.append(base + route[:, 4 + kk:5 + kk])
    return slots


def _chunk_cols(j):
    return (j * MOE_CHUNK + lax.broadcasted_iota(jnp.int32, (1, MOE_CHUNK), 1)).astype(F32)


def _piece(ref, row):
    return ref.at[pl.ds(pl.multiple_of(row, MOE_PIECE), MOE_PIECE), :]


def _dispatch_kernel(dst_ref, np_ref, zdst_ref, zvalid_ref, nu_ref, route_ref, loc_ref, h_ref, xs_ref,
                     srt, zeros, sem, zsem):
    tile = pl.program_id(0)
    n_pieces = np_ref[tile]
    n_blocks = xs_ref.shape[0] // MOE_BLOCK

    def zero_piece(z):
        return pltpu.make_async_copy(_piece(zeros, 0), _piece(xs_ref, zdst_ref[z]), zsem)

    def zero_block(b):
        return pltpu.make_async_copy(
            zeros, xs_ref.at[pl.ds(pl.multiple_of(b * MOE_BLOCK, MOE_BLOCK), MOE_BLOCK), :], zsem)

    @pl.when(tile == 0)
    def _():
        zeros[...] = jnp.zeros(zeros.shape, BF16)
        for wait in (False, True):
            def piece_body(z, carry, wait=wait):
                @pl.when(zvalid_ref[z] != 0)
                def _():
                    zero_piece(z).wait() if wait else zero_piece(z).start()
                return carry

            def block_body(b, carry, wait=wait):
                zero_block(b).wait() if wait else zero_block(b).start()
                return carry

            lax.fori_loop(0, N_EXPERTS * ZERO_PIECES, piece_body, 0)
            lax.fori_loop(nu_ref[0], n_blocks, block_body, 0)

    slot1, slot2 = _tile_slots(route_ref[...], loc_ref[0, 0:1, :])
    h = h_ref[...]

    def chunk(j, carry):
        col = _chunk_cols(j)
        onehot = jnp.where(col == slot1, 1.0, jnp.where(col == slot2, 1.0, 0.0)).astype(BF16)
        srt[pl.ds(pl.multiple_of(j * MOE_CHUNK, MOE_CHUNK), MOE_CHUNK), :] = _mm_tn(onehot, h).astype(BF16)
        return carry

    pieces_per_chunk = MOE_CHUNK // MOE_PIECE
    lax.fori_loop(0, (n_pieces + pieces_per_chunk - 1) // pieces_per_chunk, chunk, 0)

    def piece_copy(p):
        return pltpu.make_async_copy(_piece(srt, p * MOE_PIECE),
                                     _piece(xs_ref, dst_ref[tile * PIECES_MAX + p]), sem)

    def start(p, carry):
        piece_copy(p).start()
        return carry

    def wait(p, carry):
        piece_copy(p).wait()
        return carry

    lax.fori_loop(0, n_pieces, start, 0)
    lax.fori_loop(0, n_pieces, wait, 0)


def _dispatch(plan, route, h, n_slots):
    n, d = h.shape
    n_tiles = n // MOE_TILE
    grid_spec = pltpu.PrefetchScalarGridSpec(
        num_scalar_prefetch=5,
        grid=(n_tiles,),
        in_specs=[pl.BlockSpec((MOE_TILE, LANES), lambda i, *_: (i, 0)),
                  pl.BlockSpec((1, SUBLANES, LANES), lambda i, *_: (i, 0, 0)),
                  pl.BlockSpec((MOE_TILE, d), lambda i, *_: (i, 0))],
        out_specs=pl.BlockSpec(memory_space=pl.ANY),
        scratch_shapes=[pltpu.VMEM((TILE_SLOTS, d), BF16), pltpu.VMEM((MOE_BLOCK, d), BF16),
                        pltpu.SemaphoreType.DMA(()), pltpu.SemaphoreType.DMA(())],
    )
    return pl.pallas_call(
        _dispatch_kernel,
        grid_spec=grid_spec,
        out_shape=jax.ShapeDtypeStruct((n_slots, d), BF16),
        compiler_params=pltpu.CompilerParams(dimension_semantics=("arbitrary",),
                                             vmem_limit_bytes=VMEM_LIMIT),
        name="moe_dispatch",
    )(plan["dst"], plan["n_pieces"], plan["zdst"], plan["zvalid"], plan["n_used"], route,
      plan["loc"], h)


def _expert_kernel(be_ref, nu_ref, x_ref, wg_ref, wu_ref, wd_ref, y_ref, wg_bf, wu_bf, wd_bf):
    i = pl.program_id(0)
    changed = jnp.logical_or(i == 0, be_ref[i] != be_ref[jnp.maximum(i - 1, 0)])

    @pl.when(jnp.logical_and(changed, i < nu_ref[0]))
    def _():
        wg_bf[...] = wg_ref[0].astype(BF16)
        wu_bf[...] = wu_ref[0].astype(BF16)
        wd_bf[...] = wd_ref[0].astype(BF16)

    @pl.when(i < nu_ref[0])
    def _():
        x = x_ref[...]
        g = _mm(x, wg_bf[...])
        u = _mm(x, wu_bf[...])
        y_ref[...] = _mm((_silu(g) * u).astype(BF16), wd_bf[...]).astype(BF16)

    @pl.when(i >= nu_ref[0])
    def _():
        y_ref[...] = jnp.zeros(y_ref.shape, BF16)


def _experts(plan, xs, wg, wu, wd, layer):
    ns, d = xs.shape
    de = wg.shape[3]
    rows = lambda i, be, nu: (jnp.minimum(i, nu[0] - 1), 0)
    grid_spec = pltpu.PrefetchScalarGridSpec(
        num_scalar_prefetch=2,
        grid=(ns // MOE_BLOCK,),
        in_specs=[pl.BlockSpec((MOE_BLOCK, d), rows),
                  pl.BlockSpec((None, 1, d, de), lambda i, be, nu: (layer, be[i], 0, 0)),
                  pl.BlockSpec((None, 1, d, de), lambda i, be, nu: (layer, be[i], 0, 0)),
                  pl.BlockSpec((None, 1, de, d), lambda i, be, nu: (layer, be[i], 0, 0))],
        out_specs=pl.BlockSpec((MOE_BLOCK, d), lambda i, be, nu: (i, 0)),
        scratch_shapes=[pltpu.VMEM((d, de), BF16), pltpu.VMEM((d, de), BF16),
                        pltpu.VMEM((de, d), BF16)],
    )
    return pl.pallas_call(
        _expert_kernel,
        grid_spec=grid_spec,
        out_shape=jax.ShapeDtypeStruct((ns, d), BF16),
        compiler_params=pltpu.CompilerParams(dimension_semantics=("arbitrary",),
                                             vmem_limit_bytes=VMEM_LIMIT),
        name="moe_experts",
    )(plan["block_expert"], plan["n_used"], xs, wg, wu, wd)


def _combine_kernel(dst_ref, np_ref, route_ref, loc_ref, x_ref, y_ref, o_ref, ysrt, sem):
    tile = pl.program_id(0)
    n_pieces = np_ref[tile]

    @pl.when(tile == 0)
    def _():
        ysrt[...] = jnp.zeros(ysrt.shape, BF16)

    def piece_copy(p):
        return pltpu.make_async_copy(_piece(y_ref, dst_ref[tile * PIECES_MAX + p]),
                                     _piece(ysrt, p * MOE_PIECE), sem)

    def start(p, carry):
        piece_copy(p).start()
        return carry

    def wait(p, carry):
        piece_copy(p).wait()
        return carry

    lax.fori_loop(0, n_pieces, start, 0)
    route = route_ref[...]
    slot1, slot2 = _tile_slots(route, loc_ref[0, 0:1, :])
    g1 = route[:, 2:3]
    g2 = route[:, 3:4]
    o_ref[...] = x_ref[...]
    lax.fori_loop(0, n_pieces, wait, 0)

    def chunk(j, carry):
        col = _chunk_cols(j)
        gates = jnp.where(col == slot1, g1, jnp.where(col == slot2, g2, 0.0)).astype(BF16)
        o_ref[...] += _mm(gates, ysrt[pl.ds(pl.multiple_of(j * MOE_CHUNK, MOE_CHUNK), MOE_CHUNK), :])
        return carry

    pieces_per_chunk = MOE_CHUNK // MOE_PIECE
    lax.fori_loop(0, (n_pieces + pieces_per_chunk - 1) // pieces_per_chunk, chunk, 0)


def _combine(plan, route, x2, y):
    n, d = x2.shape
    grid_spec = pltpu.PrefetchScalarGridSpec(
        num_scalar_prefetch=2,
        grid=(n // MOE_TILE,),
        in_specs=[pl.BlockSpec((MOE_TILE, LANES), lambda i, *_: (i, 0)),
                  pl.BlockSpec((1, SUBLANES, LANES), lambda i, *_: (i, 0, 0)),
                  pl.BlockSpec((MOE_TILE, d), lambda i, *_: (i, 0)),
                  pl.BlockSpec(memory_space=pl.ANY)],
        out_specs=pl.BlockSpec((MOE_TILE, d), lambda i, *_: (i, 0)),
        scratch_shapes=[pltpu.VMEM((TILE_SLOTS, d), BF16), pltpu.SemaphoreType.DMA(())],
    )
    return pl.pallas_call(
        _combine_kernel,
        grid_spec=grid_spec,
        out_shape=jax.ShapeDtypeStruct((n, d), F32),
        compiler_params=pltpu.CompilerParams(dimension_semantics=("arbitrary",),
                                             vmem_limit_bytes=VMEM_LIMIT),
        name="moe_combine",
    )(plan["dst"], plan["n_pieces"], route, plan["loc"], x2, y)


def _round_up(v, m):
    return (v + m - 1) // m * m


def _moe_plan(counts, n_slots):
    n_tiles = counts.shape[0]
    cnt = counts[:, 0, :N_EXPERTS].astype(jnp.int32)
    seg = _round_up(cnt, MOE_PIECE)
    loc_end = jnp.cumsum(seg, axis=1)
    loc_start = loc_end - seg
    totals = jnp.sum(seg, axis=0)
    padded = _round_up(totals, MOE_BLOCK)
    pad_end = jnp.cumsum(padded)
    pad_start = pad_end - padded
    seg_start = pad_start[None, :] + jnp.cumsum(seg, axis=0) - seg
    piece_off = jnp.arange(PIECES_MAX, dtype=jnp.int32) * MOE_PIECE
    piece_e = jnp.sum(loc_end[:, None, :] <= piece_off[None, :, None], axis=2)
    hit = piece_e[..., None] == jnp.arange(N_EXPERTS)
    shift = jnp.sum(jnp.where(hit, (seg_start - loc_start)[:, None, :], 0), axis=2)
    valid = piece_off[None, :] < loc_end[:, -1:]
    dst = jnp.where(valid, shift + piece_off[None, :], 0).astype(jnp.int32).reshape(-1)
    zk = jnp.arange(ZERO_PIECES, dtype=jnp.int32)[None, :] * MOE_PIECE
    zvalid = zk < (padded - totals)[:, None]
    zdst = jnp.where(zvalid, (pad_start + totals)[:, None] + zk, 0)
    blk_start = jnp.arange(n_slots // MOE_BLOCK, dtype=jnp.int32) * MOE_BLOCK
    block_expert = jnp.minimum(jnp.sum(pad_end[None, :] <= blk_start[:, None], axis=1), N_EXPERTS - 1)
    loc = jnp.pad(loc_start.astype(F32), ((0, 0), (0, LANES - N_EXPERTS)))
    return {
        "dst": dst,
        "n_pieces": (loc_end[:, -1] // MOE_PIECE).astype(jnp.int32),
        "zdst": zdst.astype(jnp.int32).reshape(-1),
        "zvalid": zvalid.astype(jnp.int32).reshape(-1),
        "loc": jnp.broadcast_to(loc[:, None, :], (n_tiles, SUBLANES, LANES)),
        "block_expert": block_expert.astype(jnp.int32),
        "n_used": (pad_end[-1:] // MOE_BLOCK).astype(jnp.int32),
    }


def _pad_lanes(a, width=LANES):
    return jnp.pad(a, [(0, 0)] * (a.ndim - 1) + [(0, width - a.shape[-1])])


def kernel(x, norm_mix, w_in, dn_conv, dn_a_log, dn_dt_bias, dn_out_norm, at_q_norm, at_k_norm,
           rel_bias, cv_dw, cv_dw_bias, cv_ln_g, cv_ln_b, w_out, norm_ffn, router_group_w,
           router_group_b, router_expert_w, router_expert_b, ex_gate, ex_up, ex_down):
    bsz, seq, d = x.shape
    n_tok = bsz * seq
    depth = w_in.shape[0]
    c_ab = 4 * DN_WIDTH
    c_at = c_ab + 2 * DN_HEADS
    c_cv = c_at + 3 * AT_WIDTH
    bd_at = _block_diag_ones(AT_WIDTH, HEAD_DIM, BF16)
    bd_dn = _block_diag_ones(DN_WIDTH, HEAD_DIM, BF16)
    bias = _bias_tables(rel_bias)
    tri = (jnp.arange(ROW_TILE)[:, None] > jnp.arange(ROW_TILE)[None, :]).astype(BF16)
    per_head_lanes = lambda v: jnp.repeat(v, HEAD_DIM)[None, :]
    n_tiles = n_tok // MOE_TILE
    n_slots = _round_up(TOP_K * n_tok + n_tiles * N_EXPERTS * (MOE_PIECE - 1)
                        + N_EXPERTS * (MOE_BLOCK - 1), MOE_BLOCK)

    x2 = x.reshape(n_tok, d)
    for layer in range(depth):
        w_l = w_in[layer]
        dn, aq, ak, av, cu, ab = _proj(
            x2, norm_mix[layer][None, :],
            w_l[:, :c_ab].astype(BF16), w_l[:, c_at:c_cv].astype(BF16), w_l[:, c_cv:].astype(BF16),
            _pad_lanes(w_l[:, c_ab:c_at]).astype(BF16), bd_at,
            jnp.tile(at_q_norm[layer], AT_HEADS)[None, :] * (HEAD_DIM ** -0.5),
            jnp.tile(at_k_norm[layer], AT_HEADS)[None, :])

        y_dn = _deltanet(dn.reshape(bsz, seq, -1), ab.reshape(bsz, seq, LANES), dn_conv[layer],
                         per_head_lanes(dn_a_log[layer]), per_head_lanes(dn_dt_bias[layer]),
                         jnp.tile(dn_out_norm[layer], DN_HEADS)[None, :], bd_dn)
        y_at = _attention(aq.reshape(bsz, seq, -1), ak.reshape(bsz, seq, -1),
                          av.reshape(bsz, seq, -1), bias)
        y_cv = _conformer_conv(cu.reshape(bsz, seq, -1), cv_dw[layer], cv_dw_bias[layer][None, :],
                               cv_ln_g[layer][None, :], cv_ln_b[layer][None, :])

        wo = w_out[layer].astype(BF16)
        w_r = _pad_lanes(jnp.concatenate([router_expert_w[layer], router_group_w[layer]], axis=1))
        b_r = _pad_lanes(jnp.concatenate([router_expert_b[layer], router_group_b[layer]])[None, :])
        x_mid, h_ffn, route, counts = _out_proj(
            x2, y_dn.reshape(n_tok, DN_WIDTH), y_at.reshape(n_tok, AT_WIDTH),
            y_cv.reshape(n_tok, CV_WIDTH), wo[:DN_WIDTH], wo[DN_WIDTH:DN_WIDTH + AT_WIDTH],
            wo[DN_WIDTH + AT_WIDTH:], norm_ffn[layer][None, :], w_r.astype(BF16), b_r, tri)

        plan = _moe_plan(counts, n_slots)
        xs = _dispatch(plan, route, h_ffn, n_slots)
        y = _experts(plan, xs, ex_gate, ex_up, ex_down, layer)
        x2 = _combine(plan, route, x_mid, y)
    return x2.reshape(bsz, seq, d)
```

```python
import math

import jax
import jax.numpy as jnp
import numpy as np
from jax import lax
from jax.experimental import pallas as pl
from jax.experimental.pallas import tpu as pltpu

F32 = jnp.float32
BF16 = jnp.bfloat16
HIGHEST = lax.Precision.HIGHEST

EPS = 1e-6
NEG_INF = -1e30

HEAD_DIM = 64
DN_HEADS = 4
DN_WIDTH = DN_HEADS * HEAD_DIM
DN_CONV = 4
DN_CHUNK = 64
AT_HEADS = 8
AT_WIDTH = AT_HEADS * HEAD_DIM
AT_BLOCK = 128
WINDOWS = ((128, 1), (512, 4), (2048, 16))
MAX_DILATION = 16
AT_TILE = 256
N_BUCKETS = 32
MAX_DISTANCE = 2048
CV_WIDTH = 256
CV_KERNEL = 31
N_GROUPS = 4
EXPERTS_PER_GROUP = 8
N_EXPERTS = N_GROUPS * EXPERTS_PER_GROUP
TOP_K = 2

LANES = 128
SUBLANES = 8
VMEM_LIMIT = 52 * 1024 * 1024

ROW_TILE = 512
SEQ_TILE = 256
DN_TILE = 64
DN_PREP_CHUNKS = 4
MOE_BLOCK = 512
MOE_TILE = ROW_TILE
MOE_PIECE = 16
MOE_CHUNK = 256
TILE_SLOTS = -(-(TOP_K * MOE_TILE + N_EXPERTS * (MOE_PIECE - 1)) // MOE_CHUNK) * MOE_CHUNK
PIECES_MAX = TILE_SLOTS // MOE_PIECE
ZERO_PIECES = MOE_BLOCK // MOE_PIECE - 1


def _mm(a, b, precision=None):
    return jnp.dot(a, b, preferred_element_type=F32, precision=precision)


def _mm_nt(a, b):
    return lax.dot_general(a, b, (((1,), (1,)), ((), ())), preferred_element_type=F32)


def _mm_tn(a, b):
    return lax.dot_general(a, b, (((0,), (0,)), ((), ())), preferred_element_type=F32)


def _sigmoid(x):
    return 1.0 / (1.0 + jnp.exp(-x))


def _silu(x):
    return x * _sigmoid(x)


def _split3(x):
    p0 = x.astype(BF16)
    r1 = x - p0.astype(F32)
    p1 = r1.astype(BF16)
    p2 = (r1 - p1.astype(F32)).astype(BF16)
    return p0, p1, p2


def _full_spec(a):
    nd = a.ndim
    return pl.BlockSpec(a.shape, lambda *_: (0,) * nd)


def _block_diag_ones(width, block, dtype):
    r = jnp.arange(width)[:, None] // block
    c = jnp.arange(width)[None, :] // block
    return (r == c).astype(dtype)


def _proj_kernel(x_ref, g_ref, wdn_ref, wat_ref, wcv_ref, wab_ref, bd_ref, qn_ref, kn_ref,
                 dn_ref, aq_ref, ak_ref, av_ref, cv_ref, ab_ref):
    x = x_ref[...]
    ms = jnp.mean(x * x, axis=-1, keepdims=True)
    h = (x * lax.rsqrt(ms + EPS) * g_ref[...]).astype(BF16)
    dn_ref[...] = _mm(h, wdn_ref[...]).astype(BF16)
    cv_ref[...] = _mm(h, wcv_ref[...]).astype(BF16)
    ab_ref[...] = _mm(h, wab_ref[...])
    at = _mm(h, wat_ref[...])
    q = at[:, 0:AT_WIDTH]
    k = at[:, AT_WIDTH:2 * AT_WIDTH]
    bd = bd_ref[...]
    qms = _mm((q * q).astype(BF16), bd) * (1.0 / HEAD_DIM)
    kms = _mm((k * k).astype(BF16), bd) * (1.0 / HEAD_DIM)
    aq_ref[...] = (q * lax.rsqrt(qms + EPS) * qn_ref[...]).astype(BF16)
    ak_ref[...] = (k * lax.rsqrt(kms + EPS) * kn_ref[...]).astype(BF16)
    av_ref[...] = at[:, 2 * AT_WIDTH:3 * AT_WIDTH].astype(BF16)


def _proj(x2, g, wdn, wat, wcv, wab, bd, qn, kn):
    n, d = x2.shape
    row = lambda w: pl.BlockSpec((ROW_TILE, w), lambda i: (i, 0))
    widths = (wdn.shape[1], AT_WIDTH, AT_WIDTH, AT_WIDTH, wcv.shape[1], LANES)
    dtypes = (BF16, BF16, BF16, BF16, BF16, F32)
    return pl.pallas_call(
        _proj_kernel,
        grid=(n // ROW_TILE,),
        in_specs=[row(d)] + [_full_spec(a) for a in (g, wdn, wat, wcv, wab, bd, qn, kn)],
        out_specs=[row(w) for w in widths],
        out_shape=[jax.ShapeDtypeStruct((n, w), t) for w, t in zip(widths, dtypes)],
        compiler_params=pltpu.CompilerParams(dimension_semantics=("parallel",),
                                             vmem_limit_bytes=VMEM_LIMIT),
        name="proj",
    )(x2, g, wdn, wat, wcv, wab, bd, qn, kn)


def _per_head(x, block_mask):
    return jnp.where(block_mask, jnp.concatenate([x] * DN_HEADS, axis=0), jnp.zeros((), x.dtype))


def _dn_kernel(dn_ref, ab_ref, cw_ref, alog_ref, dtb_ref, onorm_ref, bd_ref, y_ref,
               xpad, qs, ks, vs, gs, bs, ws, us, qks, qds, kds, os_, st):
    seq = dn_ref.shape[1]
    n_tiles = seq // DN_TILE
    cw3 = 3 * DN_WIDTH
    pad = SUBLANES
    c = DN_CHUNK
    bd = bd_ref[...]
    hid = lax.broadcasted_iota(jnp.int32, (1, DN_WIDTH), 1) // HEAD_DIM

    def expand(cols, first):
        out = cols[:, first + DN_HEADS - 1:first + DN_HEADS]
        for h in range(DN_HEADS - 2, -1, -1):
            out = jnp.where(hid == h, cols[:, first + h:first + h + 1], out)
        return out

    xpad[0:pad, :] = jnp.zeros((pad, cw3), F32)
    for t in range(n_tiles):
        r0 = t * DN_TILE
        xpad[pad + r0:pad + r0 + DN_TILE, :] = dn_ref[0, r0:r0 + DN_TILE, 0:cw3].astype(F32)
    for t in range(n_tiles):
        r0 = t * DN_TILE
        ab = ab_ref[0, r0:r0 + DN_TILE, :]
        sp_in = expand(ab, 0) + dtb_ref[...]
        softplus = jnp.maximum(sp_in, 0.0) + jnp.log(1.0 + jnp.exp(-jnp.abs(sp_in)))
        gs[r0:r0 + DN_TILE, :] = -jnp.exp(alog_ref[...]) * softplus
        bs[r0:r0 + DN_TILE, :] = _sigmoid(expand(ab, DN_HEADS))
        acc = jnp.zeros((DN_TILE, cw3), F32)
        for j in range(DN_CONV):
            off = pad + r0 - (DN_CONV - 1) + j
            acc = acc + xpad[off:off + DN_TILE, :] * cw_ref[j:j + 1, :]
        y = _silu(acc)
        q = y[:, 0:DN_WIDTH]
        k = y[:, DN_WIDTH:2 * DN_WIDTH]
        qss = _mm((q * q).astype(BF16), bd)
        kss = _mm((k * k).astype(BF16), bd)
        qs[r0:r0 + DN_TILE, :] = q * lax.rsqrt(qss + EPS) * (HEAD_DIM ** -0.5)
        ks[r0:r0 + DN_TILE, :] = k * lax.rsqrt(kss + EPS)
        vs[r0:r0 + DN_TILE, :] = y[:, 2 * DN_WIDTH:3 * DN_WIDTH]

    ri = lax.broadcasted_iota(jnp.int32, (c, DN_WIDTH), 0)
    ci = lax.broadcasted_iota(jnp.int32, (c, DN_WIDTH), 1) % HEAD_DIM
    causal = ri >= ci
    strict = ri > ci
    eye_cat = (ri == ci).astype(F32)
    r2 = lax.broadcasted_iota(jnp.int32, (c, c), 0)
    c2 = lax.broadcasted_iota(jnp.int32, (c, c), 1)
    lower_ones = (r2 >= c2).astype(BF16)
    all_ones = jnp.ones((c, c), BF16)
    block_mask = (lax.broadcasted_iota(jnp.int32, (DN_WIDTH, DN_WIDTH), 0) // HEAD_DIM
                  == lax.broadcasted_iota(jnp.int32, (DN_WIDTH, DN_WIDTH), 1) // HEAD_DIM)

    def mm_exact_rhs(lhs_bf, x):
        p0, p1, p2 = _split3(x)
        return _mm(lhs_bf, p0) + _mm(lhs_bf, p1) + _mm(lhs_bf, p2)

    def mm_bd(lhs, rhs_cat):
        return _mm(lhs.astype(BF16), _per_head(rhs_cat.astype(BF16), block_mask))

    def prep(m, carry):
        rows = [pl.multiple_of((DN_PREP_CHUNKS * m + cc) * c, c) for cc in range(DN_PREP_CHUNKS)]
        g_cum = [mm_exact_rhs(lower_ones, gs[pl.ds(r, c), :]) for r in rows]
        g_row = [mm_exact_rhs(all_ones, g * eye_cat) for g in g_cum]
        decay = [jnp.exp(jnp.where(causal, g - gr, NEG_INF)) for g, gr in zip(g_cum, g_row)]
        kc = [ks[pl.ds(r, c), :] for r in rows]
        qc = [qs[pl.ds(r, c), :] for r in rows]
        beta = [bs[pl.ds(r, c), :] for r in rows]
        kb = [k * b for k, b in zip(kc, beta)]
        aq = [_mm_nt(jnp.concatenate([b_, q_], axis=0).astype(BF16),
                     _per_head(k_.astype(BF16), block_mask))
              for b_, q_, k_ in zip(kb, qc, kc)]
        a_mat = [jnp.where(strict, x[:c] * d, 0.0) for x, d in zip(aq, decay)]
        qk = [x[c:] * d for x, d in zip(aq, decay)]
        p = [eye_cat - a for a in a_mat]
        pw = [mm_bd(a, a) for a in a_mat]
        for _ in range(4):
            both = [mm_bd(jnp.concatenate([p_, x], axis=0), x) for p_, x in zip(p, pw)]
            p = [p_ + b_[:c] for p_, b_ in zip(p, both)]
            pw = [b_[c:] for b_ in both]
        p = [p_ + mm_bd(p_, x) for p_, x in zip(p, pw)]
        for i, r in enumerate(rows):
            e_g = jnp.exp(g_cum[i])
            ws[pl.ds(r, c), :] = mm_bd(p[i], kb[i] * e_g).astype(BF16)
            us[pl.ds(r, c), :] = mm_bd(p[i], vs[pl.ds(r, c), :] * beta[i])
            qks[pl.ds(r, c), :] = qk[i].astype(BF16)
            qds[pl.ds(r, c), :] = (qc[i] * e_g).astype(BF16)
            kds[pl.ds(r, c), :] = (kc[i] * jnp.exp(g_cum[i][c - 1:c, :] - g_cum[i])).astype(BF16)
            gs[pl.ds(r, c), :] = g_cum[i]
        return carry

    lax.fori_loop(0, seq // (DN_PREP_CHUNKS * c), prep, 0)

    st[...] = jnp.zeros(st.shape, F32)

    def scan(n, carry):
        r = pl.multiple_of(n * c, c)
        state = st[...]
        both = _mm(jnp.concatenate([ws[pl.ds(r, c), :], qds[pl.ds(r, c), :]], axis=0),
                   _per_head(state.astype(BF16), block_mask))
        v_new = (us[pl.ds(r, c), :] - both[:c]).astype(BF16)
        os_[pl.ds(r, c), :] = both[c:] + _mm(qks[pl.ds(r, c), :], _per_head(v_new, block_mask))
        kv = _mm_tn(kds[pl.ds(r, c), :], v_new)
        upd = kv[(DN_HEADS - 1) * c:DN_HEADS * c, :]
        for h in range(DN_HEADS - 2, -1, -1):
            upd = jnp.where(hid == h, kv[h * c:(h + 1) * c, :], upd)
        g_last = gs[pl.ds(r + c - SUBLANES, SUBLANES), :][SUBLANES - 1:SUBLANES, :]
        st[...] = state * jnp.exp(g_last) + upd
        return carry

    lax.fori_loop(0, seq // c, scan, 0)

    for t in range(n_tiles):
        r0 = t * DN_TILE
        o = os_[r0:r0 + DN_TILE, :]
        z = dn_ref[0, r0:r0 + DN_TILE, cw3:cw3 + DN_WIDTH].astype(F32)
        ms = _mm((o * o).astype(BF16), bd) * (1.0 / HEAD_DIM)
        y_ref[0, r0:r0 + DN_TILE, :] = (o * lax.rsqrt(ms + EPS) * onorm_ref[...] * _silu(z)).astype(BF16)


def _deltanet(dn, ab, conv_w, alog_cat, dtb_cat, onorm_cat, bd):
    b, seq, w = dn.shape
    f32buf = pltpu.VMEM((seq, DN_WIDTH), F32)
    bf16buf = pltpu.VMEM((seq, DN_WIDTH), BF16)
    return pl.pallas_call(
        _dn_kernel,
        grid=(b,),
        in_specs=[pl.BlockSpec((1, seq, w), lambda i: (i, 0, 0)),
                  pl.BlockSpec((1, seq, LANES), lambda i: (i, 0, 0))]
                 + [_full_spec(a) for a in (conv_w, alog_cat, dtb_cat, onorm_cat, bd)],
        out_specs=pl.BlockSpec((1, seq, DN_WIDTH), lambda i: (i, 0, 0)),
        out_shape=jax.ShapeDtypeStruct((b, seq, DN_WIDTH), BF16),
        scratch_shapes=[pltpu.VMEM((SUBLANES + seq, 3 * DN_WIDTH), F32),
                        f32buf, f32buf, f32buf, f32buf, f32buf,
                        bf16buf, f32buf, bf16buf, bf16buf, bf16buf,
                        f32buf,
                        pltpu.VMEM((DN_CHUNK, DN_WIDTH), F32)],
        compiler_params=pltpu.CompilerParams(dimension_semantics=("parallel",),
                                             vmem_limit_bytes=VMEM_LIMIT),
        name="deltanet",
    )(dn, ab, conv_w, alog_cat, dtb_cat, onorm_cat, bd)


def _run_starts(dil, r, i):
    if dil == 16:
        return [(tt * AT_TILE + r * 16, 16) for tt in range(AT_BLOCK * dil // AT_TILE)]
    if dil == 4:
        return [((2 * i + th) * AT_TILE + (4 * s + r) * 16, 16) for th in range(2) for s in range(4)]
    assert dil == 1
    return [((i // 2) * AT_TILE + rr * 16 + 8 * (i % 2), 8) for rr in range(16)]


def _run_order(dil):
    if dil == 16:
        return np.arange(AT_BLOCK)
    if dil == 4:
        th, s, ml = np.meshgrid(np.arange(2), np.arange(4), np.arange(16), indexing="ij")
        return (64 * th + 4 * ml + s).reshape(-1)
    rr, m8 = np.meshgrid(np.arange(16), np.arange(8), indexing="ij")
    return (16 * m8 + rr).reshape(-1)


def _load_runs(ref, p, runs):
    return jnp.concatenate([ref[p, pl.ds(pl.multiple_of(s, n), n), :] for s, n in runs], axis=0)


def _store_runs(ref, p, runs, val):
    off = 0
    for s, n in runs:
        ref[p, pl.ds(pl.multiple_of(s, n), n), :] = val[off:off + n]
        off += n


def _attn_kernel(q_ref, k_ref, v_ref, perm_ref, perm_t_ref, bias_ref, y_ref, qf, kf, vf, acc, ms, ls):
    seq = q_ref.shape[1]
    pairs = AT_HEADS // 2
    perm = perm_ref[...]
    for t in range(seq // AT_TILE):
        r0 = t * AT_TILE
        for src, dst in ((q_ref, qf), (k_ref, kf), (v_ref, vf)):
            rows = _mm(perm, src[0, r0:r0 + AT_TILE, :])
            for p in range(pairs):
                dst[p, r0:r0 + AT_TILE, :] = rows[:, p * LANES:(p + 1) * LANES]

    lane = lax.broadcasted_iota(jnp.int32, (1, LANES), 1)
    key_col = lax.broadcasted_iota(jnp.int32, (1, 2 * AT_BLOCK), 1)
    order = sorted(range(len(WINDOWS)), key=lambda g: -WINDOWS[g][1])
    for step, grp in enumerate(order):
        dil = WINDOWS[grp][1]
        nb = seq // dil // AT_BLOCK
        is_first = step == 0
        is_last = step == len(order) - 1

        def unit(u, carry, dil=dil, nb=nb, grp=grp, is_first=is_first, is_last=is_last):
            r = u // nb
            i = u % nb
            q_runs = _run_starts(dil, r, i)
            p_runs = _run_starts(dil, r, jnp.maximum(i - 1, 0))
            no_prev = jnp.where(jnp.logical_and(i == 0, key_col < AT_BLOCK), NEG_INF, 0.0)
            for p in range(pairs):
                q2 = _load_runs(qf, p, q_runs).astype(BF16)
                k2 = jnp.concatenate([_load_runs(kf, p, p_runs), _load_runs(kf, p, q_runs)],
                                     axis=0).astype(BF16)
                v2 = jnp.concatenate([_load_runs(vf, p, p_runs), _load_runs(vf, p, q_runs)],
                                     axis=0).astype(BF16)
                v_ext = jnp.concatenate([v2, jnp.ones(v2.shape, BF16)], axis=1)
                m_new = l_new = o_new = None
                for hh in range(2):
                    mask = (lane // HEAD_DIM) == hh
                    qm = jnp.where(mask, q2, jnp.zeros_like(q2))
                    s = _mm_nt(qm, k2) + bias_ref[grp, 2 * p + hh] + no_prev
                    m_h = jnp.max(s, axis=-1, keepdims=True)
                    pexp = jnp.exp(s - m_h).astype(BF16)
                    res = _mm(pexp, v_ext)
                    m_b = jnp.broadcast_to(m_h, (AT_BLOCK, LANES))
                    if hh == 0:
                        m_new, o_new, l_new = m_b, res[:, :LANES], res[:, LANES:]
                    else:
                        m_new = jnp.where(mask, m_b, m_new)
                        o_new = jnp.where(mask, res[:, :LANES], o_new)
                        l_new = jnp.where(mask, res[:, LANES:], l_new)
                if not is_first:
                    m_old = _load_runs(ms, p, q_runs)
                    m_tot = jnp.maximum(m_old, m_new)
                    a_old = jnp.exp(m_old - m_tot)
                    a_new = jnp.exp(m_new - m_tot)
                    l_new = a_old * _load_runs(ls, p, q_runs) + a_new * l_new
                    o_new = a_old * _load_runs(acc, p, q_runs) + a_new * o_new
                    m_new = m_tot
                if is_last:
                    _store_runs(acc, p, q_runs, o_new / l_new)
                else:
                    _store_runs(ms, p, q_runs, m_new)
                    _store_runs(ls, p, q_runs, l_new)
                    _store_runs(acc, p, q_runs, o_new)
            return carry

        lax.fori_loop(0, seq // AT_BLOCK, unit, 0)

    perm_t = perm_t_ref[...]
    for t in range(seq // AT_TILE):
        r0 = t * AT_TILE
        for p in range(pairs):
            y_ref[0, r0:r0 + AT_TILE, p * LANES:(p + 1) * LANES] = _mm(
                perm_t, acc[p, r0:r0 + AT_TILE, :].astype(BF16)).astype(BF16)


def _tile_permutation():
    t = np.arange(AT_TILE)
    row = (t % MAX_DILATION) * (AT_TILE // MAX_DILATION) + t // MAX_DILATION
    perm = np.zeros((AT_TILE, AT_TILE), np.float32)
    perm[row, t] = 1.0
    return perm


def _attention(aq, ak, av, bias):
    b, seq, w = aq.shape
    assert seq == AT_BLOCK * MAX_DILATION and [d for _, d in WINDOWS] == [1, 4, 16]
    slab = pltpu.VMEM((AT_HEADS // 2, seq, LANES), F32)
    tok = pl.BlockSpec((1, seq, w), lambda i: (i, 0, 0))
    perm = _tile_permutation()
    perm_in = jnp.asarray(perm, BF16)
    perm_out = jnp.asarray(perm.T, BF16)
    return pl.pallas_call(
        _attn_kernel,
        grid=(b,),
        in_specs=[tok, tok, tok, _full_spec(perm_in), _full_spec(perm_out), _full_spec(bias)],
        out_specs=tok,
        out_shape=jax.ShapeDtypeStruct((b, seq, w), BF16),
        scratch_shapes=[slab] * 6,
        compiler_params=pltpu.CompilerParams(dimension_semantics=("parallel",),
                                             vmem_limit_bytes=VMEM_LIMIT),
        name="dilated_attention",
    )(aq, ak, av, perm_in, perm_out, bias)


def _t5_bucket(dist):
    max_exact = N_BUCKETS // 2
    d = np.maximum(dist, 1).astype(np.float32)
    log_bucket = max_exact + (np.log(d / np.float32(max_exact))
                              / np.float32(math.log(MAX_DISTANCE / max_exact))
                              * np.float32(N_BUCKETS - max_exact)).astype(np.int32)
    return np.where(dist < max_exact, dist, np.minimum(log_bucket, N_BUCKETS - 1))


def _bias_tables(rel_bias):
    tabs = []
    for window, dil in WINDOWS:
        n_back = window // dil
        j = _run_order(dil)
        rel = j[:, None] + AT_BLOCK - np.concatenate([j, AT_BLOCK + j])[None, :]
        valid = (rel >= 0) & (rel <= n_back)
        bucket = _t5_bucket(dil * np.clip(rel, 0, n_back)).reshape(-1)
        onehot = (np.arange(N_BUCKETS)[:, None] == bucket[None, :]).astype(np.float32)
        bias = jnp.dot(rel_bias.astype(F32).T, jnp.asarray(onehot, BF16).astype(F32), precision=HIGHEST)
        bias = bias.reshape(AT_HEADS, AT_BLOCK, 2 * AT_BLOCK)
        tabs.append(jnp.where(jnp.asarray(valid)[None], bias, NEG_INF))
    return jnp.stack(tabs)


def _cv_kernel(cu_ref, dw_ref, dwb_ref, g_ref, b_ref, y_ref, ypad):
    seq = cu_ref.shape[1]
    n_tiles = seq // SEQ_TILE
    pad = 4 * SUBLANES
    ypad[0:pad, :] = jnp.zeros((pad, CV_WIDTH), F32)
    for t in range(n_tiles):
        r0 = t * SEQ_TILE
        u = cu_ref[0, r0:r0 + SEQ_TILE, :].astype(F32)
        ypad[pad + r0:pad + r0 + SEQ_TILE, :] = u[:, :CV_WIDTH] * _sigmoid(u[:, CV_WIDTH:])
    for t in range(n_tiles):
        r0 = t * SEQ_TILE
        acc = jnp.zeros((SEQ_TILE, CV_WIDTH), F32) + dwb_ref[...]
        for j in range(CV_KERNEL):
            off = pad + r0 - (CV_KERNEL - 1) + j
            acc = acc + ypad[off:off + SEQ_TILE, :] * dw_ref[j:j + 1, :]
        mu = jnp.mean(acc, axis=-1, keepdims=True)
        cen = acc - mu
        var = jnp.mean(cen * cen, axis=-1, keepdims=True)
        yn = cen * lax.rsqrt(var + EPS) * g_ref[...] + b_ref[...]
        y_ref[0, r0:r0 + SEQ_TILE, :] = _silu(yn).astype(BF16)


def _conformer_conv(cu, dw, dwb, ln_g, ln_b):
    b, seq, w = cu.shape
    return pl.pallas_call(
        _cv_kernel,
        grid=(b,),
        in_specs=[pl.BlockSpec((1, seq, w), lambda i: (i, 0, 0))]
                 + [_full_spec(a) for a in (dw, dwb, ln_g, ln_b)],
        out_specs=pl.BlockSpec((1, seq, CV_WIDTH), lambda i: (i, 0, 0)),
        out_shape=jax.ShapeDtypeStruct((b, seq, CV_WIDTH), BF16),
        scratch_shapes=[pltpu.VMEM((4 * SUBLANES + seq, CV_WIDTH), F32)],
        compiler_params=pltpu.CompilerParams(dimension_semantics=("parallel",),
                                             vmem_limit_bytes=VMEM_LIMIT),
        name="conformer_conv",
    )(cu, dw, dwb, ln_g, ln_b)


def _out_kernel(x_ref, ydn_ref, yat_ref, ycv_ref, wdn_ref, wat_ref, wcv_ref, g_ref, wr_ref, rb_ref,
                tri_ref, xo_ref, h_ref, route_ref, cnt_ref):
    x = (x_ref[...] + _mm(ydn_ref[...], wdn_ref[...]) + _mm(yat_ref[...], wat_ref[...])
         + _mm(ycv_ref[...], wcv_ref[...]))
    xo_ref[...] = x
    ms = jnp.mean(x * x, axis=-1, keepdims=True)
    h = x * lax.rsqrt(ms + EPS) * g_ref[...]
    h_ref[...] = h.astype(BF16)

    logits = _mm(h.astype(BF16), wr_ref[...]) + rb_ref[...]
    lane = lax.broadcasted_iota(jnp.int32, logits.shape, 1)
    is_group = (lane >= N_EXPERTS) & (lane < N_EXPERTS + N_GROUPS)
    gl = jnp.where(is_group, logits, NEG_INF)
    gmax = jnp.max(gl, axis=-1, keepdims=True)
    gsel = jnp.min(jnp.where(gl == gmax, lane, 2 * LANES), axis=-1, keepdims=True) - N_EXPERTS
    p_group = 1.0 / jnp.sum(jnp.where(is_group, jnp.exp(gl - gmax), 0.0), axis=-1, keepdims=True)
    lo = gsel * EXPERTS_PER_GROUP
    in_group = (lane >= lo) & (lane < lo + EXPERTS_PER_GROUP)
    el = jnp.where(in_group, logits, NEG_INF)
    v1 = jnp.max(el, axis=-1, keepdims=True)
    i1 = jnp.min(jnp.where(el == v1, lane, LANES), axis=-1, keepdims=True)
    el2 = jnp.where(lane == i1, NEG_INF, el)
    v2 = jnp.max(el2, axis=-1, keepdims=True)
    i2 = jnp.min(jnp.where(el2 == v2, lane, LANES), axis=-1, keepdims=True)
    t = jnp.exp(v2 - v1)
    g1 = p_group / (1.0 + t)
    g2 = g1 * t

    oh1 = lane == i1
    oh2 = lane == i2
    both = jnp.where(oh1 | oh2, 1.0, 0.0).astype(BF16)
    before = _mm(tri_ref[...], both)
    rank1 = jnp.sum(jnp.where(oh1, before, 0.0), axis=-1, keepdims=True)
    rank2 = jnp.sum(jnp.where(oh2, before, 0.0), axis=-1, keepdims=True)
    cnt_ref[0] = _mm(jnp.ones((SUBLANES, both.shape[0]), BF16), both)

    route = jnp.where(lane == 0, i1.astype(F32), 0.0)
    route = jnp.where(lane == 1, i2.astype(F32), route)
    route = jnp.where(lane == 2, g1, route)
    route = jnp.where(lane == 3, g2, route)
    route = jnp.where(lane == 4, rank1, route)
    route = jnp.where(lane == 5, rank2, route)
    route_ref[...] = route


def _out_proj(x2, ydn, yat, ycv, wdn, wat, wcv, g, wr, rb, tri):
    n, d = x2.shape
    row = lambda w: pl.BlockSpec((ROW_TILE, w), lambda i: (i, 0))
    n_tiles = n // ROW_TILE
    return pl.pallas_call(
        _out_kernel,
        grid=(n_tiles,),
        in_specs=[row(d), row(DN_WIDTH), row(AT_WIDTH), row(CV_WIDTH)]
                 + [_full_spec(a) for a in (wdn, wat, wcv, g, wr, rb, tri)],
        out_specs=[row(d), row(d), row(LANES), pl.BlockSpec((1, SUBLANES, LANES), lambda i: (i, 0, 0))],
        out_shape=[jax.ShapeDtypeStruct((n, d), F32), jax.ShapeDtypeStruct((n, d), BF16),
                   jax.ShapeDtypeStruct((n, LANES), F32),
                   jax.ShapeDtypeStruct((n_tiles, SUBLANES, LANES), F32)],
        compiler_params=pltpu.CompilerParams(dimension_semantics=("parallel",),
                                             vmem_limit_bytes=VMEM_LIMIT),
        name="out_proj",
    )(x2, ydn, yat, ycv, wdn, wat, wcv, g, wr, rb, tri)


def _tile_slots(route, loc_row):
    lane = lax.broadcasted_iota(jnp.int32, route.shape, 1).astype(F32)
    slots = []
    for kk in range(TOP_K):
        base = jnp.sum(jnp.where(lane == route[:, kk:kk + 1], loc_row, 0.0), axis=-1, keepdims=True)
        slots.append(base + route[:, 4 + kk:5 + kk])
    return slots


def _chunk_cols(j):
    return (j * MOE_CHUNK + lax.broadcasted_iota(jnp.int32, (1, MOE_CHUNK), 1)).astype(F32)


def _piece(ref, row):
    return ref.at[pl.ds(pl.multiple_of(row, MOE_PIECE), MOE_PIECE), :]


def _dispatch_kernel(dst_ref, np_ref, zdst_ref, zvalid_ref, nu_ref, route_ref, loc_ref, h_ref, xs_ref,
                     srt, zeros, sem, zsem):
    tile = pl.program_id(0)
    n_pieces = np_ref[tile]
    n_blocks = xs_ref.shape[0] // MOE_BLOCK

    def zero_piece(z):
        return pltpu.make_async_copy(_piece(zeros, 0), _piece(xs_ref, zdst_ref[z]), zsem)

    def zero_block(b):
        return pltpu.make_async_copy(
            zeros, xs_ref.at[pl.ds(pl.multiple_of(b * MOE_BLOCK, MOE_BLOCK), MOE_BLOCK), :], zsem)

    @pl.when(tile == 0)
    def _():
        zeros[...] = jnp.zeros(zeros.shape, BF16)
        for wait in (False, True):
            def piece_body(z, carry, wait=wait):
                @pl.when(zvalid_ref[z] != 0)
                def _():
                    zero_piece(z).wait() if wait else zero_piece(z).start()
                return carry

            def block_body(b, carry, wait=wait):
                zero_block(b).wait() if wait else zero_block(b).start()
                return carry

            lax.fori_loop(0, N_EXPERTS * ZERO_PIECES, piece_body, 0)
            lax.fori_loop(nu_ref[0], n_blocks, block_body, 0)

    slot1, slot2 = _tile_slots(route_ref[...], loc_ref[0, 0:1, :])
    lane = lax.broadcasted_iota(jnp.int32, (1, LANES), 1)
    slot_cols = jnp.where(lane == 0, slot1, jnp.where(lane == 1, slot2, 0.0))
    pick = (lax.broadcasted_iota(jnp.int32, (SUBLANES, LANES), 0)
            == lax.broadcasted_iota(jnp.int32, (SUBLANES, LANES), 1)).astype(BF16)
    slot_rows = sum(_mm_nt(pick, piece) for piece in _split3(slot_cols))
    slot1_row = slot_rows[0:1, :]
    slot2_row = slot_rows[1:2, :]
    h = h_ref[...]
    buf = srt.at[tile % 2]

    def chunk(j, carry):
        row = (j * MOE_CHUNK + lax.broadcasted_iota(jnp.int32, (MOE_CHUNK, 1), 0)).astype(F32)
        onehot = jnp.where(row == slot1_row, 1.0, jnp.where(row == slot2_row, 1.0, 0.0)).astype(BF16)
        buf[pl.ds(pl.multiple_of(j * MOE_CHUNK, MOE_CHUNK), MOE_CHUNK), :] = _mm(onehot, h).astype(BF16)
        return carry

    pieces_per_chunk = MOE_CHUNK // MOE_PIECE
    lax.fori_loop(0, (n_pieces + pieces_per_chunk - 1) // pieces_per_chunk, chunk, 0)

    def piece_copy(t, p):
        return pltpu.make_async_copy(_piece(srt.at[t % 2], p * MOE_PIECE),
                                     _piece(xs_ref, dst_ref[t * PIECES_MAX + p]), sem.at[t % 2])

    def start(p, carry):
        piece_copy(tile, p).start()
        return carry

    lax.fori_loop(0, n_pieces, start, 0)

    def wait_tile(t):
        def wait(p, carry):
            piece_copy(t, p).wait()
            return carry
        lax.fori_loop(0, np_ref[t], wait, 0)

    @pl.when(tile > 0)
    def _():
        wait_tile(tile - 1)

    @pl.when(tile == pl.num_programs(0) - 1)
    def _():
        wait_tile(tile)


def _dispatch(plan, route, h, n_slots):
    n, d = h.shape
    n_tiles = n // MOE_TILE
    grid_spec = pltpu.PrefetchScalarGridSpec(
        num_scalar_prefetch=5,
        grid=(n_tiles,),
        in_specs=[pl.BlockSpec((MOE_TILE, LANES), lambda i, *_: (i, 0)),
                  pl.BlockSpec((1, SUBLANES, LANES), lambda i, *_: (i, 0, 0)),
                  pl.BlockSpec((MOE_TILE, d), lambda i, *_: (i, 0))],
        out_specs=pl.BlockSpec(memory_space=pl.ANY),
        scratch_shapes=[pltpu.VMEM((2, TILE_SLOTS, d), BF16), pltpu.VMEM((MOE_BLOCK, d), BF16),
                        pltpu.SemaphoreType.DMA((2,)), pltpu.SemaphoreType.DMA(())],
    )
    return pl.pallas_call(
        _dispatch_kernel,
        grid_spec=grid_spec,
        out_shape=jax.ShapeDtypeStruct((n_slots, d), BF16),
        compiler_params=pltpu.CompilerParams(dimension_semantics=("arbitrary",),
                                             vmem_limit_bytes=VMEM_LIMIT),
        name="moe_dispatch",
    )(plan["dst"], plan["n_pieces"], plan["zdst"], plan["zvalid"], plan["n_used"], route,
      plan["loc"], h)


def _expert_kernel(be_ref, nu_ref, x_ref, wg_ref, wu_ref, wd_ref, y_ref, wg_bf, wu_bf, wd_bf):
    i = pl.program_id(0)
    changed = jnp.logical_or(i == 0, be_ref[i] != be_ref[jnp.maximum(i - 1, 0)])

    @pl.when(jnp.logical_and(changed, i < nu_ref[0]))
    def _():
        wg_bf[...] = wg_ref[0].astype(BF16)
        wu_bf[...] = wu_ref[0].astype(BF16)
        wd_bf[...] = wd_ref[0].astype(BF16)

    @pl.when(i < nu_ref[0])
    def _():
        x = x_ref[...]
        g = _mm(x, wg_bf[...])
        u = _mm(x, wu_bf[...])
        y_ref[...] = _mm((_silu(g) * u).astype(BF16), wd_bf[...]).astype(BF16)

    @pl.when(i >= nu_ref[0])
    def _():
        y_ref[...] = jnp.zeros(y_ref.shape, BF16)


def _experts(plan, xs, wg, wu, wd, layer):
    ns, d = xs.shape
    de = wg.shape[3]
    rows = lambda i, be, nu: (jnp.minimum(i, nu[0] - 1), 0)
    grid_spec = pltpu.PrefetchScalarGridSpec(
        num_scalar_prefetch=2,
        grid=(ns // MOE_BLOCK,),
        in_specs=[pl.BlockSpec((MOE_BLOCK, d), rows),
                  pl.BlockSpec((None, 1, d, de), lambda i, be, nu: (layer, be[i], 0, 0)),
                  pl.BlockSpec((None, 1, d, de), lambda i, be, nu: (layer, be[i], 0, 0)),
                  pl.BlockSpec((None, 1, de, d), lambda i, be, nu: (layer, be[i], 0, 0))],
        out_specs=pl.BlockSpec((MOE_BLOCK, d), lambda i, be, nu: (i, 0)),
        scratch_shapes=[pltpu.VMEM((d, de), BF16), pltpu.VMEM((d, de), BF16),
                        pltpu.VMEM((de, d), BF16)],
    )
    return pl.pallas_call(
        _expert_kernel,
        grid_spec=grid_spec,
        out_shape=jax.ShapeDtypeStruct((ns, d), BF16),
        compiler_params=pltpu.CompilerParams(dimension_semantics=("arbitrary",),
                                             vmem_limit_bytes=VMEM_LIMIT),
        name="moe_experts",
    )(plan["block_expert"], plan["n_used"], xs, wg, wu, wd)


def _combine_kernel(dst_ref, np_ref, route_ref, loc_ref, x_ref, y_ref, o_ref, ysrt, sem):
    tile = pl.program_id(0)
    n_pieces = np_ref[tile]

    @pl.when(tile == 0)
    def _():
        ysrt[...] = jnp.zeros(ysrt.shape, BF16)

    def piece_copy(t, p):
        return pltpu.make_async_copy(_piece(y_ref, dst_ref[t * PIECES_MAX + p]),
                                     _piece(ysrt.at[t % 2], p * MOE_PIECE), sem.at[t % 2])

    def fetch_tile(t):
        def start(p, carry):
            piece_copy(t, p).start()
            return carry
        lax.fori_loop(0, np_ref[t], start, 0)

    @pl.when(tile == 0)
    def _():
        fetch_tile(tile)

    @pl.when(tile + 1 < pl.num_programs(0))
    def _():
        fetch_tile(tile + 1)

    route = route_ref[...]
    slot1, slot2 = _tile_slots(route, loc_ref[0, 0:1, :])
    g1 = route[:, 2:3]
    g2 = route[:, 3:4]
    gates = jnp.concatenate(
        [jnp.where(_chunk_cols(j) == slot1, g1, jnp.where(_chunk_cols(j) == slot2, g2, 0.0)).astype(BF16)
         for j in range(TILE_SLOTS // MOE_CHUNK)], axis=1)

    def wait(p, carry):
        piece_copy(tile, p).wait()
        return carry

    lax.fori_loop(0, n_pieces, wait, 0)
    o_ref[...] = x_ref[...] + _mm(gates, ysrt[tile % 2])


def _combine(plan, route, x2, y):
    n, d = x2.shape
    grid_spec = pltpu.PrefetchScalarGridSpec(
        num_scalar_prefetch=2,
        grid=(n // MOE_TILE,),
        in_specs=[pl.BlockSpec((MOE_TILE, LANES), lambda i, *_: (i, 0)),
                  pl.BlockSpec((1, SUBLANES, LANES), lambda i, *_: (i, 0, 0)),
                  pl.BlockSpec((MOE_TILE, d), lambda i, *_: (i, 0)),
                  pl.BlockSpec(memory_space=pl.ANY)],
        out_specs=pl.BlockSpec((MOE_TILE, d), lambda i, *_: (i, 0)),
        scratch_shapes=[pltpu.VMEM((2, TILE_SLOTS, d), BF16), pltpu.SemaphoreType.DMA((2,))],
    )
    return pl.pallas_call(
        _combine_kernel,
        grid_spec=grid_spec,
        out_shape=jax.ShapeDtypeStruct((n, d), F32),
        compiler_params=pltpu.CompilerParams(dimension_semantics=("arbitrary",),
                                             vmem_limit_bytes=VMEM_LIMIT),
        name="moe_combine",
    )(plan["dst"], plan["n_pieces"], route, plan["loc"], x2, y)


def _round_up(v, m):
    return (v + m - 1) // m * m


def _moe_plan(counts, n_slots):
    n_tiles = counts.shape[0]
    cnt = counts[:, 0, :N_EXPERTS].astype(jnp.int32)
    seg = _round_up(cnt, MOE_PIECE)
    loc_end = jnp.cumsum(seg, axis=1)
    loc_start = loc_end - seg
    totals = jnp.sum(seg, axis=0)
    padded = _round_up(totals, MOE_BLOCK)
    pad_end = jnp.cumsum(padded)
    pad_start = pad_end - padded
    seg_start = pad_start[None, :] + jnp.cumsum(seg, axis=0) - seg
    piece_off = jnp.arange(PIECES_MAX, dtype=jnp.int32) * MOE_PIECE
    piece_e = jnp.sum(loc_end[:, None, :] <= piece_off[None, :, None], axis=2)
    hit = piece_e[..., None] == jnp.arange(N_EXPERTS)
    shift = jnp.sum(jnp.where(hit, (seg_start - loc_start)[:, None, :], 0), axis=2)
    valid = piece_off[None, :] < loc_end[:, -1:]
    dst = jnp.where(valid, shift + piece_off[None, :], 0).astype(jnp.int32).reshape(-1)
    zk = jnp.arange(ZERO_PIECES, dtype=jnp.int32)[None, :] * MOE_PIECE
    zvalid = zk < (padded - totals)[:, None]
    zdst = jnp.where(zvalid, (pad_start + totals)[:, None] + zk, 0)
    blk_start = jnp.arange(n_slots // MOE_BLOCK, dtype=jnp.int32) * MOE_BLOCK
    block_expert = jnp.minimum(jnp.sum(pad_end[None, :] <= blk_start[:, None], axis=1), N_EXPERTS - 1)
    loc = jnp.pad(loc_start.astype(F32), ((0, 0), (0, LANES - N_EXPERTS)))
    return {
        "dst": dst,
        "n_pieces": (loc_end[:, -1] // MOE_PIECE).astype(jnp.int32),
        "zdst": zdst.astype(jnp.int32).reshape(-1),
        "zvalid": zvalid.astype(jnp.int32).reshape(-1),
        "loc": jnp.broadcast_to(loc[:, None, :], (n_tiles, SUBLANES, LANES)),
        "block_expert": block_expert.astype(jnp.int32),
        "n_used": (pad_end[-1:] // MOE_BLOCK).astype(jnp.int32),
    }


def _pad_lanes(a, width=LANES):
    return jnp.pad(a, [(0, 0)] * (a.ndim - 1) + [(0, width - a.shape[-1])])


def kernel(x, norm_mix, w_in, dn_conv, dn_a_log, dn_dt_bias, dn_out_norm, at_q_norm, at_k_norm,
           rel_bias, cv_dw, cv_dw_bias, cv_ln_g, cv_ln_b, w_out, norm_ffn, router_group_w,
           router_group_b, router_expert_w, router_expert_b, ex_gate, ex_up, ex_down):
    bsz, seq, d = x.shape
    n_tok = bsz * seq
    depth = w_in.shape[0]
    c_ab = 4 * DN_WIDTH
    c_at = c_ab + 2 * DN_HEADS
    c_cv = c_at + 3 * AT_WIDTH
    bd_at = _block_diag_ones(AT_WIDTH, HEAD_DIM, BF16)
    bd_dn = _block_diag_ones(DN_WIDTH, HEAD_DIM, BF16)
    bias = _bias_tables(rel_bias)
    tri = (jnp.arange(ROW_TILE)[:, None] > jnp.arange(ROW_TILE)[None, :]).astype(BF16)
    per_head_lanes = lambda v: jnp.repeat(v, HEAD_DIM)[None, :]
    n_tiles = n_tok // MOE_TILE
    n_slots = _round_up(TOP_K * n_tok + n_tiles * N_EXPERTS * (MOE_PIECE - 1)
                        + N_EXPERTS * (MOE_BLOCK - 1), MOE_BLOCK)

    x2 = x.reshape(n_tok, d)
    for layer in range(depth):
        w_l = w_in[layer]
        dn, aq, ak, av, cu, ab = _proj(
            x2, norm_mix[layer][None, :],
            w_l[:, :c_ab].astype(BF16), w_l[:, c_at:c_cv].astype(BF16), w_l[:, c_cv:].astype(BF16),
            _pad_lanes(w_l[:, c_ab:c_at]).astype(BF16), bd_at,
            jnp.tile(at_q_norm[layer], AT_HEADS)[None, :] * (HEAD_DIM ** -0.5),
            jnp.tile(at_k_norm[layer], AT_HEADS)[None, :])

        y_dn = _deltanet(dn.reshape(bsz, seq, -1), ab.reshape(bsz, seq, LANES), dn_conv[layer],
                         per_head_lanes(dn_a_log[layer]), per_head_lanes(dn_dt_bias[layer]),
                         jnp.tile(dn_out_norm[layer], DN_HEADS)[None, :], bd_dn)
        y_at = _attention(aq.reshape(bsz, seq, -1), ak.reshape(bsz, seq, -1),
                          av.reshape(bsz, seq, -1), bias)
        y_cv = _conformer_conv(cu.reshape(bsz, seq, -1), cv_dw[layer], cv_dw_bias[layer][None, :],
                               cv_ln_g[layer][None, :], cv_ln_b[layer][None, :])

        wo = w_out[layer].astype(BF16)
        w_r = _pad_lanes(jnp.concatenate([router_expert_w[layer], router_group_w[layer]], axis=1))
        b_r = _pad_lanes(jnp.concatenate([router_expert_b[layer], router_group_b[layer]])[None, :])
        x_mid, h_ffn, route, counts = _out_proj(
            x2, y_dn.reshape(n_tok, DN_WIDTH), y_at.reshape(n_tok, AT_WIDTH),
            y_cv.reshape(n_tok, CV_WIDTH), wo[:DN_WIDTH], wo[DN_WIDTH:DN_WIDTH + AT_WIDTH],
            wo[DN_WIDTH + AT_WIDTH:], norm_ffn[layer][None, :], w_r.astype(BF16), b_r, tri)

        plan = _moe_plan(counts, n_slots)
        xs = _dispatch(plan, route, h_ffn, n_slots)
        y = _experts(plan, xs, ex_gate, ex_up, ex_down, layer)
        x2 = _combine(plan, route, x_mid, y)
    return x2.reshape(bsz, seq, d)
```

```python
import math

import jax
import jax.numpy as jnp
import numpy as np
from jax import lax
from jax.experimental import pallas as pl
from jax.experimental.pallas import tpu as pltpu

F32 = jnp.float32
BF16 = jnp.bfloat16
HIGHEST = lax.Precision.HIGHEST

EPS = 1e-6
NEG_INF = -1e30

HEAD_DIM = 64
DN_HEADS = 4
DN_WIDTH = DN_HEADS * HEAD_DIM
DN_CONV = 4
DN_CHUNK = 64
AT_HEADS = 8
AT_WIDTH = AT_HEADS * HEAD_DIM
AT_BLOCK = 128
WINDOWS = ((128, 1), (512, 4), (2048, 16))
MAX_DILATION = 16
AT_TILE = 256
AT_LOCKSTEP = 4
N_BUCKETS = 32
MAX_DISTANCE = 2048
CV_WIDTH = 256
CV_KERNEL = 31
N_GROUPS = 4
EXPERTS_PER_GROUP = 8
N_EXPERTS = N_GROUPS * EXPERTS_PER_GROUP
TOP_K = 2

LANES = 128
SUBLANES = 8
VMEM_LIMIT = 52 * 1024 * 1024

ROW_TILE = 512
SEQ_TILE = 256
DN_TILE = 64
DN_PREP_CHUNKS = 4
MOE_BLOCK = 512
MOE_TILE = ROW_TILE
MOE_PIECE = 16
MOE_CHUNK = 256
TILE_SLOTS = -(-(TOP_K * MOE_TILE + N_EXPERTS * (MOE_PIECE - 1)) // MOE_CHUNK) * MOE_CHUNK
PIECES_MAX = TILE_SLOTS // MOE_PIECE
ZERO_PIECES = MOE_BLOCK // MOE_PIECE - 1


def _mm(a, b, precision=None):
    return jnp.dot(a, b, preferred_element_type=F32, precision=precision)


def _mm_nt(a, b):
    return lax.dot_general(a, b, (((1,), (1,)), ((), ())), preferred_element_type=F32)


def _mm_tn(a, b):
    return lax.dot_general(a, b, (((0,), (0,)), ((), ())), preferred_element_type=F32)


def _sigmoid(x):
    return 1.0 / (1.0 + jnp.exp(-x))


def _silu(x):
    return x * _sigmoid(x)


def _split3(x):
    p0 = x.astype(BF16)
    r1 = x - p0.astype(F32)
    p1 = r1.astype(BF16)
    p2 = (r1 - p1.astype(F32)).astype(BF16)
    return p0, p1, p2


def _alternate(*stages):
    live = list(stages)
    while live:
        for gen in list(live):
            if next(gen, live) is live:
                live.remove(gen)


def _full_spec(a):
    nd = a.ndim
    return pl.BlockSpec(a.shape, lambda *_: (0,) * nd)


def _block_diag_ones(width, block, dtype):
    r = jnp.arange(width)[:, None] // block
    c = jnp.arange(width)[None, :] // block
    return (r == c).astype(dtype)


def _proj_kernel(x_ref, g_ref, wdn_ref, wat_ref, wcv_ref, wab_ref, bd_ref, qn_ref, kn_ref,
                 dn_ref, aq_ref, ak_ref, av_ref, cv_ref, ab_ref):
    x = x_ref[...]
    ms = jnp.mean(x * x, axis=-1, keepdims=True)
    h = (x * lax.rsqrt(ms + EPS) * g_ref[...]).astype(BF16)
    dn_ref[...] = _mm(h, wdn_ref[...]).astype(BF16)
    cv_ref[...] = _mm(h, wcv_ref[...]).astype(BF16)
    ab_ref[...] = _mm(h, wab_ref[...])
    at = _mm(h, wat_ref[...])
    q = at[:, 0:AT_WIDTH]
    k = at[:, AT_WIDTH:2 * AT_WIDTH]
    bd = bd_ref[...]
    qms = _mm((q * q).astype(BF16), bd) * (1.0 / HEAD_DIM)
    kms = _mm((k * k).astype(BF16), bd) * (1.0 / HEAD_DIM)
    aq_ref[...] = (q * lax.rsqrt(qms + EPS) * qn_ref[...]).astype(BF16)
    ak_ref[...] = (k * lax.rsqrt(kms + EPS) * kn_ref[...]).astype(BF16)
    av_ref[...] = at[:, 2 * AT_WIDTH:3 * AT_WIDTH].astype(BF16)


def _proj(x2, g, wdn, wat, wcv, wab, bd, qn, kn):
    n, d = x2.shape
    row = lambda w: pl.BlockSpec((ROW_TILE, w), lambda i: (i, 0))
    widths = (wdn.shape[1], AT_WIDTH, AT_WIDTH, AT_WIDTH, wcv.shape[1], LANES)
    dtypes = (BF16, BF16, BF16, BF16, BF16, F32)
    return pl.pallas_call(
        _proj_kernel,
        grid=(n // ROW_TILE,),
        in_specs=[row(d)] + [_full_spec(a) for a in (g, wdn, wat, wcv, wab, bd, qn, kn)],
        out_specs=[row(w) for w in widths],
        out_shape=[jax.ShapeDtypeStruct((n, w), t) for w, t in zip(widths, dtypes)],
        compiler_params=pltpu.CompilerParams(dimension_semantics=("parallel",),
                                             vmem_limit_bytes=VMEM_LIMIT),
        name="proj",
    )(x2, g, wdn, wat, wcv, wab, bd, qn, kn)


def _per_head(x, block_mask):
    return jnp.where(block_mask, jnp.concatenate([x] * DN_HEADS, axis=0), jnp.zeros((), x.dtype))


def _dn_kernel(dn_ref, ab_ref, cw_ref, alog_ref, dtb_ref, onorm_ref, bd_ref, y_ref,
               xpad, qs, ks, vs, gs, bs, os_, st, *group_bufs):
    set_a, set_b = group_bufs[:6], group_bufs[6:]
    seq = dn_ref.shape[1]
    n_tiles = seq // DN_TILE
    cw3 = 3 * DN_WIDTH
    pad = SUBLANES
    c = DN_CHUNK
    bd = bd_ref[...]
    hid = lax.broadcasted_iota(jnp.int32, (1, DN_WIDTH), 1) // HEAD_DIM

    def expand(cols, first):
        out = cols[:, first + DN_HEADS - 1:first + DN_HEADS]
        for h in range(DN_HEADS - 2, -1, -1):
            out = jnp.where(hid == h, cols[:, first + h:first + h + 1], out)
        return out

    xpad[0:pad, :] = jnp.zeros((pad, cw3), F32)
    for t in range(n_tiles):
        r0 = t * DN_TILE
        xpad[pad + r0:pad + r0 + DN_TILE, :] = dn_ref[0, r0:r0 + DN_TILE, 0:cw3].astype(F32)
    for t in range(n_tiles):
        r0 = t * DN_TILE
        ab = ab_ref[0, r0:r0 + DN_TILE, :]
        sp_in = expand(ab, 0) + dtb_ref[...]
        softplus = jnp.maximum(sp_in, 0.0) + jnp.log(1.0 + jnp.exp(-jnp.abs(sp_in)))
        gs[r0:r0 + DN_TILE, :] = -jnp.exp(alog_ref[...]) * softplus
        bs[r0:r0 + DN_TILE, :] = _sigmoid(expand(ab, DN_HEADS))
        acc = jnp.zeros((DN_TILE, cw3), F32)
        for j in range(DN_CONV):
            off = pad + r0 - (DN_CONV - 1) + j
            acc = acc + xpad[off:off + DN_TILE, :] * cw_ref[j:j + 1, :]
        y = _silu(acc)
        q = y[:, 0:DN_WIDTH]
        k = y[:, DN_WIDTH:2 * DN_WIDTH]
        qss = _mm((q * q).astype(BF16), bd)
        kss = _mm((k * k).astype(BF16), bd)
        qs[r0:r0 + DN_TILE, :] = q * lax.rsqrt(qss + EPS) * (HEAD_DIM ** -0.5)
        ks[r0:r0 + DN_TILE, :] = k * lax.rsqrt(kss + EPS)
        vs[r0:r0 + DN_TILE, :] = y[:, 2 * DN_WIDTH:3 * DN_WIDTH]

    ri = lax.broadcasted_iota(jnp.int32, (c, DN_WIDTH), 0)
    ci = lax.broadcasted_iota(jnp.int32, (c, DN_WIDTH), 1) % HEAD_DIM
    causal = ri >= ci
    strict = ri > ci
    eye_cat = (ri == ci).astype(F32)
    r2 = lax.broadcasted_iota(jnp.int32, (c, c), 0)
    c2 = lax.broadcasted_iota(jnp.int32, (c, c), 1)
    lower_ones = (r2 >= c2).astype(BF16)
    all_ones = jnp.ones((c, c), BF16)
    block_mask = (lax.broadcasted_iota(jnp.int32, (DN_WIDTH, DN_WIDTH), 0) // HEAD_DIM
                  == lax.broadcasted_iota(jnp.int32, (DN_WIDTH, DN_WIDTH), 1) // HEAD_DIM)

    def mm_exact_rhs(lhs_bf, x):
        p0, p1, p2 = _split3(x)
        return _mm(lhs_bf, p0) + _mm(lhs_bf, p1) + _mm(lhs_bf, p2)

    def mm_bd(lhs, rhs_cat):
        return _mm(lhs.astype(BF16), _per_head(rhs_cat.astype(BF16), block_mask))

    def prep(m, dst):
        ws, us, qks, qds, kds, gls = dst
        rows = [pl.multiple_of((DN_PREP_CHUNKS * m + cc) * c, c) for cc in range(DN_PREP_CHUNKS)]
        g_cum = [mm_exact_rhs(lower_ones, gs[pl.ds(r, c), :]) for r in rows]
        yield
        g_row = [mm_exact_rhs(all_ones, g * eye_cat) for g in g_cum]
        yield
        decay = [jnp.exp(jnp.where(causal, g - gr, NEG_INF)) for g, gr in zip(g_cum, g_row)]
        kc = [ks[pl.ds(r, c), :] for r in rows]
        qc = [qs[pl.ds(r, c), :] for r in rows]
        beta = [bs[pl.ds(r, c), :] for r in rows]
        kb = [k * b for k, b in zip(kc, beta)]
        aq = [_mm_nt(jnp.concatenate([b_, q_], axis=0).astype(BF16),
                     _per_head(k_.astype(BF16), block_mask))
              for b_, q_, k_ in zip(kb, qc, kc)]
        a_mat = [jnp.where(strict, x[:c] * d, 0.0) for x, d in zip(aq, decay)]
        qk = [x[c:] * d for x, d in zip(aq, decay)]
        p = [eye_cat - a for a in a_mat]
        yield
        pw = [mm_bd(a, a) for a in a_mat]
        for _ in range(4):
            yield
            both = [mm_bd(jnp.concatenate([p_, x], axis=0), x) for p_, x in zip(p, pw)]
            p = [p_ + b_[:c] for p_, b_ in zip(p, both)]
            pw = [b_[c:] for b_ in both]
        yield
        p = [p_ + mm_bd(p_, x) for p_, x in zip(p, pw)]
        yield
        for i, r in enumerate(rows):
            e_g = jnp.exp(g_cum[i])
            sl = slice(i * c, (i + 1) * c)
            ws[sl, :] = mm_bd(p[i], kb[i] * e_g).astype(BF16)
            us[sl, :] = mm_bd(p[i], vs[pl.ds(r, c), :] * beta[i])
            qks[sl, :] = qk[i].astype(BF16)
            qds[sl, :] = (qc[i] * e_g).astype(BF16)
            g_last = g_cum[i][c - 1:c, :]
            kds[sl, :] = (kc[i] * jnp.exp(g_last - g_cum[i])).astype(BF16)
            gls[i * SUBLANES:(i + 1) * SUBLANES, :] = jnp.broadcast_to(jnp.exp(g_last),
                                                                       (SUBLANES, DN_WIDTH))

    def scan_group(m, src):
        ws, us, qks, qds, kds, gls = src
        for i in range(DN_PREP_CHUNKS):
            sl = slice(i * c, (i + 1) * c)
            r = pl.multiple_of((DN_PREP_CHUNKS * m + i) * c, c)
            state = st[...]
            both = _mm(jnp.concatenate([ws[sl, :], qds[sl, :]], axis=0),
                       _per_head(state.astype(BF16), block_mask))
            yield
            v_new = (us[sl, :] - both[:c]).astype(BF16)
            os_[pl.ds(r, c), :] = both[c:] + _mm(qks[sl, :], _per_head(v_new, block_mask))
            kv = _mm_tn(kds[sl, :], v_new)
            upd = kv[(DN_HEADS - 1) * c:DN_HEADS * c, :]
            for h in range(DN_HEADS - 2, -1, -1):
                upd = jnp.where(hid == h, kv[h * c:(h + 1) * c, :], upd)
            yield
            st[...] = state * gls[i * SUBLANES:i * SUBLANES + 1, :] + upd

    alternate = _alternate

    st[...] = jnp.zeros(st.shape, F32)
    n_groups = seq // (DN_PREP_CHUNKS * c)
    assert n_groups % 2 == 0
    alternate(prep(0, set_a))

    def pair(j, carry):
        alternate(prep(2 * j + 1, set_b), scan_group(2 * j, set_a))
        alternate(prep(2 * j + 2, set_a), scan_group(2 * j + 1, set_b))
        return carry

    lax.fori_loop(0, n_groups // 2 - 1, pair, 0)
    alternate(prep(n_groups - 1, set_b), scan_group(n_groups - 2, set_a))
    alternate(scan_group(n_groups - 1, set_b))

    for t in range(n_tiles):
        r0 = t * DN_TILE
        o = os_[r0:r0 + DN_TILE, :]
        z = dn_ref[0, r0:r0 + DN_TILE, cw3:cw3 + DN_WIDTH].astype(F32)
        ms = _mm((o * o).astype(BF16), bd) * (1.0 / HEAD_DIM)
        y_ref[0, r0:r0 + DN_TILE, :] = (o * lax.rsqrt(ms + EPS) * onorm_ref[...] * _silu(z)).astype(BF16)


def _deltanet(dn, ab, conv_w, alog_cat, dtb_cat, onorm_cat, bd):
    b, seq, w = dn.shape
    f32buf = pltpu.VMEM((seq, DN_WIDTH), F32)
    group_rows = DN_PREP_CHUNKS * DN_CHUNK
    group_set = [pltpu.VMEM((group_rows, DN_WIDTH), BF16),
                 pltpu.VMEM((group_rows, DN_WIDTH), F32),
                 pltpu.VMEM((group_rows, DN_WIDTH), BF16),
                 pltpu.VMEM((group_rows, DN_WIDTH), BF16),
                 pltpu.VMEM((group_rows, DN_WIDTH), BF16),
                 pltpu.VMEM((DN_PREP_CHUNKS * SUBLANES, DN_WIDTH), F32)]
    return pl.pallas_call(
        _dn_kernel,
        grid=(b,),
        in_specs=[pl.BlockSpec((1, seq, w), lambda i: (i, 0, 0)),
                  pl.BlockSpec((1, seq, LANES), lambda i: (i, 0, 0))]
                 + [_full_spec(a) for a in (conv_w, alog_cat, dtb_cat, onorm_cat, bd)],
        out_specs=pl.BlockSpec((1, seq, DN_WIDTH), lambda i: (i, 0, 0)),
        out_shape=jax.ShapeDtypeStruct((b, seq, DN_WIDTH), BF16),
        scratch_shapes=[pltpu.VMEM((SUBLANES + seq, 3 * DN_WIDTH), F32),
                        f32buf, f32buf, f32buf, f32buf, f32buf,
                        f32buf,
                        pltpu.VMEM((DN_CHUNK, DN_WIDTH), F32)]
                       + group_set + group_set,
        compiler_params=pltpu.CompilerParams(dimension_semantics=("parallel",),
                                             vmem_limit_bytes=VMEM_LIMIT),
        name="deltanet",
    )(dn, ab, conv_w, alog_cat, dtb_cat, onorm_cat, bd)


def _run_starts(dil, r, i):
    if dil == 16:
        return [(tt * AT_TILE + r * 16, 16) for tt in range(AT_BLOCK * dil // AT_TILE)]
    if dil == 4:
        return [((2 * i + th) * AT_TILE + (4 * s + r) * 16, 16) for th in range(2) for s in range(4)]
    assert dil == 1
    return [((i // 2) * AT_TILE + rr * 16 + 8 * (i % 2), 8) for rr in range(16)]


def _run_order(dil):
    if dil == 16:
        return np.arange(AT_BLOCK)
    if dil == 4:
        th, s, ml = np.meshgrid(np.arange(2), np.arange(4), np.arange(16), indexing="ij")
        return (64 * th + 4 * ml + s).reshape(-1)
    rr, m8 = np.meshgrid(np.arange(16), np.arange(8), indexing="ij")
    return (16 * m8 + rr).reshape(-1)


def _load_runs(ref, p, runs):
    return jnp.concatenate([ref[p, pl.ds(pl.multiple_of(s, n), n), :] for s, n in runs], axis=0)


def _store_runs(ref, p, runs, val):
    off = 0
    for s, n in runs:
        ref[p, pl.ds(pl.multiple_of(s, n), n), :] = val[off:off + n]
        off += n


def _attn_kernel(q_ref, k_ref, v_ref, perm_ref, perm_t_ref, bias_ref, y_ref, qf, kf, vf, acc, ms, ls):
    seq = q_ref.shape[1]
    pairs = AT_HEADS // 2
    perm = perm_ref[...]
    for t in range(seq // AT_TILE):
        r0 = t * AT_TILE
        for src, dst in ((q_ref, qf), (k_ref, kf), (v_ref, vf)):
            rows = _mm(perm, src[0, r0:r0 + AT_TILE, :])
            for p in range(pairs):
                dst[p, r0:r0 + AT_TILE, :] = rows[:, p * LANES:(p + 1) * LANES]

    lane = lax.broadcasted_iota(jnp.int32, (1, LANES), 1)
    key_col = lax.broadcasted_iota(jnp.int32, (1, 2 * AT_BLOCK), 1)
    order = sorted(range(len(WINDOWS)), key=lambda g: -WINDOWS[g][1])
    for step, grp in enumerate(order):
        dil = WINDOWS[grp][1]
        nb = seq // dil // AT_BLOCK
        is_first = step == 0
        is_last = step == len(order) - 1

        def unit(u, carry, dil=dil, nb=nb, grp=grp, is_first=is_first, is_last=is_last):
            r = u // nb
            i = u % nb
            q_runs = _run_starts(dil, r, i)
            p_runs = _run_starts(dil, r, jnp.maximum(i - 1, 0))
            no_prev = jnp.where(jnp.logical_and(i == 0, key_col < AT_BLOCK), NEG_INF, 0.0)
            def head_pair(p):
                q2 = _load_runs(qf, p, q_runs).astype(BF16)
                k2 = jnp.concatenate([_load_runs(kf, p, p_runs), _load_runs(kf, p, q_runs)],
                                     axis=0).astype(BF16)
                v2 = jnp.concatenate([_load_runs(vf, p, p_runs), _load_runs(vf, p, q_runs)],
                                     axis=0).astype(BF16)
                v_ext = jnp.concatenate([v2, jnp.ones(v2.shape, BF16)], axis=1)
                masks = [(lane // HEAD_DIM) == hh for hh in range(2)]
                scores = [_mm_nt(jnp.where(mask, q2, jnp.zeros_like(q2)), k2)
                          + bias_ref[grp, 2 * p + hh] + no_prev for hh, mask in enumerate(masks)]
                yield
                maxes = [jnp.max(s, axis=-1, keepdims=True) for s in scores]
                probs = [jnp.exp(s - m_h).astype(BF16) for s, m_h in zip(scores, maxes)]
                yield
                results = [_mm(pexp, v_ext) for pexp in probs]
                yield
                m_new = jnp.where(masks[1], maxes[1], jnp.broadcast_to(maxes[0], (AT_BLOCK, LANES)))
                o_new = jnp.where(masks[1], results[1][:, :LANES], results[0][:, :LANES])
                l_new = jnp.where(masks[1], results[1][:, LANES:], results[0][:, LANES:])
                if not is_first:
                    m_old = _load_runs(ms, p, q_runs)
                    m_tot = jnp.maximum(m_old, m_new)
                    a_old = jnp.exp(m_old - m_tot)
                    a_new = jnp.exp(m_new - m_tot)
                    l_new = a_old * _load_runs(ls, p, q_runs) + a_new * l_new
                    o_new = a_old * _load_runs(acc, p, q_runs) + a_new * o_new
                    m_new = m_tot
                if is_last:
                    _store_runs(acc, p, q_runs, o_new / l_new)
                else:
                    _store_runs(ms, p, q_runs, m_new)
                    _store_runs(ls, p, q_runs, l_new)
                    _store_runs(acc, p, q_runs, o_new)

            for p0 in range(0, pairs, AT_LOCKSTEP):
                _alternate(*[head_pair(p) for p in range(p0, p0 + AT_LOCKSTEP)])
            return carry

        lax.fori_loop(0, seq // AT_BLOCK, unit, 0)

    perm_t = perm_t_ref[...]
    for t in range(seq // AT_TILE):
        r0 = t * AT_TILE
        for p in range(pairs):
            y_ref[0, r0:r0 + AT_TILE, p * LANES:(p + 1) * LANES] = _mm(
                perm_t, acc[p, r0:r0 + AT_TILE, :].astype(BF16)).astype(BF16)


def _tile_permutation():
    t = np.arange(AT_TILE)
    row = (t % MAX_DILATION) * (AT_TILE // MAX_DILATION) + t // MAX_DILATION
    perm = np.zeros((AT_TILE, AT_TILE), np.float32)
    perm[row, t] = 1.0
    return perm


def _attention(aq, ak, av, bias):
    b, seq, w = aq.shape
    assert seq == AT_BLOCK * MAX_DILATION and [d for _, d in WINDOWS] == [1, 4, 16]
    slab = pltpu.VMEM((AT_HEADS // 2, seq, LANES), F32)
    tok = pl.BlockSpec((1, seq, w), lambda i: (i, 0, 0))
    perm = _tile_permutation()
    perm_in = jnp.asarray(perm, BF16)
    perm_out = jnp.asarray(perm.T, BF16)
    return pl.pallas_call(
        _attn_kernel,
        grid=(b,),
        in_specs=[tok, tok, tok, _full_spec(perm_in), _full_spec(perm_out), _full_spec(bias)],
        out_specs=tok,
        out_shape=jax.ShapeDtypeStruct((b, seq, w), BF16),
        scratch_shapes=[slab] * 6,
        compiler_params=pltpu.CompilerParams(dimension_semantics=("parallel",),
                                             vmem_limit_bytes=VMEM_LIMIT),
        name="dilated_attention",
    )(aq, ak, av, perm_in, perm_out, bias)


def _t5_bucket(dist):
    max_exact = N_BUCKETS // 2
    d = np.maximum(dist, 1).astype(np.float32)
    log_bucket = max_exact + (np.log(d / np.float32(max_exact))
                              / np.float32(math.log(MAX_DISTANCE / max_exact))
                              * np.float32(N_BUCKETS - max_exact)).astype(np.int32)
    return np.where(dist < max_exact, dist, np.minimum(log_bucket, N_BUCKETS - 1))


def _bias_tables(rel_bias):
    tabs = []
    for window, dil in WINDOWS:
        n_back = window // dil
        j = _run_order(dil)
        rel = j[:, None] + AT_BLOCK - np.concatenate([j, AT_BLOCK + j])[None, :]
        valid = (rel >= 0) & (rel <= n_back)
        bucket = _t5_bucket(dil * np.clip(rel, 0, n_back)).reshape(-1)
        onehot = (np.arange(N_BUCKETS)[:, None] == bucket[None, :]).astype(np.float32)
        bias = jnp.dot(rel_bias.astype(F32).T, jnp.asarray(onehot, BF16).astype(F32), precision=HIGHEST)
        bias = bias.reshape(AT_HEADS, AT_BLOCK, 2 * AT_BLOCK)
        tabs.append(jnp.where(jnp.asarray(valid)[None], bias, NEG_INF))
    return jnp.stack(tabs)


def _cv_kernel(cu_ref, dw_ref, dwb_ref, g_ref, b_ref, y_ref, ypad):
    seq = cu_ref.shape[1]
    n_tiles = seq // SEQ_TILE
    pad = 4 * SUBLANES
    ypad[0:pad, :] = jnp.zeros((pad, CV_WIDTH), F32)
    for t in range(n_tiles):
        r0 = t * SEQ_TILE
        u = cu_ref[0, r0:r0 + SEQ_TILE, :].astype(F32)
        ypad[pad + r0:pad + r0 + SEQ_TILE, :] = u[:, :CV_WIDTH] * _sigmoid(u[:, CV_WIDTH:])
    for t in range(n_tiles):
        r0 = t * SEQ_TILE
        acc = jnp.zeros((SEQ_TILE, CV_WIDTH), F32) + dwb_ref[...]
        for j in range(CV_KERNEL):
            off = pad + r0 - (CV_KERNEL - 1) + j
            acc = acc + ypad[off:off + SEQ_TILE, :] * dw_ref[j:j + 1, :]
        mu = jnp.mean(acc, axis=-1, keepdims=True)
        cen = acc - mu
        var = jnp.mean(cen * cen, axis=-1, keepdims=True)
        yn = cen * lax.rsqrt(var + EPS) * g_ref[...] + b_ref[...]
        y_ref[0, r0:r0 + SEQ_TILE, :] = _silu(yn).astype(BF16)


def _conformer_conv(cu, dw, dwb, ln_g, ln_b):
    b, seq, w = cu.shape
    return pl.pallas_call(
        _cv_kernel,
        grid=(b,),
        in_specs=[pl.BlockSpec((1, seq, w), lambda i: (i, 0, 0))]
                 + [_full_spec(a) for a in (dw, dwb, ln_g, ln_b)],
        out_specs=pl.BlockSpec((1, seq, CV_WIDTH), lambda i: (i, 0, 0)),
        out_shape=jax.ShapeDtypeStruct((b, seq, CV_WIDTH), BF16),
        scratch_shapes=[pltpu.VMEM((4 * SUBLANES + seq, CV_WIDTH), F32)],
        compiler_params=pltpu.CompilerParams(dimension_semantics=("parallel",),
                                             vmem_limit_bytes=VMEM_LIMIT),
        name="conformer_conv",
    )(cu, dw, dwb, ln_g, ln_b)


def _out_kernel(x_ref, ydn_ref, yat_ref, ycv_ref, wdn_ref, wat_ref, wcv_ref, g_ref, wr_ref, rb_ref,
                tri_ref, xo_ref, h_ref, route_ref, cnt_ref):
    x = (x_ref[...] + _mm(ydn_ref[...], wdn_ref[...]) + _mm(yat_ref[...], wat_ref[...])
         + _mm(ycv_ref[...], wcv_ref[...]))
    xo_ref[...] = x
    ms = jnp.mean(x * x, axis=-1, keepdims=True)
    h = x * lax.rsqrt(ms + EPS) * g_ref[...]
    h_ref[...] = h.astype(BF16)

    logits = _mm(h.astype(BF16), wr_ref[...]) + rb_ref[...]
    lane = lax.broadcasted_iota(jnp.int32, logits.shape, 1)
    is_group = (lane >= N_EXPERTS) & (lane < N_EXPERTS + N_GROUPS)
    gl = jnp.where(is_group, logits, NEG_INF)
    gmax = jnp.max(gl, axis=-1, keepdims=True)
    gsel = jnp.min(jnp.where(gl == gmax, lane, 2 * LANES), axis=-1, keepdims=True) - N_EXPERTS
    p_group = 1.0 / jnp.sum(jnp.where(is_group, jnp.exp(gl - gmax), 0.0), axis=-1, keepdims=True)
    lo = gsel * EXPERTS_PER_GROUP
    in_group = (lane >= lo) & (lane < lo + EXPERTS_PER_GROUP)
    el = jnp.where(in_group, logits, NEG_INF)
    v1 = jnp.max(el, axis=-1, keepdims=True)
    i1 = jnp.min(jnp.where(el == v1, lane, LANES), axis=-1, keepdims=True)
    el2 = jnp.where(lane == i1, NEG_INF, el)
    v2 = jnp.max(el2, axis=-1, keepdims=True)
    i2 = jnp.min(jnp.where(el2 == v2, lane, LANES), axis=-1, keepdims=True)
    t = jnp.exp(v2 - v1)
    g1 = p_group / (1.0 + t)
    g2 = g1 * t

    oh1 = lane == i1
    oh2 = lane == i2
    both = jnp.where(oh1 | oh2, 1.0, 0.0).astype(BF16)
    before = _mm(tri_ref[...], both)
    rank1 = jnp.sum(jnp.where(oh1, before, 0.0), axis=-1, keepdims=True)
    rank2 = jnp.sum(jnp.where(oh2, before, 0.0), axis=-1, keepdims=True)
    cnt_ref[0] = _mm(jnp.ones((SUBLANES, both.shape[0]), BF16), both)

    route = jnp.where(lane == 0, i1.astype(F32), 0.0)
    route = jnp.where(lane == 1, i2.astype(F32), route)
    route = jnp.where(lane == 2, g1, route)
    route = jnp.where(lane == 3, g2, route)
    route = jnp.where(lane == 4, rank1, route)
    route = jnp.where(lane == 5, rank2, route)
    route_ref[...] = route


def _out_proj(x2, ydn, yat, ycv, wdn, wat, wcv, g, wr, rb, tri):
    n, d = x2.shape
    row = lambda w: pl.BlockSpec((ROW_TILE, w), lambda i: (i, 0))
    n_tiles = n // ROW_TILE
    return pl.pallas_call(
        _out_kernel,
        grid=(n_tiles,),
        in_specs=[row(d), row(DN_WIDTH), row(AT_WIDTH), row(CV_WIDTH)]
                 + [_full_spec(a) for a in (wdn, wat, wcv, g, wr, rb, tri)],
        out_specs=[row(d), row(d), row(LANES), pl.BlockSpec((1, SUBLANES, LANES), lambda i: (i, 0, 0))],
        out_shape=[jax.ShapeDtypeStruct((n, d), F32), jax.ShapeDtypeStruct((n, d), BF16),
                   jax.ShapeDtypeStruct((n, LANES), F32),
                   jax.ShapeDtypeStruct((n_tiles, SUBLANES, LANES), F32)],
        compiler_params=pltpu.CompilerParams(dimension_semantics=("parallel",),
                                             vmem_limit_bytes=VMEM_LIMIT),
        name="out_proj",
    )(x2, ydn, yat, ycv, wdn, wat, wcv, g, wr, rb, tri)


def _tile_slots(route, loc_row):
    lane = lax.broadcasted_iota(jnp.int32, route.shape, 1).astype(F32)
    slots = []
    for kk in range(TOP_K):
        base = jnp.sum(jnp.where(lane == route[:, kk:kk + 1], loc_row, 0.0), axis=-1, keepdims=True)
        slots.append(base + route[:, 4 + kk:5 + kk])
    return slots


def _chunk_cols(j):
    return (j * MOE_CHUNK + lax.broadcasted_iota(jnp.int32, (1, MOE_CHUNK), 1)).astype(F32)


def _piece(ref, row):
    return ref.at[pl.ds(pl.multiple_of(row, MOE_PIECE), MOE_PIECE), :]


def _dispatch_kernel(dst_ref, np_ref, zdst_ref, zvalid_ref, nu_ref, route_ref, loc_ref, h_ref, xs_ref,
                     srt, zeros, sem, zsem):
    tile = pl.program_id(0)
    n_pieces = np_ref[tile]
    n_blocks = xs_ref.shape[0] // MOE_BLOCK

    def zero_piece(z):
        return pltpu.make_async_copy(_piece(zeros, 0), _piece(xs_ref, zdst_ref[z]), zsem)

    def zero_block(b):
        return pltpu.make_async_copy(
            zeros, xs_ref.at[pl.ds(pl.multiple_of(b * MOE_BLOCK, MOE_BLOCK), MOE_BLOCK), :], zsem)

    @pl.when(tile == 0)
    def _():
        zeros[...] = jnp.zeros(zeros.shape, BF16)
        for wait in (False, True):
            def piece_body(z, carry, wait=wait):
                @pl.when(zvalid_ref[z] != 0)
                def _():
                    zero_piece(z).wait() if wait else zero_piece(z).start()
                return carry

            def block_body(b, carry, wait=wait):
                zero_block(b).wait() if wait else zero_block(b).start()
                return carry

            lax.fori_loop(0, N_EXPERTS * ZERO_PIECES, piece_body, 0)
            lax.fori_loop(nu_ref[0], n_blocks, block_body, 0)

    slot1, slot2 = _tile_slots(route_ref[...], loc_ref[0, 0:1, :])
    lane = lax.broadcasted_iota(jnp.int32, (1, LANES), 1)
    slot_cols = jnp.where(lane == 0, slot1, jnp.where(lane == 1, slot2, 0.0))
    pick = (lax.broadcasted_iota(jnp.int32, (SUBLANES, LANES), 0)
            == lax.broadcasted_iota(jnp.int32, (SUBLANES, LANES), 1)).astype(BF16)
    slot_rows = sum(_mm_nt(pick, piece) for piece in _split3(slot_cols))
    slot1_row = slot_rows[0:1, :]
    slot2_row = slot_rows[1:2, :]
    h = h_ref[...]
    buf = srt.at[tile % 2]

    def chunk(j, carry):
        row = (j * MOE_CHUNK + lax.broadcasted_iota(jnp.int32, (MOE_CHUNK, 1), 0)).astype(F32)
        onehot = jnp.where(row == slot1_row, 1.0, jnp.where(row == slot2_row, 1.0, 0.0)).astype(BF16)
        buf[pl.ds(pl.multiple_of(j * MOE_CHUNK, MOE_CHUNK), MOE_CHUNK), :] = _mm(onehot, h).astype(BF16)
        return carry

    pieces_per_chunk = MOE_CHUNK // MOE_PIECE
    lax.fori_loop(0, (n_pieces + pieces_per_chunk - 1) // pieces_per_chunk, chunk, 0)

    def piece_copy(t, p):
        return pltpu.make_async_copy(_piece(srt.at[t % 2], p * MOE_PIECE),
                                     _piece(xs_ref, dst_ref[t * PIECES_MAX + p]), sem.at[t % 2])

    def start(p, carry):
        piece_copy(tile, p).start()
        return carry

    lax.fori_loop(0, n_pieces, start, 0)

    def wait_tile(t):
        def wait(p, carry):
            piece_copy(t, p).wait()
            return carry
        lax.fori_loop(0, np_ref[t], wait, 0)

    @pl.when(tile > 0)
    def _():
        wait_tile(tile - 1)

    @pl.when(tile == pl.num_programs(0) - 1)
    def _():
        wait_tile(tile)


def _dispatch(plan, route, h, n_slots):
    n, d = h.shape
    n_tiles = n // MOE_TILE
    grid_spec = pltpu.PrefetchScalarGridSpec(
        num_scalar_prefetch=5,
        grid=(n_tiles,),
        in_specs=[pl.BlockSpec((MOE_TILE, LANES), lambda i, *_: (i, 0)),
                  pl.BlockSpec((1, SUBLANES, LANES), lambda i, *_: (i, 0, 0)),
                  pl.BlockSpec((MOE_TILE, d), lambda i, *_: (i, 0))],
        out_specs=pl.BlockSpec(memory_space=pl.ANY),
        scratch_shapes=[pltpu.VMEM((2, TILE_SLOTS, d), BF16), pltpu.VMEM((MOE_BLOCK, d), BF16),
                        pltpu.SemaphoreType.DMA((2,)), pltpu.SemaphoreType.DMA(())],
    )
    return pl.pallas_call(
        _dispatch_kernel,
        grid_spec=grid_spec,
        out_shape=jax.ShapeDtypeStruct((n_slots, d), BF16),
        compiler_params=pltpu.CompilerParams(dimension_semantics=("arbitrary",),
                                             vmem_limit_bytes=VMEM_LIMIT),
        name="moe_dispatch",
    )(plan["dst"], plan["n_pieces"], plan["zdst"], plan["zvalid"], plan["n_used"], route,
      plan["loc"], h)


def _expert_kernel(be_ref, nu_ref, x_ref, wg_ref, wu_ref, wd_ref, y_ref, wg_bf, wu_bf, wd_bf):
    i = pl.program_id(0)
    changed = jnp.logical_or(i == 0, be_ref[i] != be_ref[jnp.maximum(i - 1, 0)])

    @pl.when(jnp.logical_and(changed, i < nu_ref[0]))
    def _():
        wg_bf[...] = wg_ref[0].astype(BF16)
        wu_bf[...] = wu_ref[0].astype(BF16)
        wd_bf[...] = wd_ref[0].astype(BF16)

    @pl.when(i < nu_ref[0])
    def _():
        x = x_ref[...]
        g = _mm(x, wg_bf[...])
        u = _mm(x, wu_bf[...])
        y_ref[...] = _mm((_silu(g) * u).astype(BF16), wd_bf[...]).astype(BF16)

    @pl.when(i >= nu_ref[0])
    def _():
        y_ref[...] = jnp.zeros(y_ref.shape, BF16)


def _experts(plan, xs, wg, wu, wd, layer):
    ns, d = xs.shape
    de = wg.shape[3]
    rows = lambda i, be, nu: (jnp.minimum(i, nu[0] - 1), 0)
    grid_spec = pltpu.PrefetchScalarGridSpec(
        num_scalar_prefetch=2,
        grid=(ns // MOE_BLOCK,),
        in_specs=[pl.BlockSpec((MOE_BLOCK, d), rows),
                  pl.BlockSpec((None, 1, d, de), lambda i, be, nu: (layer, be[i], 0, 0)),
                  pl.BlockSpec((None, 1, d, de), lambda i, be, nu: (layer, be[i], 0, 0)),
                  pl.BlockSpec((None, 1, de, d), lambda i, be, nu: (layer, be[i], 0, 0))],
        out_specs=pl.BlockSpec((MOE_BLOCK, d), lambda i, be, nu: (i, 0)),
        scratch_shapes=[pltpu.VMEM((d, de), BF16), pltpu.VMEM((d, de), BF16),
                        pltpu.VMEM((de, d), BF16)],
    )
    return pl.pallas_call(
        _expert_kernel,
        grid_spec=grid_spec,
        out_shape=jax.ShapeDtypeStruct((ns, d), BF16),
        compiler_params=pltpu.CompilerParams(dimension_semantics=("arbitrary",),
                                             vmem_limit_bytes=VMEM_LIMIT),
        name="moe_experts",
    )(plan["block_expert"], plan["n_used"], xs, wg, wu, wd)


def _combine_kernel(dst_ref, np_ref, route_ref, loc_ref, x_ref, y_ref, o_ref, ysrt, sem):
    tile = pl.program_id(0)
    n_pieces = np_ref[tile]

    @pl.when(tile == 0)
    def _():
        ysrt[...] = jnp.zeros(ysrt.shape, BF16)

    def piece_copy(t, p):
        return pltpu.make_async_copy(_piece(y_ref, dst_ref[t * PIECES_MAX + p]),
                                     _piece(ysrt.at[t % 2], p * MOE_PIECE), sem.at[t % 2])

    def fetch_tile(t):
        def start(p, carry):
            piece_copy(t, p).start()
            return carry
        lax.fori_loop(0, np_ref[t], start, 0)

    @pl.when(tile == 0)
    def _():
        fetch_tile(tile)

    @pl.when(tile + 1 < pl.num_programs(0))
    def _():
        fetch_tile(tile + 1)

    route = route_ref[...]
    slot1, slot2 = _tile_slots(route, loc_ref[0, 0:1, :])
    g1 = route[:, 2:3]
    g2 = route[:, 3:4]
    gates = jnp.concatenate(
        [jnp.where(_chunk_cols(j) == slot1, g1, jnp.where(_chunk_cols(j) == slot2, g2, 0.0)).astype(BF16)
         for j in range(TILE_SLOTS // MOE_CHUNK)], axis=1)

    def wait(p, carry):
        piece_copy(tile, p).wait()
        return carry

    lax.fori_loop(0, n_pieces, wait, 0)
    o_ref[...] = x_ref[...] + _mm(gates, ysrt[tile % 2])


def _combine(plan, route, x2, y):
    n, d = x2.shape
    grid_spec = pltpu.PrefetchScalarGridSpec(
        num_scalar_prefetch=2,
        grid=(n // MOE_TILE,),
        in_specs=[pl.BlockSpec((MOE_TILE, LANES), lambda i, *_: (i, 0)),
                  pl.BlockSpec((1, SUBLANES, LANES), lambda i, *_: (i, 0, 0)),
                  pl.BlockSpec((MOE_TILE, d), lambda i, *_: (i, 0)),
                  pl.BlockSpec(memory_space=pl.ANY)],
        out_specs=pl.BlockSpec((MOE_TILE, d), lambda i, *_: (i, 0)),
        scratch_shapes=[pltpu.VMEM((2, TILE_SLOTS, d), BF16), pltpu.SemaphoreType.DMA((2,))],
    )
    return pl.pallas_call(
        _combine_kernel,
        grid_spec=grid_spec,
        out_shape=jax.ShapeDtypeStruct((n, d), F32),
        compiler_params=pltpu.CompilerParams(dimension_semantics=("arbitrary",),
                                             vmem_limit_bytes=VMEM_LIMIT),
        name="moe_combine",
    )(plan["dst"], plan["n_pieces"], route, plan["loc"], x2, y)


def _round_up(v, m):
    return (v + m - 1) // m * m


def _moe_plan(counts, n_slots):
    n_tiles = counts.shape[0]
    cnt = counts[:, 0, :N_EXPERTS].astype(jnp.int32)
    seg = _round_up(cnt, MOE_PIECE)
    loc_end = jnp.cumsum(seg, axis=1)
    loc_start = loc_end - seg
    totals = jnp.sum(seg, axis=0)
    padded = _round_up(totals, MOE_BLOCK)
    pad_end = jnp.cumsum(padded)
    pad_start = pad_end - padded
    seg_start = pad_start[None, :] + jnp.cumsum(seg, axis=0) - seg
    piece_off = jnp.arange(PIECES_MAX, dtype=jnp.int32) * MOE_PIECE
    piece_e = jnp.sum(loc_end[:, None, :] <= piece_off[None, :, None], axis=2)
    hit = piece_e[..., None] == jnp.arange(N_EXPERTS)
    shift = jnp.sum(jnp.where(hit, (seg_start - loc_start)[:, None, :], 0), axis=2)
    valid = piece_off[None, :] < loc_end[:, -1:]
    dst = jnp.where(valid, shift + piece_off[None, :], 0).astype(jnp.int32).reshape(-1)
    zk = jnp.arange(ZERO_PIECES, dtype=jnp.int32)[None, :] * MOE_PIECE
    zvalid = zk < (padded - totals)[:, None]
    zdst = jnp.where(zvalid, (pad_start + totals)[:, None] + zk, 0)
    blk_start = jnp.arange(n_slots // MOE_BLOCK, dtype=jnp.int32) * MOE_BLOCK
    block_expert = jnp.minimum(jnp.sum(pad_end[None, :] <= blk_start[:, None], axis=1), N_EXPERTS - 1)
    loc = jnp.pad(loc_start.astype(F32), ((0, 0), (0, LANES - N_EXPERTS)))
    return {
        "dst": dst,
        "n_pieces": (loc_end[:, -1] // MOE_PIECE).astype(jnp.int32),
        "zdst": zdst.astype(jnp.int32).reshape(-1),
        "zvalid": zvalid.astype(jnp.int32).reshape(-1),
        "loc": jnp.broadcast_to(loc[:, None, :], (n_tiles, SUBLANES, LANES)),
        "block_expert": block_expert.astype(jnp.int32),
        "n_used": (pad_end[-1:] // MOE_BLOCK).astype(jnp.int32),
    }


def _pad_lanes(a, width=LANES):
    return jnp.pad(a, [(0, 0)] * (a.ndim - 1) + [(0, width - a.shape[-1])])


def kernel(x, norm_mix, w_in, dn_conv, dn_a_log, dn_dt_bias, dn_out_norm, at_q_norm, at_k_norm,
           rel_bias, cv_dw, cv_dw_bias, cv_ln_g, cv_ln_b, w_out, norm_ffn, router_group_w,
           router_group_b, router_expert_w, router_expert_b, ex_gate, ex_up, ex_down):
    bsz, seq, d = x.shape
    n_tok = bsz * seq
    depth = w_in.shape[0]
    c_ab = 4 * DN_WIDTH
    c_at = c_ab + 2 * DN_HEADS
    c_cv = c_at + 3 * AT_WIDTH
    bd_at = _block_diag_ones(AT_WIDTH, HEAD_DIM, BF16)
    bd_dn = _block_diag_ones(DN_WIDTH, HEAD_DIM, BF16)
    bias = _bias_tables(rel_bias)
    tri = (jnp.arange(ROW_TILE)[:, None] > jnp.arange(ROW_TILE)[None, :]).astype(BF16)
    per_head_lanes = lambda v: jnp.repeat(v, HEAD_DIM)[None, :]
    n_tiles = n_tok // MOE_TILE
    n_slots = _round_up(TOP_K * n_tok + n_tiles * N_EXPERTS * (MOE_PIECE - 1)
                        + N_EXPERTS * (MOE_BLOCK - 1), MOE_BLOCK)

    x2 = x.reshape(n_tok, d)
    for layer in range(depth):
        w_l = w_in[layer]
        dn, aq, ak, av, cu, ab = _proj(
            x2, norm_mix[layer][None, :],
            w_l[:, :c_ab].astype(BF16), w_l[:, c_at:c_cv].astype(BF16), w_l[:, c_cv:].astype(BF16),
            _pad_lanes(w_l[:, c_ab:c_at]).astype(BF16), bd_at,
            jnp.tile(at_q_norm[layer], AT_HEADS)[None, :] * (HEAD_DIM ** -0.5),
            jnp.tile(at_k_norm[layer], AT_HEADS)[None, :])

        y_dn = _deltanet(dn.reshape(bsz, seq, -1), ab.reshape(bsz, seq, LANES), dn_conv[layer],
                         per_head_lanes(dn_a_log[layer]), per_head_lanes(dn_dt_bias[layer]),
                         jnp.tile(dn_out_norm[layer], DN_HEADS)[None, :], bd_dn)
        y_at = _attention(aq.reshape(bsz, seq, -1), ak.reshape(bsz, seq, -1),
                          av.reshape(bsz, seq, -1), bias)
        y_cv = _conformer_conv(cu.reshape(bsz, seq, -1), cv_dw[layer], cv_dw_bias[layer][None, :],
                               cv_ln_g[layer][None, :], cv_ln_b[layer][None, :])

        wo = w_out[layer].astype(BF16)
        w_r = _pad_lanes(jnp.concatenate([router_expert_w[layer], router_group_w[layer]], axis=1))
        b_r = _pad_lanes(jnp.concatenate([router_expert_b[layer], router_group_b[layer]])[None, :])
        x_mid, h_ffn, route, counts = _out_proj(
            x2, y_dn.reshape(n_tok, DN_WIDTH), y_at.reshape(n_tok, AT_WIDTH),
            y_cv.reshape(n_tok, CV_WIDTH), wo[:DN_WIDTH], wo[DN_WIDTH:DN_WIDTH + AT_WIDTH],
            wo[DN_WIDTH + AT_WIDTH:], norm_ffn[layer][None, :], w_r.astype(BF16), b_r, tri)

        plan = _moe_plan(counts, n_slots)
        xs = _dispatch(plan, route, h_ffn, n_slots)
        y = _experts(plan, xs, ex_gate, ex_up, ex_down, layer)
        x2 = _combine(plan, route, x_mid, y)
    return x2.reshape(bsz, seq, d)
```

```python
import math

import jax
import jax.numpy as jnp
import numpy as np
from jax import lax
from jax.experimental import pallas as pl
from jax.experimental.pallas import tpu as pltpu

F32 = jnp.float32
BF16 = jnp.bfloat16
HIGHEST = lax.Precision.HIGHEST

EPS = 1e-6
NEG_INF = -1e30

HEAD_DIM = 64
DN_HEADS = 4
DN_WIDTH = DN_HEADS * HEAD_DIM
DN_CONV = 4
DN_CHUNK = 64
AT_HEADS = 8
AT_WIDTH = AT_HEADS * HEAD_DIM
AT_BLOCK = 128
WINDOWS = ((128, 1), (512, 4), (2048, 16))
MAX_DILATION = 16
AT_TILE = 256
AT_LOCKSTEP = 4
N_BUCKETS = 32
MAX_DISTANCE = 2048
CV_WIDTH = 256
CV_KERNEL = 31
N_GROUPS = 4
EXPERTS_PER_GROUP = 8
N_EXPERTS = N_GROUPS * EXPERTS_PER_GROUP
TOP_K = 2

LANES = 128
SUBLANES = 8
VMEM_LIMIT = 52 * 1024 * 1024

ROW_TILE = 512
SEQ_TILE = 256
CV_TILE = 64
OUT_SLAB = 256
DN_TILE = 64
DN_PREP_CHUNKS = 4
MOE_BLOCK = 512
MOE_TILE = ROW_TILE
MOE_PIECE = 16
MOE_CHUNK = 256
TILE_SLOTS = -(-(TOP_K * MOE_TILE + N_EXPERTS * (MOE_PIECE - 1)) // MOE_CHUNK) * MOE_CHUNK
PIECES_MAX = TILE_SLOTS // MOE_PIECE
ZERO_PIECES = MOE_BLOCK // MOE_PIECE - 1


def _mm(a, b, precision=None):
    return jnp.dot(a, b, preferred_element_type=F32, precision=precision)


def _mm_nt(a, b):
    return lax.dot_general(a, b, (((1,), (1,)), ((), ())), preferred_element_type=F32)


def _mm_tn(a, b):
    return lax.dot_general(a, b, (((0,), (0,)), ((), ())), preferred_element_type=F32)


def _sigmoid(x):
    return 0.5 * jnp.tanh(0.5 * x) + 0.5


def _silu(x):
    return x * _sigmoid(x)


def _split3(x):
    p0 = x.astype(BF16)
    r1 = x - p0.astype(F32)
    p1 = r1.astype(BF16)
    p2 = (r1 - p1.astype(F32)).astype(BF16)
    return p0, p1, p2


def _alternate(*stages):
    live = list(stages)
    while live:
        for gen in list(live):
            if next(gen, live) is live:
                live.remove(gen)


def _full_spec(a):
    nd = a.ndim
    return pl.BlockSpec(a.shape, lambda *_: (0,) * nd)


def _block_diag_ones(width, block, dtype):
    r = jnp.arange(width)[:, None] // block
    c = jnp.arange(width)[None, :] // block
    return (r == c).astype(dtype)


def _proj_kernel(x_ref, g_ref, wdn_ref, wat_ref, wcv_ref, wab_ref, bd_ref, qn_ref, kn_ref,
                 dn_ref, aq_ref, ak_ref, av_ref, cv_ref, ab_ref):
    x = x_ref[...]
    ms = jnp.mean(x * x, axis=-1, keepdims=True)
    h = (x * lax.rsqrt(ms + EPS) * g_ref[...]).astype(BF16)
    dn_ref[...] = _mm(h, wdn_ref[...]).astype(BF16)
    cv_ref[...] = _mm(h, wcv_ref[...]).astype(BF16)
    ab_ref[...] = _mm(h, wab_ref[...])
    at = _mm(h, wat_ref[...])
    q = at[:, 0:AT_WIDTH]
    k = at[:, AT_WIDTH:2 * AT_WIDTH]
    bd = bd_ref[...]
    qms = _mm((q * q).astype(BF16), bd) * (1.0 / HEAD_DIM)
    kms = _mm((k * k).astype(BF16), bd) * (1.0 / HEAD_DIM)
    aq_ref[...] = (q * lax.rsqrt(qms + EPS) * qn_ref[...]).astype(BF16)
    ak_ref[...] = (k * lax.rsqrt(kms + EPS) * kn_ref[...]).astype(BF16)
    av_ref[...] = at[:, 2 * AT_WIDTH:3 * AT_WIDTH].astype(BF16)


def _proj(x2, g, wdn, wat, wcv, wab, bd, qn, kn):
    n, d = x2.shape
    row = lambda w: pl.BlockSpec((ROW_TILE, w), lambda i: (i, 0))
    widths = (wdn.shape[1], AT_WIDTH, AT_WIDTH, AT_WIDTH, wcv.shape[1], LANES)
    dtypes = (BF16, BF16, BF16, BF16, BF16, F32)
    return pl.pallas_call(
        _proj_kernel,
        grid=(n // ROW_TILE,),
        in_specs=[row(d)] + [_full_spec(a) for a in (g, wdn, wat, wcv, wab, bd, qn, kn)],
        out_specs=[row(w) for w in widths],
        out_shape=[jax.ShapeDtypeStruct((n, w), t) for w, t in zip(widths, dtypes)],
        compiler_params=pltpu.CompilerParams(dimension_semantics=("parallel",),
                                             vmem_limit_bytes=VMEM_LIMIT),
        name="proj",
    )(x2, g, wdn, wat, wcv, wab, bd, qn, kn)


def _per_head(x, block_mask):
    return jnp.where(block_mask, jnp.concatenate([x] * DN_HEADS, axis=0), jnp.zeros((), x.dtype))


def _dn_kernel(dn_ref, ab_ref, cw_ref, alog_ref, dtb_ref, onorm_ref, bd_ref, y_ref,
               xpad, qs, ks, vs, gs, bs, os_, st, *group_bufs):
    set_a, set_b = group_bufs[:6], group_bufs[6:]
    seq = dn_ref.shape[1]
    n_tiles = seq // DN_TILE
    cw3 = 3 * DN_WIDTH
    pad = SUBLANES
    c = DN_CHUNK
    bd = bd_ref[...]
    hid = lax.broadcasted_iota(jnp.int32, (1, DN_WIDTH), 1) // HEAD_DIM

    def expand(cols, first):
        out = cols[:, first + DN_HEADS - 1:first + DN_HEADS]
        for h in range(DN_HEADS - 2, -1, -1):
            out = jnp.where(hid == h, cols[:, first + h:first + h + 1], out)
        return out

    xpad[0:pad, :] = jnp.zeros((pad, cw3), F32)
    for t in range(n_tiles):
        r0 = t * DN_TILE
        xpad[pad + r0:pad + r0 + DN_TILE, :] = dn_ref[0, r0:r0 + DN_TILE, 0:cw3].astype(F32)
    for t in range(n_tiles):
        r0 = t * DN_TILE
        ab = ab_ref[0, r0:r0 + DN_TILE, :]
        sp_in = expand(ab, 0) + dtb_ref[...]
        softplus = jnp.maximum(sp_in, 0.0) + jnp.log(1.0 + jnp.exp(-jnp.abs(sp_in)))
        gs[r0:r0 + DN_TILE, :] = -jnp.exp(alog_ref[...]) * softplus
        bs[r0:r0 + DN_TILE, :] = _sigmoid(expand(ab, DN_HEADS))
        acc = jnp.zeros((DN_TILE, cw3), F32)
        for j in range(DN_CONV):
            off = pad + r0 - (DN_CONV - 1) + j
            acc = acc + xpad[off:off + DN_TILE, :] * cw_ref[j:j + 1, :]
        y = _silu(acc)
        q = y[:, 0:DN_WIDTH]
        k = y[:, DN_WIDTH:2 * DN_WIDTH]
        qss = _mm((q * q).astype(BF16), bd)
        kss = _mm((k * k).astype(BF16), bd)
        qs[r0:r0 + DN_TILE, :] = q * lax.rsqrt(qss + EPS) * (HEAD_DIM ** -0.5)
        ks[r0:r0 + DN_TILE, :] = k * lax.rsqrt(kss + EPS)
        vs[r0:r0 + DN_TILE, :] = y[:, 2 * DN_WIDTH:3 * DN_WIDTH]

    ri = lax.broadcasted_iota(jnp.int32, (c, DN_WIDTH), 0)
    ci = lax.broadcasted_iota(jnp.int32, (c, DN_WIDTH), 1) % HEAD_DIM
    causal = ri >= ci
    strict = ri > ci
    eye_cat = (ri == ci).astype(F32)
    r2 = lax.broadcasted_iota(jnp.int32, (c, c), 0)
    c2 = lax.broadcasted_iota(jnp.int32, (c, c), 1)
    lower_ones = (r2 >= c2).astype(BF16)
    all_ones = jnp.ones((c, c), BF16)
    block_mask = (lax.broadcasted_iota(jnp.int32, (DN_WIDTH, DN_WIDTH), 0) // HEAD_DIM
                  == lax.broadcasted_iota(jnp.int32, (DN_WIDTH, DN_WIDTH), 1) // HEAD_DIM)

    def mm_exact_rhs(lhs_bf, x):
        p0, p1, p2 = _split3(x)
        return _mm(lhs_bf, p0) + _mm(lhs_bf, p1) + _mm(lhs_bf, p2)

    def mm_bd(lhs, rhs_cat):
        return _mm(lhs.astype(BF16), _per_head(rhs_cat.astype(BF16), block_mask))

    def prep(m, dst):
        ws, us, qks, qds, kds, gls = dst
        rows = [pl.multiple_of((DN_PREP_CHUNKS * m + cc) * c, c) for cc in range(DN_PREP_CHUNKS)]
        g_cum = [mm_exact_rhs(lower_ones, gs[pl.ds(r, c), :]) for r in rows]
        yield
        g_row = [mm_exact_rhs(all_ones, g * eye_cat) for g in g_cum]
        yield
        decay = [jnp.exp(jnp.where(causal, g - gr, NEG_INF)) for g, gr in zip(g_cum, g_row)]
        kc = [ks[pl.ds(r, c), :] for r in rows]
        qc = [qs[pl.ds(r, c), :] for r in rows]
        beta = [bs[pl.ds(r, c), :] for r in rows]
        kb = [k * b for k, b in zip(kc, beta)]
        aq = [_mm_nt(jnp.concatenate([b_, q_], axis=0).astype(BF16),
                     _per_head(k_.astype(BF16), block_mask))
              for b_, q_, k_ in zip(kb, qc, kc)]
        a_mat = [jnp.where(strict, x[:c] * d, 0.0) for x, d in zip(aq, decay)]
        qk = [x[c:] * d for x, d in zip(aq, decay)]
        p = [eye_cat - a for a in a_mat]
        yield
        pw = [mm_bd(a, a) for a in a_mat]
        for _ in range(4):
            yield
            both = [mm_bd(jnp.concatenate([p_, x], axis=0), x) for p_, x in zip(p, pw)]
            p = [p_ + b_[:c] for p_, b_ in zip(p, both)]
            pw = [b_[c:] for b_ in both]
        yield
        p = [p_ + mm_bd(p_, x) for p_, x in zip(p, pw)]
        yield
        for i, r in enumerate(rows):
            e_g = jnp.exp(g_cum[i])
            sl = slice(i * c, (i + 1) * c)
            ws[sl, :] = mm_bd(p[i], kb[i] * e_g).astype(BF16)
            us[sl, :] = mm_bd(p[i], vs[pl.ds(r, c), :] * beta[i])
            qks[sl, :] = qk[i].astype(BF16)
            qds[sl, :] = (qc[i] * e_g).astype(BF16)
            g_last = g_cum[i][c - 1:c, :]
            kds[sl, :] = (kc[i] * jnp.exp(g_last - g_cum[i])).astype(BF16)
            gls[i * SUBLANES:(i + 1) * SUBLANES, :] = jnp.broadcast_to(jnp.exp(g_last),
                                                                       (SUBLANES, DN_WIDTH))

    def scan_group(m, src):
        ws, us, qks, qds, kds, gls = src
        for i in range(DN_PREP_CHUNKS):
            sl = slice(i * c, (i + 1) * c)
            r = pl.multiple_of((DN_PREP_CHUNKS * m + i) * c, c)
            state = st[...]
            both = _mm(jnp.concatenate([ws[sl, :], qds[sl, :]], axis=0),
                       _per_head(state.astype(BF16), block_mask))
            yield
            v_new = (us[sl, :] - both[:c]).astype(BF16)
            os_[pl.ds(r, c), :] = both[c:] + _mm(qks[sl, :], _per_head(v_new, block_mask))
            kv = _mm_tn(kds[sl, :], v_new)
            upd = kv[(DN_HEADS - 1) * c:DN_HEADS * c, :]
            for h in range(DN_HEADS - 2, -1, -1):
                upd = jnp.where(hid == h, kv[h * c:(h + 1) * c, :], upd)
            yield
            st[...] = state * gls[i * SUBLANES:i * SUBLANES + 1, :] + upd

    alternate = _alternate

    st[...] = jnp.zeros(st.shape, F32)
    n_groups = seq // (DN_PREP_CHUNKS * c)
    assert n_groups % 2 == 0
    alternate(prep(0, set_a))

    def pair(j, carry):
        alternate(prep(2 * j + 1, set_b), scan_group(2 * j, set_a))
        alternate(prep(2 * j + 2, set_a), scan_group(2 * j + 1, set_b))
        return carry

    lax.fori_loop(0, n_groups // 2 - 1, pair, 0)
    alternate(prep(n_groups - 1, set_b), scan_group(n_groups - 2, set_a))
    alternate(scan_group(n_groups - 1, set_b))

    for t in range(n_tiles):
        r0 = t * DN_TILE
        o = os_[r0:r0 + DN_TILE, :]
        z = dn_ref[0, r0:r0 + DN_TILE, cw3:cw3 + DN_WIDTH].astype(F32)
        ms = _mm((o * o).astype(BF16), bd) * (1.0 / HEAD_DIM)
        y_ref[0, r0:r0 + DN_TILE, :] = (o * lax.rsqrt(ms + EPS) * onorm_ref[...] * _silu(z)).astype(BF16)


def _deltanet(dn, ab, conv_w, alog_cat, dtb_cat, onorm_cat, bd):
    b, seq, w = dn.shape
    f32buf = pltpu.VMEM((seq, DN_WIDTH), F32)
    group_rows = DN_PREP_CHUNKS * DN_CHUNK
    group_set = [pltpu.VMEM((group_rows, DN_WIDTH), BF16),
                 pltpu.VMEM((group_rows, DN_WIDTH), F32),
                 pltpu.VMEM((group_rows, DN_WIDTH), BF16),
                 pltpu.VMEM((group_rows, DN_WIDTH), BF16),
                 pltpu.VMEM((group_rows, DN_WIDTH), BF16),
                 pltpu.VMEM((DN_PREP_CHUNKS * SUBLANES, DN_WIDTH), F32)]
    return pl.pallas_call(
        _dn_kernel,
        grid=(b,),
        in_specs=[pl.BlockSpec((1, seq, w), lambda i: (i, 0, 0)),
                  pl.BlockSpec((1, seq, LANES), lambda i: (i, 0, 0))]
                 + [_full_spec(a) for a in (conv_w, alog_cat, dtb_cat, onorm_cat, bd)],
        out_specs=pl.BlockSpec((1, seq, DN_WIDTH), lambda i: (i, 0, 0)),
        out_shape=jax.ShapeDtypeStruct((b, seq, DN_WIDTH), BF16),
        scratch_shapes=[pltpu.VMEM((SUBLANES + seq, 3 * DN_WIDTH), F32),
                        f32buf, f32buf, f32buf, f32buf, f32buf,
                        f32buf,
                        pltpu.VMEM((DN_CHUNK, DN_WIDTH), F32)]
                       + group_set + group_set,
        compiler_params=pltpu.CompilerParams(dimension_semantics=("parallel",),
                                             vmem_limit_bytes=VMEM_LIMIT),
        name="deltanet",
    )(dn, ab, conv_w, alog_cat, dtb_cat, onorm_cat, bd)


def _run_starts(dil, r, i):
    if dil == 16:
        return [(tt * AT_TILE + r * 16, 16) for tt in range(AT_BLOCK * dil // AT_TILE)]
    if dil == 4:
        return [((2 * i + th) * AT_TILE + (4 * s + r) * 16, 16) for th in range(2) for s in range(4)]
    assert dil == 1
    return [((i // 2) * AT_TILE + rr * 16 + 8 * (i % 2), 8) for rr in range(16)]


def _run_order(dil):
    if dil == 16:
        return np.arange(AT_BLOCK)
    if dil == 4:
        th, s, ml = np.meshgrid(np.arange(2), np.arange(4), np.arange(16), indexing="ij")
        return (64 * th + 4 * ml + s).reshape(-1)
    rr, m8 = np.meshgrid(np.arange(16), np.arange(8), indexing="ij")
    return (16 * m8 + rr).reshape(-1)


def _load_runs(ref, p, runs):
    return jnp.concatenate([ref[p, pl.ds(pl.multiple_of(s, n), n), :] for s, n in runs], axis=0)


def _store_runs(ref, p, runs, val):
    off = 0
    for s, n in runs:
        ref[p, pl.ds(pl.multiple_of(s, n), n), :] = val[off:off + n]
        off += n


def _attn_kernel(q_ref, k_ref, v_ref, perm_ref, perm_t_ref, bias_ref, y_ref, qf, kf, vf, acc, ms, ls):
    seq = q_ref.shape[1]
    pairs = AT_HEADS // 2
    perm = perm_ref[...]
    for t in range(seq // AT_TILE):
        r0 = t * AT_TILE
        for src, dst in ((q_ref, qf), (k_ref, kf), (v_ref, vf)):
            rows = _mm(perm, src[0, r0:r0 + AT_TILE, :])
            for p in range(pairs):
                dst[p, r0:r0 + AT_TILE, :] = rows[:, p * LANES:(p + 1) * LANES]

    lane = lax.broadcasted_iota(jnp.int32, (1, LANES), 1)
    key_col = lax.broadcasted_iota(jnp.int32, (1, 2 * AT_BLOCK), 1)
    order = sorted(range(len(WINDOWS)), key=lambda g: -WINDOWS[g][1])
    for step, grp in enumerate(order):
        dil = WINDOWS[grp][1]
        nb = seq // dil // AT_BLOCK
        is_first = step == 0
        is_last = step == len(order) - 1

        def unit(u, carry, dil=dil, nb=nb, grp=grp, is_first=is_first, is_last=is_last):
            r = u // nb
            i = u % nb
            q_runs = _run_starts(dil, r, i)
            p_runs = _run_starts(dil, r, jnp.maximum(i - 1, 0))
            no_prev = jnp.where(jnp.logical_and(i == 0, key_col < AT_BLOCK), NEG_INF, 0.0)
            def head_pair(p):
                q2 = _load_runs(qf, p, q_runs).astype(BF16)
                k2 = jnp.concatenate([_load_runs(kf, p, p_runs), _load_runs(kf, p, q_runs)],
                                     axis=0).astype(BF16)
                v2 = jnp.concatenate([_load_runs(vf, p, p_runs), _load_runs(vf, p, q_runs)],
                                     axis=0).astype(BF16)
                v_ext = jnp.concatenate([v2, jnp.ones(v2.shape, BF16)], axis=1)
                masks = [(lane // HEAD_DIM) == hh for hh in range(2)]
                scores = [_mm_nt(jnp.where(mask, q2, jnp.zeros_like(q2)), k2)
                          + bias_ref[grp, 2 * p + hh] + no_prev for hh, mask in enumerate(masks)]
                yield
                maxes = [jnp.max(s, axis=-1, keepdims=True) for s in scores]
                probs = [jnp.exp(s - m_h).astype(BF16) for s, m_h in zip(scores, maxes)]
                yield
                results = [_mm(pexp, v_ext) for pexp in probs]
                yield
                m_new = jnp.where(masks[1], maxes[1], jnp.broadcast_to(maxes[0], (AT_BLOCK, LANES)))
                o_new = jnp.where(masks[1], results[1][:, :LANES], results[0][:, :LANES])
                l_new = jnp.where(masks[1], results[1][:, LANES:], results[0][:, LANES:])
                if not is_first:
                    m_old = _load_runs(ms, p, q_runs)
                    m_tot = jnp.maximum(m_old, m_new)
                    a_old = jnp.exp(m_old - m_tot)
                    a_new = jnp.exp(m_new - m_tot)
                    l_new = a_old * _load_runs(ls, p, q_runs) + a_new * l_new
                    o_new = a_old * _load_runs(acc, p, q_runs) + a_new * o_new
                    m_new = m_tot
                if is_last:
                    _store_runs(acc, p, q_runs, o_new / l_new)
                else:
                    _store_runs(ms, p, q_runs, m_new)
                    _store_runs(ls, p, q_runs, l_new)
                    _store_runs(acc, p, q_runs, o_new)

            for p0 in range(0, pairs, AT_LOCKSTEP):
                _alternate(*[head_pair(p) for p in range(p0, p0 + AT_LOCKSTEP)])
            return carry

        lax.fori_loop(0, seq // AT_BLOCK, unit, 0)

    perm_t = perm_t_ref[...]
    for t in range(seq // AT_TILE):
        r0 = t * AT_TILE
        for p in range(pairs):
            y_ref[0, r0:r0 + AT_TILE, p * LANES:(p + 1) * LANES] = _mm(
                perm_t, acc[p, r0:r0 + AT_TILE, :].astype(BF16)).astype(BF16)


def _tile_permutation():
    t = np.arange(AT_TILE)
    row = (t % MAX_DILATION) * (AT_TILE // MAX_DILATION) + t // MAX_DILATION
    perm = np.zeros((AT_TILE, AT_TILE), np.float32)
    perm[row, t] = 1.0
    return perm


def _attention(aq, ak, av, bias):
    b, seq, w = aq.shape
    assert seq == AT_BLOCK * MAX_DILATION and [d for _, d in WINDOWS] == [1, 4, 16]
    slab = pltpu.VMEM((AT_HEADS // 2, seq, LANES), F32)
    tok = pl.BlockSpec((1, seq, w), lambda i: (i, 0, 0))
    perm = _tile_permutation()
    perm_in = jnp.asarray(perm, BF16)
    perm_out = jnp.asarray(perm.T, BF16)
    return pl.pallas_call(
        _attn_kernel,
        grid=(b,),
        in_specs=[tok, tok, tok, _full_spec(perm_in), _full_spec(perm_out), _full_spec(bias)],
        out_specs=tok,
        out_shape=jax.ShapeDtypeStruct((b, seq, w), BF16),
        scratch_shapes=[slab] * 6,
        compiler_params=pltpu.CompilerParams(dimension_semantics=("parallel",),
                                             vmem_limit_bytes=VMEM_LIMIT),
        name="dilated_attention",
    )(aq, ak, av, perm_in, perm_out, bias)


def _t5_bucket(dist):
    max_exact = N_BUCKETS // 2
    d = np.maximum(dist, 1).astype(np.float32)
    log_bucket = max_exact + (np.log(d / np.float32(max_exact))
                              / np.float32(math.log(MAX_DISTANCE / max_exact))
                              * np.float32(N_BUCKETS - max_exact)).astype(np.int32)
    return np.where(dist < max_exact, dist, np.minimum(log_bucket, N_BUCKETS - 1))


def _bias_tables(rel_bias):
    tabs = []
    for window, dil in WINDOWS:
        n_back = window // dil
        j = _run_order(dil)
        rel = j[:, None] + AT_BLOCK - np.concatenate([j, AT_BLOCK + j])[None, :]
        valid = (rel >= 0) & (rel <= n_back)
        bucket = _t5_bucket(dil * np.clip(rel, 0, n_back)).reshape(-1)
        onehot = (np.arange(N_BUCKETS)[:, None] == bucket[None, :]).astype(np.float32)
        bias = jnp.dot(rel_bias.astype(F32).T, jnp.asarray(onehot, BF16).astype(F32), precision=HIGHEST)
        bias = bias.reshape(AT_HEADS, AT_BLOCK, 2 * AT_BLOCK)
        tabs.append(jnp.where(jnp.asarray(valid)[None], bias, NEG_INF))
    return jnp.stack(tabs)


def _cv_kernel(cu_ref, dw_ref, dwb_ref, g_ref, b_ref, y_ref, ypad):
    seq = cu_ref.shape[1]
    n_tiles = seq // SEQ_TILE
    pad = 4 * SUBLANES
    ypad[0:pad, :] = jnp.zeros((pad, CV_WIDTH), F32)
    for t in range(n_tiles):
        r0 = t * SEQ_TILE
        u = cu_ref[0, r0:r0 + SEQ_TILE, :].astype(F32)
        ypad[pad + r0:pad + r0 + SEQ_TILE, :] = u[:, :CV_WIDTH] * _sigmoid(u[:, CV_WIDTH:])
    for t in range(seq // CV_TILE):
        r0 = t * CV_TILE
        window = ypad[r0:r0 + pad + CV_TILE, :]
        acc = jnp.zeros((CV_TILE, CV_WIDTH), F32) + dwb_ref[...]
        for b in range(SUBLANES):
            rolled = pltpu.roll(window, b, axis=0) if b else window
            for j in range(CV_KERNEL):
                off = pad - (CV_KERNEL - 1) + j
                if (-off) % SUBLANES == b:
                    a8 = off + b
                    acc = acc + rolled[a8:a8 + CV_TILE, :] * dw_ref[j:j + 1, :]
        mu = jnp.mean(acc, axis=-1, keepdims=True)
        cen = acc - mu
        var = jnp.mean(cen * cen, axis=-1, keepdims=True)
        yn = cen * lax.rsqrt(var + EPS) * g_ref[...] + b_ref[...]
        y_ref[0, r0:r0 + CV_TILE, :] = _silu(yn).astype(BF16)


def _conformer_conv(cu, dw, dwb, ln_g, ln_b):
    b, seq, w = cu.shape
    return pl.pallas_call(
        _cv_kernel,
        grid=(b,),
        in_specs=[pl.BlockSpec((1, seq, w), lambda i: (i, 0, 0))]
                 + [_full_spec(a) for a in (dw, dwb, ln_g, ln_b)],
        out_specs=pl.BlockSpec((1, seq, CV_WIDTH), lambda i: (i, 0, 0)),
        out_shape=jax.ShapeDtypeStruct((b, seq, CV_WIDTH), BF16),
        scratch_shapes=[pltpu.VMEM((4 * SUBLANES + seq, CV_WIDTH), F32)],
        compiler_params=pltpu.CompilerParams(dimension_semantics=("parallel",),
                                             vmem_limit_bytes=VMEM_LIMIT),
        name="conformer_conv",
    )(cu, dw, dwb, ln_g, ln_b)


def _out_kernel(x_ref, ydn_ref, yat_ref, ycv_ref, wdn_ref, wat_ref, wcv_ref, g_ref, wr_ref, rb_ref,
                tri_ref, xo_ref, h_ref, route_ref, cnt_ref):
    lane = lax.broadcasted_iota(jnp.int32, (OUT_SLAB, LANES), 1)
    picks = {}

    def row_slab(r0):
        sl = slice(r0, r0 + OUT_SLAB)
        x = (x_ref[sl, :] + _mm(ydn_ref[sl, :], wdn_ref[...]) + _mm(yat_ref[sl, :], wat_ref[...])
             + _mm(ycv_ref[sl, :], wcv_ref[...]))
        xo_ref[sl, :] = x
        yield
        ms = jnp.mean(x * x, axis=-1, keepdims=True)
        h = (x * lax.rsqrt(ms + EPS) * g_ref[...]).astype(BF16)
        h_ref[sl, :] = h
        logits = _mm(h, wr_ref[...]) + rb_ref[...]
        yield
        is_group = (lane >= N_EXPERTS) & (lane < N_EXPERTS + N_GROUPS)
        gl = jnp.where(is_group, logits, NEG_INF)
        gmax = jnp.max(gl, axis=-1, keepdims=True)
        gsel = jnp.min(jnp.where(gl == gmax, lane, 2 * LANES), axis=-1, keepdims=True) - N_EXPERTS
        p_group = 1.0 / jnp.sum(jnp.where(is_group, jnp.exp(gl - gmax), 0.0), axis=-1, keepdims=True)
        yield
        lo = gsel * EXPERTS_PER_GROUP
        in_group = (lane >= lo) & (lane < lo + EXPERTS_PER_GROUP)
        el = jnp.where(in_group, logits, NEG_INF)
        v1 = jnp.max(el, axis=-1, keepdims=True)
        i1 = jnp.min(jnp.where(el == v1, lane, LANES), axis=-1, keepdims=True)
        yield
        el2 = jnp.where(lane == i1, NEG_INF, el)
        v2 = jnp.max(el2, axis=-1, keepdims=True)
        i2 = jnp.min(jnp.where(el2 == v2, lane, LANES), axis=-1, keepdims=True)
        t = jnp.exp(v2 - v1)
        g1 = p_group / (1.0 + t)
        picks[r0] = (i1, i2, g1, g1 * t)

    slabs = list(range(0, ROW_TILE, OUT_SLAB))
    _alternate(*[row_slab(r0) for r0 in slabs])

    both = jnp.concatenate(
        [jnp.where((lane == picks[r0][0]) | (lane == picks[r0][1]), 1.0, 0.0).astype(BF16)
         for r0 in slabs], axis=0)
    before = _mm(tri_ref[...], both)
    cnt_ref[0] = _mm(jnp.ones((SUBLANES, ROW_TILE), BF16), both)
    for r0 in slabs:
        i1, i2, g1, g2 = picks[r0]
        seen = before[r0:r0 + OUT_SLAB, :]
        rank1 = jnp.sum(jnp.where(lane == i1, seen, 0.0), axis=-1, keepdims=True)
        rank2 = jnp.sum(jnp.where(lane == i2, seen, 0.0), axis=-1, keepdims=True)
        route = jnp.where(lane == 0, i1.astype(F32), 0.0)
        route = jnp.where(lane == 1, i2.astype(F32), route)
        route = jnp.where(lane == 2, g1, route)
        route = jnp.where(lane == 3, g2, route)
        route = jnp.where(lane == 4, rank1, route)
        route = jnp.where(lane == 5, rank2, route)
        route_ref[r0:r0 + OUT_SLAB, :] = route


def _out_proj(x2, ydn, yat, ycv, wdn, wat, wcv, g, wr, rb, tri):
    n, d = x2.shape
    row = lambda w: pl.BlockSpec((ROW_TILE, w), lambda i: (i, 0))
    n_tiles = n // ROW_TILE
    return pl.pallas_call(
        _out_kernel,
        grid=(n_tiles,),
        in_specs=[row(d), row(DN_WIDTH), row(AT_WIDTH), row(CV_WIDTH)]
                 + [_full_spec(a) for a in (wdn, wat, wcv, g, wr, rb, tri)],
        out_specs=[row(d), row(d), row(LANES), pl.BlockSpec((1, SUBLANES, LANES), lambda i: (i, 0, 0))],
        out_shape=[jax.ShapeDtypeStruct((n, d), F32), jax.ShapeDtypeStruct((n, d), BF16),
                   jax.ShapeDtypeStruct((n, LANES), F32),
                   jax.ShapeDtypeStruct((n_tiles, SUBLANES, LANES), F32)],
        compiler_params=pltpu.CompilerParams(dimension_semantics=("parallel",),
                                             vmem_limit_bytes=VMEM_LIMIT),
        name="out_proj",
    )(x2, ydn, yat, ycv, wdn, wat, wcv, g, wr, rb, tri)


def _tile_slots(route, loc_row):
    lane = lax.broadcasted_iota(jnp.int32, route.shape, 1).astype(F32)
    slots = []
    for kk in range(TOP_K):
        base = jnp.sum(jnp.where(lane == route[:, kk:kk + 1], loc_row, 0.0), axis=-1, keepdims=True)
        slots.append(base + route[:, 4 + kk:5 + kk])
    return slots


def _chunk_cols(j):
    return (j * MOE_CHUNK + lax.broadcasted_iota(jnp.int32, (1, MOE_CHUNK), 1)).astype(F32)


def _piece(ref, row):
    return ref.at[pl.ds(pl.multiple_of(row, MOE_PIECE), MOE_PIECE), :]


def _dispatch_kernel(dst_ref, np_ref, zdst_ref, zvalid_ref, nu_ref, route_ref, loc_ref, h_ref, xs_ref,
                     srt, zeros, sem, zsem):
    tile = pl.program_id(0)
    n_pieces = np_ref[tile]
    n_blocks = xs_ref.shape[0] // MOE_BLOCK

    def zero_piece(z):
        return pltpu.make_async_copy(_piece(zeros, 0), _piece(xs_ref, zdst_ref[z]), zsem)

    def zero_block(b):
        return pltpu.make_async_copy(
            zeros, xs_ref.at[pl.ds(pl.multiple_of(b * MOE_BLOCK, MOE_BLOCK), MOE_BLOCK), :], zsem)

    @pl.when(tile == 0)
    def _():
        zeros[...] = jnp.zeros(zeros.shape, BF16)
        for wait in (False, True):
            def piece_body(z, carry, wait=wait):
                @pl.when(zvalid_ref[z] != 0)
                def _():
                    zero_piece(z).wait() if wait else zero_piece(z).start()
                return carry

            def block_body(b, carry, wait=wait):
                zero_block(b).wait() if wait else zero_block(b).start()
                return carry

            lax.fori_loop(0, N_EXPERTS * ZERO_PIECES, piece_body, 0)
            lax.fori_loop(nu_ref[0], n_blocks, block_body, 0)

    slot1, slot2 = _tile_slots(route_ref[...], loc_ref[0, 0:1, :])
    lane = lax.broadcasted_iota(jnp.int32, (1, LANES), 1)
    slot_cols = jnp.where(lane == 0, slot1, jnp.where(lane == 1, slot2, 0.0))
    pick = (lax.broadcasted_iota(jnp.int32, (SUBLANES, LANES), 0)
            == lax.broadcasted_iota(jnp.int32, (SUBLANES, LANES), 1)).astype(BF16)
    slot_rows = sum(_mm_nt(pick, piece) for piece in _split3(slot_cols))
    slot1_row = slot_rows[0:1, :]
    slot2_row = slot_rows[1:2, :]
    h = h_ref[...]
    buf = srt.at[tile % 2]

    def chunk(j, carry):
        row = (j * MOE_CHUNK + lax.broadcasted_iota(jnp.int32, (MOE_CHUNK, 1), 0)).astype(F32)
        onehot = jnp.where(row == slot1_row, 1.0, jnp.where(row == slot2_row, 1.0, 0.0)).astype(BF16)
        buf[pl.ds(pl.multiple_of(j * MOE_CHUNK, MOE_CHUNK), MOE_CHUNK), :] = _mm(onehot, h).astype(BF16)
        return carry

    pieces_per_chunk = MOE_CHUNK // MOE_PIECE
    lax.fori_loop(0, (n_pieces + pieces_per_chunk - 1) // pieces_per_chunk, chunk, 0)

    def piece_copy(t, p):
        return pltpu.make_async_copy(_piece(srt.at[t % 2], p * MOE_PIECE),
                                     _piece(xs_ref, dst_ref[t * PIECES_MAX + p]), sem.at[t % 2])

    def start(p, carry):
        piece_copy(tile, p).start()
        return carry

    lax.fori_loop(0, n_pieces, start, 0)

    def wait_tile(t):
        def wait(p, carry):
            piece_copy(t, p).wait()
            return carry
        lax.fori_loop(0, np_ref[t], wait, 0)

    @pl.when(tile > 0)
    def _():
        wait_tile(tile - 1)

    @pl.when(tile == pl.num_programs(0) - 1)
    def _():
        wait_tile(tile)


def _dispatch(plan, route, h, n_slots):
    n, d = h.shape
    n_tiles = n // MOE_TILE
    grid_spec = pltpu.PrefetchScalarGridSpec(
        num_scalar_prefetch=5,
        grid=(n_tiles,),
        in_specs=[pl.BlockSpec((MOE_TILE, LANES), lambda i, *_: (i, 0)),
                  pl.BlockSpec((1, SUBLANES, LANES), lambda i, *_: (i, 0, 0)),
                  pl.BlockSpec((MOE_TILE, d), lambda i, *_: (i, 0))],
        out_specs=pl.BlockSpec(memory_space=pl.ANY),
        scratch_shapes=[pltpu.VMEM((2, TILE_SLOTS, d), BF16), pltpu.VMEM((MOE_BLOCK, d), BF16),
                        pltpu.SemaphoreType.DMA((2,)), pltpu.SemaphoreType.DMA(())],
    )
    return pl.pallas_call(
        _dispatch_kernel,
        grid_spec=grid_spec,
        out_shape=jax.ShapeDtypeStruct((n_slots, d), BF16),
        compiler_params=pltpu.CompilerParams(dimension_semantics=("arbitrary",),
                                             vmem_limit_bytes=VMEM_LIMIT),
        name="moe_dispatch",
    )(plan["dst"], plan["n_pieces"], plan["zdst"], plan["zvalid"], plan["n_used"], route,
      plan["loc"], h)


def _expert_kernel(be_ref, nu_ref, x_ref, wg_ref, wu_ref, wd_ref, y_ref, wg_bf, wu_bf, wd_bf):
    i = pl.program_id(0)
    changed = jnp.logical_or(i == 0, be_ref[i] != be_ref[jnp.maximum(i - 1, 0)])

    @pl.when(jnp.logical_and(changed, i < nu_ref[0]))
    def _():
        wg_bf[...] = wg_ref[0].astype(BF16)
        wu_bf[...] = wu_ref[0].astype(BF16)
        wd_bf[...] = wd_ref[0].astype(BF16)

    @pl.when(i < nu_ref[0])
    def _():
        x = x_ref[...]
        g = _mm(x, wg_bf[...])
        u = _mm(x, wu_bf[...])
        y_ref[...] = _mm((_silu(g) * u).astype(BF16), wd_bf[...]).astype(BF16)

    @pl.when(i >= nu_ref[0])
    def _():
        y_ref[...] = jnp.zeros(y_ref.shape, BF16)


def _experts(plan, xs, wg, wu, wd, layer):
    ns, d = xs.shape
    de = wg.shape[3]
    rows = lambda i, be, nu: (jnp.minimum(i, nu[0] - 1), 0)
    grid_spec = pltpu.PrefetchScalarGridSpec(
        num_scalar_prefetch=2,
        grid=(ns // MOE_BLOCK,),
        in_specs=[pl.BlockSpec((MOE_BLOCK, d), rows),
                  pl.BlockSpec((None, 1, d, de), lambda i, be, nu: (layer, be[i], 0, 0)),
                  pl.BlockSpec((None, 1, d, de), lambda i, be, nu: (layer, be[i], 0, 0)),
                  pl.BlockSpec((None, 1, de, d), lambda i, be, nu: (layer, be[i], 0, 0))],
        out_specs=pl.BlockSpec((MOE_BLOCK, d), lambda i, be, nu: (i, 0)),
        scratch_shapes=[pltpu.VMEM((d, de), BF16), pltpu.VMEM((d, de), BF16),
                        pltpu.VMEM((de, d), BF16)],
    )
    return pl.pallas_call(
        _expert_kernel,
        grid_spec=grid_spec,
        out_shape=jax.ShapeDtypeStruct((ns, d), BF16),
        compiler_params=pltpu.CompilerParams(dimension_semantics=("arbitrary",),
                                             vmem_limit_bytes=VMEM_LIMIT),
        name="moe_experts",
    )(plan["block_expert"], plan["n_used"], xs, wg, wu, wd)


def _combine_kernel(dst_ref, np_ref, route_ref, loc_ref, x_ref, y_ref, o_ref, ysrt, sem):
    tile = pl.program_id(0)
    n_pieces = np_ref[tile]

    @pl.when(tile == 0)
    def _():
        ysrt[...] = jnp.zeros(ysrt.shape, BF16)

    def piece_copy(t, p):
        return pltpu.make_async_copy(_piece(y_ref, dst_ref[t * PIECES_MAX + p]),
                                     _piece(ysrt.at[t % 2], p * MOE_PIECE), sem.at[t % 2])

    def fetch_tile(t):
        def start(p, carry):
            piece_copy(t, p).start()
            return carry
        lax.fori_loop(0, np_ref[t], start, 0)

    @pl.when(tile == 0)
    def _():
        fetch_tile(tile)

    @pl.when(tile + 1 < pl.num_programs(0))
    def _():
        fetch_tile(tile + 1)

    route = route_ref[...]
    slot1, slot2 = _tile_slots(route, loc_ref[0, 0:1, :])
    g1 = route[:, 2:3]
    g2 = route[:, 3:4]
    gates = jnp.concatenate(
        [jnp.where(_chunk_cols(j) == slot1, g1, jnp.where(_chunk_cols(j) == slot2, g2, 0.0)).astype(BF16)
         for j in range(TILE_SLOTS // MOE_CHUNK)], axis=1)

    def wait(p, carry):
        piece_copy(tile, p).wait()
        return carry

    lax.fori_loop(0, n_pieces, wait, 0)
    o_ref[...] = x_ref[...] + _mm(gates, ysrt[tile % 2])


def _combine(plan, route, x2, y):
    n, d = x2.shape
    grid_spec = pltpu.PrefetchScalarGridSpec(
        num_scalar_prefetch=2,
        grid=(n // MOE_TILE,),
        in_specs=[pl.BlockSpec((MOE_TILE, LANES), lambda i, *_: (i, 0)),
                  pl.BlockSpec((1, SUBLANES, LANES), lambda i, *_: (i, 0, 0)),
                  pl.BlockSpec((MOE_TILE, d), lambda i, *_: (i, 0)),
                  pl.BlockSpec(memory_space=pl.ANY)],
        out_specs=pl.BlockSpec((MOE_TILE, d), lambda i, *_: (i, 0)),
        scratch_shapes=[pltpu.VMEM((2, TILE_SLOTS, d), BF16), pltpu.SemaphoreType.DMA((2,))],
    )
    return pl.pallas_call(
        _combine_kernel,
        grid_spec=grid_spec,
        out_shape=jax.ShapeDtypeStruct((n, d), F32),
        compiler_params=pltpu.CompilerParams(dimension_semantics=("arbitrary",),
                                             vmem_limit_bytes=VMEM_LIMIT),
        name="moe_combine",
    )(plan["dst"], plan["n_pieces"], route, plan["loc"], x2, y)


def _round_up(v, m):
    return (v + m - 1) // m * m


def _moe_plan(counts, n_slots):
    n_tiles = counts.shape[0]
    cnt = counts[:, 0, :N_EXPERTS].astype(jnp.int32)
    seg = _round_up(cnt, MOE_PIECE)
    loc_end = jnp.cumsum(seg, axis=1)
    loc_start = loc_end - seg
    totals = jnp.sum(seg, axis=0)
    padded = _round_up(totals, MOE_BLOCK)
    pad_end = jnp.cumsum(padded)
    pad_start = pad_end - padded
    seg_start = pad_start[None, :] + jnp.cumsum(seg, axis=0) - seg
    piece_off = jnp.arange(PIECES_MAX, dtype=jnp.int32) * MOE_PIECE
    piece_e = jnp.sum(loc_end[:, None, :] <= piece_off[None, :, None], axis=2)
    hit = piece_e[..., None] == jnp.arange(N_EXPERTS)
    shift = jnp.sum(jnp.where(hit, (seg_start - loc_start)[:, None, :], 0), axis=2)
    valid = piece_off[None, :] < loc_end[:, -1:]
    dst = jnp.where(valid, shift + piece_off[None, :], 0).astype(jnp.int32).reshape(-1)
    zk = jnp.arange(ZERO_PIECES, dtype=jnp.int32)[None, :] * MOE_PIECE
    zvalid = zk < (padded - totals)[:, None]
    zdst = jnp.where(zvalid, (pad_start + totals)[:, None] + zk, 0)
    blk_start = jnp.arange(n_slots // MOE_BLOCK, dtype=jnp.int32) * MOE_BLOCK
    block_expert = jnp.minimum(jnp.sum(pad_end[None, :] <= blk_start[:, None], axis=1), N_EXPERTS - 1)
    loc = jnp.pad(loc_start.astype(F32), ((0, 0), (0, LANES - N_EXPERTS)))
    return {
        "dst": dst,
        "n_pieces": (loc_end[:, -1] // MOE_PIECE).astype(jnp.int32),
        "zdst": zdst.astype(jnp.int32).reshape(-1),
        "zvalid": zvalid.astype(jnp.int32).reshape(-1),
        "loc": jnp.broadcast_to(loc[:, None, :], (n_tiles, SUBLANES, LANES)),
        "block_expert": block_expert.astype(jnp.int32),
        "n_used": (pad_end[-1:] // MOE_BLOCK).astype(jnp.int32),
    }


def _pad_lanes(a, width=LANES):
    return jnp.pad(a, [(0, 0)] * (a.ndim - 1) + [(0, width - a.shape[-1])])


def kernel(x, norm_mix, w_in, dn_conv, dn_a_log, dn_dt_bias, dn_out_norm, at_q_norm, at_k_norm,
           rel_bias, cv_dw, cv_dw_bias, cv_ln_g, cv_ln_b, w_out, norm_ffn, router_group_w,
           router_group_b, router_expert_w, router_expert_b, ex_gate, ex_up, ex_down):
    bsz, seq, d = x.shape
    n_tok = bsz * seq
    depth = w_in.shape[0]
    c_ab = 4 * DN_WIDTH
    c_at = c_ab + 2 * DN_HEADS
    c_cv = c_at + 3 * AT_WIDTH
    bd_at = _block_diag_ones(AT_WIDTH, HEAD_DIM, BF16)
    bd_dn = _block_diag_ones(DN_WIDTH, HEAD_DIM, BF16)
    bias = _bias_tables(rel_bias)
    tri = (jnp.arange(ROW_TILE)[:, None] > jnp.arange(ROW_TILE)[None, :]).astype(BF16)
    per_head_lanes = lambda v: jnp.repeat(v, HEAD_DIM)[None, :]
    n_tiles = n_tok // MOE_TILE
    n_slots = _round_up(TOP_K * n_tok + n_tiles * N_EXPERTS * (MOE_PIECE - 1)
                        + N_EXPERTS * (MOE_BLOCK - 1), MOE_BLOCK)

    x2 = x.reshape(n_tok, d)
    for layer in range(depth):
        w_l = w_in[layer]
        dn, aq, ak, av, cu, ab = _proj(
            x2, norm_mix[layer][None, :],
            w_l[:, :c_ab].astype(BF16), w_l[:, c_at:c_cv].astype(BF16), w_l[:, c_cv:].astype(BF16),
            _pad_lanes(w_l[:, c_ab:c_at]).astype(BF16), bd_at,
            jnp.tile(at_q_norm[layer], AT_HEADS)[None, :] * (HEAD_DIM ** -0.5),
            jnp.tile(at_k_norm[layer], AT_HEADS)[None, :])

        y_dn = _deltanet(dn.reshape(bsz, seq, -1), ab.reshape(bsz, seq, LANES), dn_conv[layer],
                         per_head_lanes(dn_a_log[layer]), per_head_lanes(dn_dt_bias[layer]),
                         jnp.tile(dn_out_norm[layer], DN_HEADS)[None, :], bd_dn)
        y_at = _attention(aq.reshape(bsz, seq, -1), ak.reshape(bsz, seq, -1),
                          av.reshape(bsz, seq, -1), bias)
        y_cv = _conformer_conv(cu.reshape(bsz, seq, -1), cv_dw[layer], cv_dw_bias[layer][None, :],
                               cv_ln_g[layer][None, :], cv_ln_b[layer][None, :])

        wo = w_out[layer].astype(BF16)
        w_r = _pad_lanes(jnp.concatenate([router_expert_w[layer], router_group_w[layer]], axis=1))
        b_r = _pad_lanes(jnp.concatenate([router_expert_b[layer], router_group_b[layer]])[None, :])
        x_mid, h_ffn, route, counts = _out_proj(
            x2, y_dn.reshape(n_tok, DN_WIDTH), y_at.reshape(n_tok, AT_WIDTH),
            y_cv.reshape(n_tok, CV_WIDTH), wo[:DN_WIDTH], wo[DN_WIDTH:DN_WIDTH + AT_WIDTH],
            wo[DN_WIDTH + AT_WIDTH:], norm_ffn[layer][None, :], w_r.astype(BF16), b_r, tri)

        plan = _moe_plan(counts, n_slots)
        xs = _dispatch(plan, route, h_ffn, n_slots)
        y = _experts(plan, xs, ex_gate, ex_up, ex_down, layer)
        x2 = _combine(plan, route, x_mid, y)
    return x2.reshape(bsz, seq, d)
```

```python
import math

import jax
import jax.numpy as jnp
import numpy as np
from jax import lax
from jax.experimental import pallas as pl
from jax.experimental.pallas import tpu as pltpu

F32 = jnp.float32
BF16 = jnp.bfloat16
HIGHEST = lax.Precision.HIGHEST

EPS = 1e-6
NEG_INF = -1e30

HEAD_DIM = 64
DN_HEADS = 4
DN_WIDTH = DN_HEADS * HEAD_DIM
DN_CONV = 4
DN_CHUNK = 64
AT_HEADS = 8
AT_WIDTH = AT_HEADS * HEAD_DIM
AT_BLOCK = 128
WINDOWS = ((128, 1), (512, 4), (2048, 16))
MAX_DILATION = 16
AT_TILE = 256
AT_LOCKSTEP = 4
N_BUCKETS = 32
MAX_DISTANCE = 2048
CV_WIDTH = 256
CV_KERNEL = 31
N_GROUPS = 4
EXPERTS_PER_GROUP = 8
N_EXPERTS = N_GROUPS * EXPERTS_PER_GROUP
TOP_K = 2

LANES = 128
SUBLANES = 8
VMEM_LIMIT = 52 * 1024 * 1024

ROW_TILE = 512
SEQ_TILE = 256
CV_TILE = 64
OUT_SLAB = 256
DN_TILE = 64
DN_PREP_CHUNKS = 4
MOE_BLOCK = 512
MOE_TILE = ROW_TILE
MOE_PIECE = 16
MOE_CHUNK = 256
TILE_SLOTS = -(-(TOP_K * MOE_TILE + N_EXPERTS * (MOE_PIECE - 1)) // MOE_CHUNK) * MOE_CHUNK
PIECES_MAX = TILE_SLOTS // MOE_PIECE
ZERO_PIECES = MOE_BLOCK // MOE_PIECE - 1


def _mm(a, b, precision=None):
    return jnp.dot(a, b, preferred_element_type=F32, precision=precision)


def _mm_nt(a, b):
    return lax.dot_general(a, b, (((1,), (1,)), ((), ())), preferred_element_type=F32)


def _mm_tn(a, b):
    return lax.dot_general(a, b, (((0,), (0,)), ((), ())), preferred_element_type=F32)


def _sigmoid(x):
    return 0.5 * jnp.tanh(0.5 * x) + 0.5


def _silu(x):
    return x * _sigmoid(x)


def _split3(x):
    p0 = x.astype(BF16)
    r1 = x - p0.astype(F32)
    p1 = r1.astype(BF16)
    p2 = (r1 - p1.astype(F32)).astype(BF16)
    return p0, p1, p2


def _alternate(*stages):
    live = list(stages)
    while live:
        for gen in list(live):
            if next(gen, live) is live:
                live.remove(gen)


def _full_spec(a):
    nd = a.ndim
    return pl.BlockSpec(a.shape, lambda *_: (0,) * nd)


def _block_diag_ones(width, block, dtype):
    r = jnp.arange(width)[:, None] // block
    c = jnp.arange(width)[None, :] // block
    return (r == c).astype(dtype)


def _proj_kernel(x_ref, g_ref, wdn_ref, wat_ref, wcv_ref, wab_ref, bd_ref, qn_ref, kn_ref,
                 dn_ref, aq_ref, ak_ref, av_ref, cv_ref, ab_ref):
    x = x_ref[...]
    ms = jnp.mean(x * x, axis=-1, keepdims=True)
    h = (x * lax.rsqrt(ms + EPS) * g_ref[...]).astype(BF16)
    dn_ref[...] = _mm(h, wdn_ref[...]).astype(BF16)
    cv_ref[...] = _mm(h, wcv_ref[...]).astype(BF16)
    ab_ref[...] = _mm(h, wab_ref[...])
    at = _mm(h, wat_ref[...])
    q = at[:, 0:AT_WIDTH]
    k = at[:, AT_WIDTH:2 * AT_WIDTH]
    bd = bd_ref[...]
    qms = _mm((q * q).astype(BF16), bd) * (1.0 / HEAD_DIM)
    kms = _mm((k * k).astype(BF16), bd) * (1.0 / HEAD_DIM)
    aq_ref[...] = (q * lax.rsqrt(qms + EPS) * qn_ref[...]).astype(BF16)
    ak_ref[...] = (k * lax.rsqrt(kms + EPS) * kn_ref[...]).astype(BF16)
    av_ref[...] = at[:, 2 * AT_WIDTH:3 * AT_WIDTH].astype(BF16)


def _proj(x2, g, wdn, wat, wcv, wab, bd, qn, kn):
    n, d = x2.shape
    row = lambda w: pl.BlockSpec((ROW_TILE, w), lambda i: (i, 0))
    widths = (wdn.shape[1], AT_WIDTH, AT_WIDTH, AT_WIDTH, wcv.shape[1], LANES)
    dtypes = (BF16, BF16, BF16, BF16, BF16, F32)
    return pl.pallas_call(
        _proj_kernel,
        grid=(n // ROW_TILE,),
        in_specs=[row(d)] + [_full_spec(a) for a in (g, wdn, wat, wcv, wab, bd, qn, kn)],
        out_specs=[row(w) for w in widths],
        out_shape=[jax.ShapeDtypeStruct((n, w), t) for w, t in zip(widths, dtypes)],
        compiler_params=pltpu.CompilerParams(dimension_semantics=("parallel",),
                                             vmem_limit_bytes=VMEM_LIMIT),
        name="proj",
    )(x2, g, wdn, wat, wcv, wab, bd, qn, kn)


def _per_head(x, block_mask):
    return jnp.where(block_mask, jnp.concatenate([x] * DN_HEADS, axis=0), jnp.zeros((), x.dtype))


def _dn_kernel(dn_ref, ab_ref, cw_ref, alog_ref, dtb_ref, onorm_ref, bd_ref, y_ref,
               xpad, qs, ks, vs, gs, bs, os_, st, *group_bufs):
    set_a, set_b = group_bufs[:6], group_bufs[6:]
    seq = dn_ref.shape[1]
    n_tiles = seq // DN_TILE
    cw3 = 3 * DN_WIDTH
    pad = SUBLANES
    c = DN_CHUNK
    bd = bd_ref[...]
    hid = lax.broadcasted_iota(jnp.int32, (1, DN_WIDTH), 1) // HEAD_DIM

    def expand(cols, first):
        out = cols[:, first + DN_HEADS - 1:first + DN_HEADS]
        for h in range(DN_HEADS - 2, -1, -1):
            out = jnp.where(hid == h, cols[:, first + h:first + h + 1], out)
        return out

    xpad[0:pad, :] = jnp.zeros((pad, cw3), F32)
    for t in range(n_tiles):
        r0 = t * DN_TILE
        xpad[pad + r0:pad + r0 + DN_TILE, :] = dn_ref[0, r0:r0 + DN_TILE, 0:cw3].astype(F32)
    for t in range(n_tiles):
        r0 = t * DN_TILE
        ab = ab_ref[0, r0:r0 + DN_TILE, :]
        sp_in = expand(ab, 0) + dtb_ref[...]
        softplus = jnp.maximum(sp_in, 0.0) + jnp.log(1.0 + jnp.exp(-jnp.abs(sp_in)))
        gs[r0:r0 + DN_TILE, :] = -jnp.exp(alog_ref[...]) * softplus
        bs[r0:r0 + DN_TILE, :] = _sigmoid(expand(ab, DN_HEADS))
        acc = jnp.zeros((DN_TILE, cw3), F32)
        for j in range(DN_CONV):
            off = pad + r0 - (DN_CONV - 1) + j
            acc = acc + xpad[off:off + DN_TILE, :] * cw_ref[j:j + 1, :]
        y = _silu(acc)
        q = y[:, 0:DN_WIDTH]
        k = y[:, DN_WIDTH:2 * DN_WIDTH]
        qss = _mm((q * q).astype(BF16), bd)
        kss = _mm((k * k).astype(BF16), bd)
        qs[r0:r0 + DN_TILE, :] = q * lax.rsqrt(qss + EPS) * (HEAD_DIM ** -0.5)
        ks[r0:r0 + DN_TILE, :] = k * lax.rsqrt(kss + EPS)
        vs[r0:r0 + DN_TILE, :] = y[:, 2 * DN_WIDTH:3 * DN_WIDTH]

    ri = lax.broadcasted_iota(jnp.int32, (c, DN_WIDTH), 0)
    ci = lax.broadcasted_iota(jnp.int32, (c, DN_WIDTH), 1) % HEAD_DIM
    causal = ri >= ci
    strict = ri > ci
    eye_cat = (ri == ci).astype(F32)
    r2 = lax.broadcasted_iota(jnp.int32, (c, c), 0)
    c2 = lax.broadcasted_iota(jnp.int32, (c, c), 1)
    lower_ones = (r2 >= c2).astype(BF16)
    all_ones = jnp.ones((c, c), BF16)
    block_mask = (lax.broadcasted_iota(jnp.int32, (DN_WIDTH, DN_WIDTH), 0) // HEAD_DIM
                  == lax.broadcasted_iota(jnp.int32, (DN_WIDTH, DN_WIDTH), 1) // HEAD_DIM)

    def mm_exact_rhs(lhs_bf, x):
        p0, p1, p2 = _split3(x)
        return _mm(lhs_bf, p0) + _mm(lhs_bf, p1) + _mm(lhs_bf, p2)

    def mm_bd(lhs, rhs_cat):
        return _mm(lhs.astype(BF16), _per_head(rhs_cat.astype(BF16), block_mask))

    def prep(m, dst):
        ws, us, qks, qds, kds, gls = dst
        rows = [pl.multiple_of((DN_PREP_CHUNKS * m + cc) * c, c) for cc in range(DN_PREP_CHUNKS)]
        g_cum = [mm_exact_rhs(lower_ones, gs[pl.ds(r, c), :]) for r in rows]
        yield
        g_row = [mm_exact_rhs(all_ones, g * eye_cat) for g in g_cum]
        yield
        decay = [jnp.exp(jnp.where(causal, g - gr, NEG_INF)) for g, gr in zip(g_cum, g_row)]
        kc = [ks[pl.ds(r, c), :] for r in rows]
        qc = [qs[pl.ds(r, c), :] for r in rows]
        beta = [bs[pl.ds(r, c), :] for r in rows]
        kb = [k * b for k, b in zip(kc, beta)]
        aq = [_mm_nt(jnp.concatenate([b_, q_], axis=0).astype(BF16),
                     _per_head(k_.astype(BF16), block_mask))
              for b_, q_, k_ in zip(kb, qc, kc)]
        a_mat = [jnp.where(strict, x[:c] * d, 0.0) for x, d in zip(aq, decay)]
        qk = [x[c:] * d for x, d in zip(aq, decay)]
        p = [eye_cat - a for a in a_mat]
        yield
        pw = [mm_bd(a, a) for a in a_mat]
        for _ in range(4):
            yield
            both = [mm_bd(jnp.concatenate([p_, x], axis=0), x) for p_, x in zip(p, pw)]
            p = [p_ + b_[:c] for p_, b_ in zip(p, both)]
            pw = [b_[c:] for b_ in both]
        yield
        p = [p_ + mm_bd(p_, x) for p_, x in zip(p, pw)]
        yield
        for i, r in enumerate(rows):
            e_g = jnp.exp(g_cum[i])
            sl = slice(i * c, (i + 1) * c)
            ws[sl, :] = mm_bd(p[i], kb[i] * e_g).astype(BF16)
            us[sl, :] = mm_bd(p[i], vs[pl.ds(r, c), :] * beta[i])
            qks[sl, :] = qk[i].astype(BF16)
            qds[sl, :] = (qc[i] * e_g).astype(BF16)
            g_last = g_cum[i][c - 1:c, :]
            kds[sl, :] = (kc[i] * jnp.exp(g_last - g_cum[i])).astype(BF16)
            gls[i * SUBLANES:(i + 1) * SUBLANES, :] = jnp.broadcast_to(jnp.exp(g_last),
                                                                       (SUBLANES, DN_WIDTH))

    def scan_group(m, src):
        ws, us, qks, qds, kds, gls = src
        for i in range(DN_PREP_CHUNKS):
            sl = slice(i * c, (i + 1) * c)
            r = pl.multiple_of((DN_PREP_CHUNKS * m + i) * c, c)
            state = st[...]
            both = _mm(jnp.concatenate([ws[sl, :], qds[sl, :]], axis=0),
                       _per_head(state.astype(BF16), block_mask))
            yield
            v_new = (us[sl, :] - both[:c]).astype(BF16)
            os_[pl.ds(r, c), :] = both[c:] + _mm(qks[sl, :], _per_head(v_new, block_mask))
            kv = _mm_tn(kds[sl, :], v_new)
            upd = kv[(DN_HEADS - 1) * c:DN_HEADS * c, :]
            for h in range(DN_HEADS - 2, -1, -1):
                upd = jnp.where(hid == h, kv[h * c:(h + 1) * c, :], upd)
            yield
            st[...] = state * gls[i * SUBLANES:i * SUBLANES + 1, :] + upd

    alternate = _alternate

    st[...] = jnp.zeros(st.shape, F32)
    n_groups = seq // (DN_PREP_CHUNKS * c)
    assert n_groups % 2 == 0
    alternate(prep(0, set_a))

    def pair(j, carry):
        alternate(prep(2 * j + 1, set_b), scan_group(2 * j, set_a))
        alternate(prep(2 * j + 2, set_a), scan_group(2 * j + 1, set_b))
        return carry

    lax.fori_loop(0, n_groups // 2 - 1, pair, 0)
    alternate(prep(n_groups - 1, set_b), scan_group(n_groups - 2, set_a))
    alternate(scan_group(n_groups - 1, set_b))

    for t in range(n_tiles):
        r0 = t * DN_TILE
        o = os_[r0:r0 + DN_TILE, :]
        z = dn_ref[0, r0:r0 + DN_TILE, cw3:cw3 + DN_WIDTH].astype(F32)
        ms = _mm((o * o).astype(BF16), bd) * (1.0 / HEAD_DIM)
        y_ref[0, r0:r0 + DN_TILE, :] = (o * lax.rsqrt(ms + EPS) * onorm_ref[...] * _silu(z)).astype(BF16)


def _deltanet(dn, ab, conv_w, alog_cat, dtb_cat, onorm_cat, bd):
    b, seq, w = dn.shape
    f32buf = pltpu.VMEM((seq, DN_WIDTH), F32)
    group_rows = DN_PREP_CHUNKS * DN_CHUNK
    group_set = [pltpu.VMEM((group_rows, DN_WIDTH), BF16),
                 pltpu.VMEM((group_rows, DN_WIDTH), F32),
                 pltpu.VMEM((group_rows, DN_WIDTH), BF16),
                 pltpu.VMEM((group_rows, DN_WIDTH), BF16),
                 pltpu.VMEM((group_rows, DN_WIDTH), BF16),
                 pltpu.VMEM((DN_PREP_CHUNKS * SUBLANES, DN_WIDTH), F32)]
    return pl.pallas_call(
        _dn_kernel,
        grid=(b,),
        in_specs=[pl.BlockSpec((1, seq, w), lambda i: (i, 0, 0)),
                  pl.BlockSpec((1, seq, LANES), lambda i: (i, 0, 0))]
                 + [_full_spec(a) for a in (conv_w, alog_cat, dtb_cat, onorm_cat, bd)],
        out_specs=pl.BlockSpec((1, seq, DN_WIDTH), lambda i: (i, 0, 0)),
        out_shape=jax.ShapeDtypeStruct((b, seq, DN_WIDTH), BF16),
        scratch_shapes=[pltpu.VMEM((SUBLANES + seq, 3 * DN_WIDTH), F32),
                        f32buf, f32buf, f32buf, f32buf, f32buf,
                        f32buf,
                        pltpu.VMEM((DN_CHUNK, DN_WIDTH), F32)]
                       + group_set + group_set,
        compiler_params=pltpu.CompilerParams(dimension_semantics=("parallel",),
                                             vmem_limit_bytes=VMEM_LIMIT),
        name="deltanet",
    )(dn, ab, conv_w, alog_cat, dtb_cat, onorm_cat, bd)


def _run_starts(dil, r, i):
    if dil == 16:
        return [(tt * AT_TILE + r * 16, 16) for tt in range(AT_BLOCK * dil // AT_TILE)]
    if dil == 4:
        return [((2 * i + th) * AT_TILE + (4 * s + r) * 16, 16) for th in range(2) for s in range(4)]
    assert dil == 1
    return [((i // 2) * AT_TILE + rr * 16 + 8 * (i % 2), 8) for rr in range(16)]


def _run_order(dil):
    if dil == 16:
        return np.arange(AT_BLOCK)
    if dil == 4:
        th, s, ml = np.meshgrid(np.arange(2), np.arange(4), np.arange(16), indexing="ij")
        return (64 * th + 4 * ml + s).reshape(-1)
    rr, m8 = np.meshgrid(np.arange(16), np.arange(8), indexing="ij")
    return (16 * m8 + rr).reshape(-1)


def _load_runs(ref, p, runs):
    return jnp.concatenate([ref[p, pl.ds(pl.multiple_of(s, n), n), :] for s, n in runs], axis=0)


def _store_runs(ref, p, runs, val):
    off = 0
    for s, n in runs:
        ref[p, pl.ds(pl.multiple_of(s, n), n), :] = val[off:off + n]
        off += n


def _attn_kernel(q_ref, k_ref, v_ref, perm_ref, perm_t_ref, bias_ref, y_ref, qf, kf, vf, acc, ms, ls):
    seq = q_ref.shape[1]
    pairs = AT_HEADS // 2
    perm = perm_ref[...]
    for t in range(seq // AT_TILE):
        r0 = t * AT_TILE
        for src, dst in ((q_ref, qf), (k_ref, kf), (v_ref, vf)):
            rows = _mm(perm, src[0, r0:r0 + AT_TILE, :])
            for p in range(pairs):
                dst[p, r0:r0 + AT_TILE, :] = rows[:, p * LANES:(p + 1) * LANES]

    lane = lax.broadcasted_iota(jnp.int32, (1, LANES), 1)
    key_col = lax.broadcasted_iota(jnp.int32, (1, 2 * AT_BLOCK), 1)
    order = sorted(range(len(WINDOWS)), key=lambda g: -WINDOWS[g][1])
    for step, grp in enumerate(order):
        dil = WINDOWS[grp][1]
        nb = seq // dil // AT_BLOCK
        is_first = step == 0
        is_last = step == len(order) - 1

        def unit(u, carry, dil=dil, nb=nb, grp=grp, is_first=is_first, is_last=is_last):
            r = u // nb
            i = u % nb
            q_runs = _run_starts(dil, r, i)
            p_runs = _run_starts(dil, r, jnp.maximum(i - 1, 0))
            no_prev = jnp.where(jnp.logical_and(i == 0, key_col < AT_BLOCK), NEG_INF, 0.0)
            def head_pair(p):
                q2 = _load_runs(qf, p, q_runs).astype(BF16)
                k2 = jnp.concatenate([_load_runs(kf, p, p_runs), _load_runs(kf, p, q_runs)],
                                     axis=0).astype(BF16)
                v2 = jnp.concatenate([_load_runs(vf, p, p_runs), _load_runs(vf, p, q_runs)],
                                     axis=0).astype(BF16)
                v_ext = jnp.concatenate([v2, jnp.ones(v2.shape, BF16)], axis=1)
                masks = [(lane // HEAD_DIM) == hh for hh in range(2)]
                scores = [_mm_nt(jnp.where(mask, q2, jnp.zeros_like(q2)), k2)
                          + bias_ref[grp, 2 * p + hh] + no_prev for hh, mask in enumerate(masks)]
                yield
                maxes = [jnp.max(s, axis=-1, keepdims=True) for s in scores]
                probs = [jnp.exp(s - m_h).astype(BF16) for s, m_h in zip(scores, maxes)]
                yield
                results = [_mm(pexp, v_ext) for pexp in probs]
                yield
                m_new = jnp.where(masks[1], maxes[1], jnp.broadcast_to(maxes[0], (AT_BLOCK, LANES)))
                o_new = jnp.where(masks[1], results[1][:, :LANES], results[0][:, :LANES])
                l_new = jnp.where(masks[1], results[1][:, LANES:], results[0][:, LANES:])
                if not is_first:
                    m_old = _load_runs(ms, p, q_runs)
                    m_tot = jnp.maximum(m_old, m_new)
                    a_old = jnp.exp(m_old - m_tot)
                    a_new = jnp.exp(m_new - m_tot)
                    l_new = a_old * _load_runs(ls, p, q_runs) + a_new * l_new
                    o_new = a_old * _load_runs(acc, p, q_runs) + a_new * o_new
                    m_new = m_tot
                if is_last:
                    _store_runs(acc, p, q_runs, o_new / l_new)
                else:
                    _store_runs(ms, p, q_runs, m_new)
                    _store_runs(ls, p, q_runs, l_new)
                    _store_runs(acc, p, q_runs, o_new)

            for p0 in range(0, pairs, AT_LOCKSTEP):
                _alternate(*[head_pair(p) for p in range(p0, p0 + AT_LOCKSTEP)])
            return carry

        lax.fori_loop(0, seq // AT_BLOCK, unit, 0)

    perm_t = perm_t_ref[...]
    for t in range(seq // AT_TILE):
        r0 = t * AT_TILE
        for p in range(pairs):
            y_ref[0, r0:r0 + AT_TILE, p * LANES:(p + 1) * LANES] = _mm(
                perm_t, acc[p, r0:r0 + AT_TILE, :].astype(BF16)).astype(BF16)


def _tile_permutation():
    t = np.arange(AT_TILE)
    row = (t % MAX_DILATION) * (AT_TILE // MAX_DILATION) + t // MAX_DILATION
    perm = np.zeros((AT_TILE, AT_TILE), np.float32)
    perm[row, t] = 1.0
    return perm


def _attention(aq, ak, av, bias):
    b, seq, w = aq.shape
    assert seq == AT_BLOCK * MAX_DILATION and [d for _, d in WINDOWS] == [1, 4, 16]
    slab = pltpu.VMEM((AT_HEADS // 2, seq, LANES), F32)
    tok = pl.BlockSpec((1, seq, w), lambda i: (i, 0, 0))
    perm = _tile_permutation()
    perm_in = jnp.asarray(perm, BF16)
    perm_out = jnp.asarray(perm.T, BF16)
    return pl.pallas_call(
        _attn_kernel,
        grid=(b,),
        in_specs=[tok, tok, tok, _full_spec(perm_in), _full_spec(perm_out), _full_spec(bias)],
        out_specs=tok,
        out_shape=jax.ShapeDtypeStruct((b, seq, w), BF16),
        scratch_shapes=[slab] * 6,
        compiler_params=pltpu.CompilerParams(dimension_semantics=("parallel",),
                                             vmem_limit_bytes=VMEM_LIMIT),
        name="dilated_attention",
    )(aq, ak, av, perm_in, perm_out, bias)


def _t5_bucket(dist):
    max_exact = N_BUCKETS // 2
    d = np.maximum(dist, 1).astype(np.float32)
    log_bucket = max_exact + (np.log(d / np.float32(max_exact))
                              / np.float32(math.log(MAX_DISTANCE / max_exact))
                              * np.float32(N_BUCKETS - max_exact)).astype(np.int32)
    return np.where(dist < max_exact, dist, np.minimum(log_bucket, N_BUCKETS - 1))


def _bias_tables(rel_bias):
    tabs = []
    for window, dil in WINDOWS:
        n_back = window // dil
        j = _run_order(dil)
        rel = j[:, None] + AT_BLOCK - np.concatenate([j, AT_BLOCK + j])[None, :]
        valid = (rel >= 0) & (rel <= n_back)
        bucket = _t5_bucket(dil * np.clip(rel, 0, n_back)).reshape(-1)
        onehot = (np.arange(N_BUCKETS)[:, None] == bucket[None, :]).astype(np.float32)
        bias = jnp.dot(rel_bias.astype(F32).T, jnp.asarray(onehot, BF16).astype(F32), precision=HIGHEST)
        bias = bias.reshape(AT_HEADS, AT_BLOCK, 2 * AT_BLOCK)
        tabs.append(jnp.where(jnp.asarray(valid)[None], bias, NEG_INF))
    return jnp.stack(tabs)


def _cv_kernel(cu_ref, dw_ref, dwb_ref, g_ref, b_ref, y_ref, ypad):
    seq = cu_ref.shape[1]
    n_tiles = seq // SEQ_TILE
    pad = 4 * SUBLANES
    ypad[0:pad, :] = jnp.zeros((pad, CV_WIDTH), F32)
    for t in range(n_tiles):
        r0 = t * SEQ_TILE
        u = cu_ref[0, r0:r0 + SEQ_TILE, :].astype(F32)
        ypad[pad + r0:pad + r0 + SEQ_TILE, :] = u[:, :CV_WIDTH] * _sigmoid(u[:, CV_WIDTH:])
    for t in range(seq // CV_TILE):
        r0 = t * CV_TILE
        window = ypad[r0:r0 + pad + CV_TILE, :]
        acc = jnp.zeros((CV_TILE, CV_WIDTH), F32) + dwb_ref[...]
        for b in range(SUBLANES):
            rolled = pltpu.roll(window, b, axis=0) if b else window
            for j in range(CV_KERNEL):
                off = pad - (CV_KERNEL - 1) + j
                if (-off) % SUBLANES == b:
                    a8 = off + b
                    acc = acc + rolled[a8:a8 + CV_TILE, :] * dw_ref[j:j + 1, :]
        mu = jnp.mean(acc, axis=-1, keepdims=True)
        cen = acc - mu
        var = jnp.mean(cen * cen, axis=-1, keepdims=True)
        yn = cen * lax.rsqrt(var + EPS) * g_ref[...] + b_ref[...]
        y_ref[0, r0:r0 + CV_TILE, :] = _silu(yn).astype(BF16)


def _conformer_conv(cu, dw, dwb, ln_g, ln_b):
    b, seq, w = cu.shape
    return pl.pallas_call(
        _cv_kernel,
        grid=(b,),
        in_specs=[pl.BlockSpec((1, seq, w), lambda i: (i, 0, 0))]
                 + [_full_spec(a) for a in (dw, dwb, ln_g, ln_b)],
        out_specs=pl.BlockSpec((1, seq, CV_WIDTH), lambda i: (i, 0, 0)),
        out_shape=jax.ShapeDtypeStruct((b, seq, CV_WIDTH), BF16),
        scratch_shapes=[pltpu.VMEM((4 * SUBLANES + seq, CV_WIDTH), F32)],
        compiler_params=pltpu.CompilerParams(dimension_semantics=("parallel",),
                                             vmem_limit_bytes=VMEM_LIMIT),
        name="conformer_conv",
    )(cu, dw, dwb, ln_g, ln_b)


def _out_kernel(x_ref, ydn_ref, yat_ref, ycv_ref, wdn_ref, wat_ref, wcv_ref, g_ref, wr_ref, rb_ref,
                tri_ref, xo_ref, h_ref, route_ref, cnt_ref):
    lane = lax.broadcasted_iota(jnp.int32, (OUT_SLAB, LANES), 1)
    picks = {}

    def row_slab(r0):
        sl = slice(r0, r0 + OUT_SLAB)
        x = (x_ref[sl, :] + _mm(ydn_ref[sl, :], wdn_ref[...]) + _mm(yat_ref[sl, :], wat_ref[...])
             + _mm(ycv_ref[sl, :], wcv_ref[...]))
        xo_ref[sl, :] = x
        yield
        ms = jnp.mean(x * x, axis=-1, keepdims=True)
        h = (x * lax.rsqrt(ms + EPS) * g_ref[...]).astype(BF16)
        h_ref[sl, :] = h
        logits = _mm(h, wr_ref[...]) + rb_ref[...]
        yield
        is_group = (lane >= N_EXPERTS) & (lane < N_EXPERTS + N_GROUPS)
        gl = jnp.where(is_group, logits, NEG_INF)
        gmax = jnp.max(gl, axis=-1, keepdims=True)
        gsel = jnp.min(jnp.where(gl == gmax, lane, 2 * LANES), axis=-1, keepdims=True) - N_EXPERTS
        p_group = 1.0 / jnp.sum(jnp.where(is_group, jnp.exp(gl - gmax), 0.0), axis=-1, keepdims=True)
        yield
        lo = gsel * EXPERTS_PER_GROUP
        in_group = (lane >= lo) & (lane < lo + EXPERTS_PER_GROUP)
        el = jnp.where(in_group, logits, NEG_INF)
        v1 = jnp.max(el, axis=-1, keepdims=True)
        i1 = jnp.min(jnp.where(el == v1, lane, LANES), axis=-1, keepdims=True)
        yield
        el2 = jnp.where(lane == i1, NEG_INF, el)
        v2 = jnp.max(el2, axis=-1, keepdims=True)
        i2 = jnp.min(jnp.where(el2 == v2, lane, LANES), axis=-1, keepdims=True)
        t = jnp.exp(v2 - v1)
        g1 = p_group / (1.0 + t)
        picks[r0] = (i1, i2, g1, g1 * t)

    slabs = list(range(0, ROW_TILE, OUT_SLAB))
    _alternate(*[row_slab(r0) for r0 in slabs])

    both = jnp.concatenate(
        [jnp.where((lane == picks[r0][0]) | (lane == picks[r0][1]), 1.0, 0.0).astype(BF16)
         for r0 in slabs], axis=0)
    before = _mm(tri_ref[...], both)
    cnt_ref[0] = _mm(jnp.ones((SUBLANES, ROW_TILE), BF16), both)
    for r0 in slabs:
        i1, i2, g1, g2 = picks[r0]
        seen = before[r0:r0 + OUT_SLAB, :]
        rank1 = jnp.sum(jnp.where(lane == i1, seen, 0.0), axis=-1, keepdims=True)
        rank2 = jnp.sum(jnp.where(lane == i2, seen, 0.0), axis=-1, keepdims=True)
        route = jnp.where(lane == 0, i1.astype(F32), 0.0)
        route = jnp.where(lane == 1, i2.astype(F32), route)
        route = jnp.where(lane == 2, g1, route)
        route = jnp.where(lane == 3, g2, route)
        route = jnp.where(lane == 4, rank1, route)
        route = jnp.where(lane == 5, rank2, route)
        route_ref[r0:r0 + OUT_SLAB, :] = route


def _out_proj(x2, ydn, yat, ycv, wdn, wat, wcv, g, wr, rb, tri):
    n, d = x2.shape
    row = lambda w: pl.BlockSpec((ROW_TILE, w), lambda i: (i, 0))
    n_tiles = n // ROW_TILE
    return pl.pallas_call(
        _out_kernel,
        grid=(n_tiles,),
        in_specs=[row(d), row(DN_WIDTH), row(AT_WIDTH), row(CV_WIDTH)]
                 + [_full_spec(a) for a in (wdn, wat, wcv, g, wr, rb, tri)],
        out_specs=[row(d), row(d), row(LANES), pl.BlockSpec((1, SUBLANES, LANES), lambda i: (i, 0, 0))],
        out_shape=[jax.ShapeDtypeStruct((n, d), F32), jax.ShapeDtypeStruct((n, d), BF16),
                   jax.ShapeDtypeStruct((n, LANES), F32),
                   jax.ShapeDtypeStruct((n_tiles, SUBLANES, LANES), F32)],
        compiler_params=pltpu.CompilerParams(dimension_semantics=("parallel",),
                                             vmem_limit_bytes=VMEM_LIMIT),
        name="out_proj",
    )(x2, ydn, yat, ycv, wdn, wat, wcv, g, wr, rb, tri)


def _tile_slots(route, loc_row):
    lane = lax.broadcasted_iota(jnp.int32, route.shape, 1).astype(F32)
    slots = []
    for kk in range(TOP_K):
        base = jnp.sum(jnp.where(lane == route[:, kk:kk + 1], loc_row, 0.0), axis=-1, keepdims=True)
        slots.append(base + route[:, 4 + kk:5 + kk])
    return slots


def _chunk_cols(j):
    return (j * MOE_CHUNK + lax.broadcasted_iota(jnp.int32, (1, MOE_CHUNK), 1)).astype(F32)


def _piece(ref, row):
    return ref.at[pl.ds(pl.multiple_of(row, MOE_PIECE), MOE_PIECE), :]


def _dispatch_kernel(dst_ref, np_ref, zdst_ref, zvalid_ref, nu_ref, route_ref, loc_ref, h_ref, xs_ref,
                     srt, zeros, sem, zsem):
    tile = pl.program_id(0)
    n_pieces = np_ref[tile]
    n_blocks = xs_ref.shape[0] // MOE_BLOCK

    def zero_piece(z):
        return pltpu.make_async_copy(_piece(zeros, 0), _piece(xs_ref, zdst_ref[z]), zsem)

    def zero_block(b):
        return pltpu.make_async_copy(
            zeros, xs_ref.at[pl.ds(pl.multiple_of(b * MOE_BLOCK, MOE_BLOCK), MOE_BLOCK), :], zsem)

    @pl.when(tile == 0)
    def _():
        zeros[...] = jnp.zeros(zeros.shape, BF16)
        for wait in (False, True):
            def piece_body(z, carry, wait=wait):
                @pl.when(zvalid_ref[z] != 0)
                def _():
                    zero_piece(z).wait() if wait else zero_piece(z).start()
                return carry

            def block_body(b, carry, wait=wait):
                zero_block(b).wait() if wait else zero_block(b).start()
                return carry

            lax.fori_loop(0, N_EXPERTS * ZERO_PIECES, piece_body, 0)
            lax.fori_loop(nu_ref[0], n_blocks, block_body, 0)

    slot1, slot2 = _tile_slots(route_ref[...], loc_ref[0, 0:1, :])
    lane = lax.broadcasted_iota(jnp.int32, (1, LANES), 1)
    slot_cols = jnp.where(lane == 0, slot1, jnp.where(lane == 1, slot2, 0.0))
    pick = (lax.broadcasted_iota(jnp.int32, (SUBLANES, LANES), 0)
            == lax.broadcasted_iota(jnp.int32, (SUBLANES, LANES), 1)).astype(BF16)
    slot_rows = sum(_mm_nt(pick, piece) for piece in _split3(slot_cols))
    slot1_row = slot_rows[0:1, :]
    slot2_row = slot_rows[1:2, :]
    h = h_ref[...]
    buf = srt.at[tile % 2]

    def chunk(j, carry):
        row = (j * MOE_CHUNK + lax.broadcasted_iota(jnp.int32, (MOE_CHUNK, 1), 0)).astype(F32)
        onehot = jnp.where(row == slot1_row, 1.0, jnp.where(row == slot2_row, 1.0, 0.0)).astype(BF16)
        buf[pl.ds(pl.multiple_of(j * MOE_CHUNK, MOE_CHUNK), MOE_CHUNK), :] = _mm(onehot, h).astype(BF16)
        return carry

    pieces_per_chunk = MOE_CHUNK // MOE_PIECE
    lax.fori_loop(0, (n_pieces + pieces_per_chunk - 1) // pieces_per_chunk, chunk, 0)

    def piece_copy(t, p):
        return pltpu.make_async_copy(_piece(srt.at[t % 2], p * MOE_PIECE),
                                     _piece(xs_ref, dst_ref[t * PIECES_MAX + p]), sem.at[t % 2])

    def start(p, carry):
        piece_copy(tile, p).start()
        return carry

    lax.fori_loop(0, n_pieces, start, 0)

    def wait_tile(t):
        done = pltpu.make_async_copy(_piece(srt.at[t % 2], 0), _piece(xs_ref, 0), sem.at[t % 2])

        def wait(p, carry):
            done.wait()
            return carry
        lax.fori_loop(0, np_ref[t], wait, 0)

    @pl.when(tile > 0)
    def _():
        wait_tile(tile - 1)

    @pl.when(tile == pl.num_programs(0) - 1)
    def _():
        wait_tile(tile)


def _dispatch(plan, route, h, n_slots):
    n, d = h.shape
    n_tiles = n // MOE_TILE
    grid_spec = pltpu.PrefetchScalarGridSpec(
        num_scalar_prefetch=5,
        grid=(n_tiles,),
        in_specs=[pl.BlockSpec((MOE_TILE, LANES), lambda i, *_: (i, 0)),
                  pl.BlockSpec((1, SUBLANES, LANES), lambda i, *_: (i, 0, 0)),
                  pl.BlockSpec((MOE_TILE, d), lambda i, *_: (i, 0))],
        out_specs=pl.BlockSpec(memory_space=pl.ANY),
        scratch_shapes=[pltpu.VMEM((2, TILE_SLOTS, d), BF16), pltpu.VMEM((MOE_BLOCK, d), BF16),
                        pltpu.SemaphoreType.DMA((2,)), pltpu.SemaphoreType.DMA(())],
    )
    return pl.pallas_call(
        _dispatch_kernel,
        grid_spec=grid_spec,
        out_shape=jax.ShapeDtypeStruct((n_slots, d), BF16),
        compiler_params=pltpu.CompilerParams(dimension_semantics=("arbitrary",),
                                             vmem_limit_bytes=VMEM_LIMIT),
        name="moe_dispatch",
    )(plan["dst"], plan["n_pieces"], plan["zdst"], plan["zvalid"], plan["n_used"], route,
      plan["loc"], h)


def _expert_kernel(be_ref, nu_ref, x_ref, wg_ref, wu_ref, wd_ref, y_ref, wg_bf, wu_bf, wd_bf):
    i = pl.program_id(0)
    changed = jnp.logical_or(i == 0, be_ref[i] != be_ref[jnp.maximum(i - 1, 0)])

    @pl.when(jnp.logical_and(changed, i < nu_ref[0]))
    def _():
        wg_bf[...] = wg_ref[0].astype(BF16)
        wu_bf[...] = wu_ref[0].astype(BF16)
        wd_bf[...] = wd_ref[0].astype(BF16)

    @pl.when(i < nu_ref[0])
    def _():
        x = x_ref[...]
        g = _mm(x, wg_bf[...])
        u = _mm(x, wu_bf[...])
        y_ref[...] = _mm((_silu(g) * u).astype(BF16), wd_bf[...]).astype(BF16)

    @pl.when(i >= nu_ref[0])
    def _():
        y_ref[...] = jnp.zeros(y_ref.shape, BF16)


def _experts(plan, xs, wg, wu, wd, layer):
    ns, d = xs.shape
    de = wg.shape[3]
    rows = lambda i, be, nu: (jnp.minimum(i, nu[0] - 1), 0)
    grid_spec = pltpu.PrefetchScalarGridSpec(
        num_scalar_prefetch=2,
        grid=(ns // MOE_BLOCK,),
        in_specs=[pl.BlockSpec((MOE_BLOCK, d), rows),
                  pl.BlockSpec((None, 1, d, de), lambda i, be, nu: (layer, be[i], 0, 0)),
                  pl.BlockSpec((None, 1, d, de), lambda i, be, nu: (layer, be[i], 0, 0)),
                  pl.BlockSpec((None, 1, de, d), lambda i, be, nu: (layer, be[i], 0, 0))],
        out_specs=pl.BlockSpec((MOE_BLOCK, d), lambda i, be, nu: (i, 0)),
        scratch_shapes=[pltpu.VMEM((d, de), BF16), pltpu.VMEM((d, de), BF16),
                        pltpu.VMEM((de, d), BF16)],
    )
    return pl.pallas_call(
        _expert_kernel,
        grid_spec=grid_spec,
        out_shape=jax.ShapeDtypeStruct((ns, d), BF16),
        compiler_params=pltpu.CompilerParams(dimension_semantics=("arbitrary",),
                                             vmem_limit_bytes=VMEM_LIMIT),
        name="moe_experts",
    )(plan["block_expert"], plan["n_used"], xs, wg, wu, wd)


def _combine_kernel(dst_ref, np_ref, route_ref, loc_ref, x_ref, y_ref, o_ref, ysrt, sem):
    tile = pl.program_id(0)
    n_pieces = np_ref[tile]

    @pl.when(tile == 0)
    def _():
        ysrt[...] = jnp.zeros(ysrt.shape, BF16)

    def piece_copy(t, p):
        return pltpu.make_async_copy(_piece(y_ref, dst_ref[t * PIECES_MAX + p]),
                                     _piece(ysrt.at[t % 2], p * MOE_PIECE), sem.at[t % 2])

    def fetch_tile(t):
        def start(p, carry):
            piece_copy(t, p).start()
            return carry
        lax.fori_loop(0, np_ref[t], start, 0)

    @pl.when(tile == 0)
    def _():
        fetch_tile(tile)

    @pl.when(tile + 1 < pl.num_programs(0))
    def _():
        fetch_tile(tile + 1)

    route = route_ref[...]
    slot1, slot2 = _tile_slots(route, loc_ref[0, 0:1, :])
    g1 = route[:, 2:3]
    g2 = route[:, 3:4]
    gates = jnp.concatenate(
        [jnp.where(_chunk_cols(j) == slot1, g1, jnp.where(_chunk_cols(j) == slot2, g2, 0.0)).astype(BF16)
         for j in range(TILE_SLOTS // MOE_CHUNK)], axis=1)

    done = pltpu.make_async_copy(_piece(y_ref, 0), _piece(ysrt.at[tile % 2], 0), sem.at[tile % 2])

    def wait(p, carry):
        done.wait()
        return carry

    lax.fori_loop(0, n_pieces, wait, 0)
    o_ref[...] = x_ref[...] + _mm(gates, ysrt[tile % 2])


def _combine(plan, route, x2, y):
    n, d = x2.shape
    grid_spec = pltpu.PrefetchScalarGridSpec(
        num_scalar_prefetch=2,
        grid=(n // MOE_TILE,),
        in_specs=[pl.BlockSpec((MOE_TILE, LANES), lambda i, *_: (i, 0)),
                  pl.BlockSpec((1, SUBLANES, LANES), lambda i, *_: (i, 0, 0)),
                  pl.BlockSpec((MOE_TILE, d), lambda i, *_: (i, 0)),
                  pl.BlockSpec(memory_space=pl.ANY)],
        out_specs=pl.BlockSpec((MOE_TILE, d), lambda i, *_: (i, 0)),
        scratch_shapes=[pltpu.VMEM((2, TILE_SLOTS, d), BF16), pltpu.SemaphoreType.DMA((2,))],
    )
    return pl.pallas_call(
        _combine_kernel,
        grid_spec=grid_spec,
        out_shape=jax.ShapeDtypeStruct((n, d), F32),
        compiler_params=pltpu.CompilerParams(dimension_semantics=("arbitrary",),
                                             vmem_limit_bytes=VMEM_LIMIT),
        name="moe_combine",
    )(plan["dst"], plan["n_pieces"], route, plan["loc"], x2, y)


def _round_up(v, m):
    return (v + m - 1) // m * m


def _moe_plan(counts, n_slots):
    n_tiles = counts.shape[0]
    cnt = counts[:, 0, :N_EXPERTS].astype(jnp.int32)
    seg = _round_up(cnt, MOE_PIECE)
    loc_end = jnp.cumsum(seg, axis=1)
    loc_start = loc_end - seg
    totals = jnp.sum(seg, axis=0)
    padded = _round_up(totals, MOE_BLOCK)
    pad_end = jnp.cumsum(padded)
    pad_start = pad_end - padded
    seg_start = pad_start[None, :] + jnp.cumsum(seg, axis=0) - seg
    piece_off = jnp.arange(PIECES_MAX, dtype=jnp.int32) * MOE_PIECE
    piece_e = jnp.sum(loc_end[:, None, :] <= piece_off[None, :, None], axis=2)
    hit = piece_e[..., None] == jnp.arange(N_EXPERTS)
    shift = jnp.sum(jnp.where(hit, (seg_start - loc_start)[:, None, :], 0), axis=2)
    valid = piece_off[None, :] < loc_end[:, -1:]
    dst = jnp.where(valid, shift + piece_off[None, :], 0).astype(jnp.int32).reshape(-1)
    zk = jnp.arange(ZERO_PIECES, dtype=jnp.int32)[None, :] * MOE_PIECE
    zvalid = zk < (padded - totals)[:, None]
    zdst = jnp.where(zvalid, (pad_start + totals)[:, None] + zk, 0)
    blk_start = jnp.arange(n_slots // MOE_BLOCK, dtype=jnp.int32) * MOE_BLOCK
    block_expert = jnp.minimum(jnp.sum(pad_end[None, :] <= blk_start[:, None], axis=1), N_EXPERTS - 1)
    loc = jnp.pad(loc_start.astype(F32), ((0, 0), (0, LANES - N_EXPERTS)))
    return {
        "dst": dst,
        "n_pieces": (loc_end[:, -1] // MOE_PIECE).astype(jnp.int32),
        "zdst": zdst.astype(jnp.int32).reshape(-1),
        "zvalid": zvalid.astype(jnp.int32).reshape(-1),
        "loc": jnp.broadcast_to(loc[:, None, :], (n_tiles, SUBLANES, LANES)),
        "block_expert": block_expert.astype(jnp.int32),
        "n_used": (pad_end[-1:] // MOE_BLOCK).astype(jnp.int32),
    }


def _pad_lanes(a, width=LANES):
    return jnp.pad(a, [(0, 0)] * (a.ndim - 1) + [(0, width - a.shape[-1])])


def kernel(x, norm_mix, w_in, dn_conv, dn_a_log, dn_dt_bias, dn_out_norm, at_q_norm, at_k_norm,
           rel_bias, cv_dw, cv_dw_bias, cv_ln_g, cv_ln_b, w_out, norm_ffn, router_group_w,
           router_group_b, router_expert_w, router_expert_b, ex_gate, ex_up, ex_down):
    bsz, seq, d = x.shape
    n_tok = bsz * seq
    depth = w_in.shape[0]
    c_ab = 4 * DN_WIDTH
    c_at = c_ab + 2 * DN_HEADS
    c_cv = c_at + 3 * AT_WIDTH
    bd_at = _block_diag_ones(AT_WIDTH, HEAD_DIM, BF16)
    bd_dn = _block_diag_ones(DN_WIDTH, HEAD_DIM, BF16)
    bias = _bias_tables(rel_bias)
    tri = (jnp.arange(ROW_TILE)[:, None] > jnp.arange(ROW_TILE)[None, :]).astype(BF16)
    per_head_lanes = lambda v: jnp.repeat(v, HEAD_DIM)[None, :]
    n_tiles = n_tok // MOE_TILE
    n_slots = _round_up(TOP_K * n_tok + n_tiles * N_EXPERTS * (MOE_PIECE - 1)
                        + N_EXPERTS * (MOE_BLOCK - 1), MOE_BLOCK)

    x2 = x.reshape(n_tok, d)
    for layer in range(depth):
        w_l = w_in[layer]
        dn, aq, ak, av, cu, ab = _proj(
            x2, norm_mix[layer][None, :],
            w_l[:, :c_ab].astype(BF16), w_l[:, c_at:c_cv].astype(BF16), w_l[:, c_cv:].astype(BF16),
            _pad_lanes(w_l[:, c_ab:c_at]).astype(BF16), bd_at,
            jnp.tile(at_q_norm[layer], AT_HEADS)[None, :] * (HEAD_DIM ** -0.5),
            jnp.tile(at_k_norm[layer], AT_HEADS)[None, :])

        y_dn = _deltanet(dn.reshape(bsz, seq, -1), ab.reshape(bsz, seq, LANES), dn_conv[layer],
                         per_head_lanes(dn_a_log[layer]), per_head_lanes(dn_dt_bias[layer]),
                         jnp.tile(dn_out_norm[layer], DN_HEADS)[None, :], bd_dn)
        y_at = _attention(aq.reshape(bsz, seq, -1), ak.reshape(bsz, seq, -1),
                          av.reshape(bsz, seq, -1), bias)
        y_cv = _conformer_conv(cu.reshape(bsz, seq, -1), cv_dw[layer], cv_dw_bias[layer][None, :],
                               cv_ln_g[layer][None, :], cv_ln_b[layer][None, :])

        wo = w_out[layer].astype(BF16)
        w_r = _pad_lanes(jnp.concatenate([router_expert_w[layer], router_group_w[layer]], axis=1))
        b_r = _pad_lanes(jnp.concatenate([router_expert_b[layer], router_group_b[layer]])[None, :])
        x_mid, h_ffn, route, counts = _out_proj(
            x2, y_dn.reshape(n_tok, DN_WIDTH), y_at.reshape(n_tok, AT_WIDTH),
            y_cv.reshape(n_tok, CV_WIDTH), wo[:DN_WIDTH], wo[DN_WIDTH:DN_WIDTH + AT_WIDTH],
            wo[DN_WIDTH + AT_WIDTH:], norm_ffn[layer][None, :], w_r.astype(BF16), b_r, tri)

        plan = _moe_plan(counts, n_slots)
        xs = _dispatch(plan, route, h_ffn, n_slots)
        y = _experts(plan, xs, ex_gate, ex_up, ex_down, layer)
        x2 = _combine(plan, route, x_mid, y)
    return x2.reshape(bsz, seq, d)
```

```python
import math

import jax
import jax.numpy as jnp
import numpy as np
from jax import lax
from jax.experimental import pallas as pl
from jax.experimental.pallas import tpu as pltpu

F32 = jnp.float32
BF16 = jnp.bfloat16
HIGHEST = lax.Precision.HIGHEST

EPS = 1e-6
NEG_INF = -1e30

HEAD_DIM = 64
DN_HEADS = 4
DN_WIDTH = DN_HEADS * HEAD_DIM
DN_CONV = 4
DN_CHUNK = 64
AT_HEADS = 8
AT_WIDTH = AT_HEADS * HEAD_DIM
AT_BLOCK = 128
WINDOWS = ((128, 1), (512, 4), (2048, 16))
MAX_DILATION = 16
AT_TILE = 256
AT_UNITS = 2
N_BUCKETS = 32
MAX_DISTANCE = 2048
CV_WIDTH = 256
CV_KERNEL = 31
N_GROUPS = 4
EXPERTS_PER_GROUP = 8
N_EXPERTS = N_GROUPS * EXPERTS_PER_GROUP
TOP_K = 2

LANES = 128
SUBLANES = 8
VMEM_LIMIT = 52 * 1024 * 1024

ROW_TILE = 512
SEQ_TILE = 256
CV_TILE = 64
OUT_SLAB = 256
DN_TILE = 64
DN_PREP_CHUNKS = 4
MOE_BLOCK = 512
MOE_TILE = ROW_TILE
MOE_PIECE = 16
MOE_CHUNK = 256
TILE_SLOTS = -(-(TOP_K * MOE_TILE + N_EXPERTS * (MOE_PIECE - 1)) // MOE_CHUNK) * MOE_CHUNK
PIECES_MAX = TILE_SLOTS // MOE_PIECE
ZERO_PIECES = MOE_BLOCK // MOE_PIECE - 1


def _mm(a, b, precision=None):
    return jnp.dot(a, b, preferred_element_type=F32, precision=precision)


def _mm_nt(a, b):
    return lax.dot_general(a, b, (((1,), (1,)), ((), ())), preferred_element_type=F32)


def _mm_tn(a, b):
    return lax.dot_general(a, b, (((0,), (0,)), ((), ())), preferred_element_type=F32)


def _sigmoid(x):
    return 0.5 * jnp.tanh(0.5 * x) + 0.5


def _silu(x):
    return x * _sigmoid(x)


def _split3(x):
    p0 = x.astype(BF16)
    r1 = x - p0.astype(F32)
    p1 = r1.astype(BF16)
    p2 = (r1 - p1.astype(F32)).astype(BF16)
    return p0, p1, p2


def _alternate(*stages):
    live = list(stages)
    while live:
        for gen in list(live):
            if next(gen, live) is live:
                live.remove(gen)


def _full_spec(a):
    nd = a.ndim
    return pl.BlockSpec(a.shape, lambda *_: (0,) * nd)


def _block_diag_ones(width, block, dtype):
    r = jnp.arange(width)[:, None] // block
    c = jnp.arange(width)[None, :] // block
    return (r == c).astype(dtype)


def _proj_kernel(x_ref, g_ref, wdn_ref, wat_ref, wcv_ref, wab_ref, bd_ref, qn_ref, kn_ref,
                 dn_ref, aq_ref, ak_ref, av_ref, cv_ref, ab_ref):
    x = x_ref[...]
    ms = jnp.mean(x * x, axis=-1, keepdims=True)
    h = (x * lax.rsqrt(ms + EPS) * g_ref[...]).astype(BF16)
    dn_ref[...] = _mm(h, wdn_ref[...]).astype(BF16)
    cv_ref[...] = _mm(h, wcv_ref[...]).astype(BF16)
    ab_ref[...] = _mm(h, wab_ref[...])
    at = _mm(h, wat_ref[...])
    q = at[:, 0:AT_WIDTH]
    k = at[:, AT_WIDTH:2 * AT_WIDTH]
    bd = bd_ref[...]
    qms = _mm((q * q).astype(BF16), bd) * (1.0 / HEAD_DIM)
    kms = _mm((k * k).astype(BF16), bd) * (1.0 / HEAD_DIM)
    aq_ref[...] = (q * lax.rsqrt(qms + EPS) * qn_ref[...]).astype(BF16)
    ak_ref[...] = (k * lax.rsqrt(kms + EPS) * kn_ref[...]).astype(BF16)
    av_ref[...] = at[:, 2 * AT_WIDTH:3 * AT_WIDTH].astype(BF16)


def _proj(x2, g, wdn, wat, wcv, wab, bd, qn, kn):
    n, d = x2.shape
    row = lambda w: pl.BlockSpec((ROW_TILE, w), lambda i: (i, 0))
    widths = (wdn.shape[1], AT_WIDTH, AT_WIDTH, AT_WIDTH, wcv.shape[1], LANES)
    dtypes = (BF16, BF16, BF16, BF16, BF16, F32)
    return pl.pallas_call(
        _proj_kernel,
        grid=(n // ROW_TILE,),
        in_specs=[row(d)] + [_full_spec(a) for a in (g, wdn, wat, wcv, wab, bd, qn, kn)],
        out_specs=[row(w) for w in widths],
        out_shape=[jax.ShapeDtypeStruct((n, w), t) for w, t in zip(widths, dtypes)],
        compiler_params=pltpu.CompilerParams(dimension_semantics=("parallel",),
                                             vmem_limit_bytes=VMEM_LIMIT),
        name="proj",
    )(x2, g, wdn, wat, wcv, wab, bd, qn, kn)


def _per_head(x, block_mask):
    return jnp.where(block_mask, jnp.concatenate([x] * DN_HEADS, axis=0), jnp.zeros((), x.dtype))


def _dn_kernel(dn_ref, ab_ref, cw_ref, alog_ref, dtb_ref, onorm_ref, bd_ref, y_ref,
               xpad, qs, ks, vs, gs, bs, os_, st, *group_bufs):
    set_a, set_b = group_bufs[:6], group_bufs[6:]
    seq = dn_ref.shape[1]
    n_tiles = seq // DN_TILE
    cw3 = 3 * DN_WIDTH
    pad = SUBLANES
    c = DN_CHUNK
    bd = bd_ref[...]
    hid = lax.broadcasted_iota(jnp.int32, (1, DN_WIDTH), 1) // HEAD_DIM

    def expand(cols, first):
        out = cols[:, first + DN_HEADS - 1:first + DN_HEADS]
        for h in range(DN_HEADS - 2, -1, -1):
            out = jnp.where(hid == h, cols[:, first + h:first + h + 1], out)
        return out

    xpad[0:pad, :] = jnp.zeros((pad, cw3), F32)
    for t in range(n_tiles):
        r0 = t * DN_TILE
        xpad[pad + r0:pad + r0 + DN_TILE, :] = dn_ref[0, r0:r0 + DN_TILE, 0:cw3].astype(F32)
    for t in range(n_tiles):
        r0 = t * DN_TILE
        ab = ab_ref[0, r0:r0 + DN_TILE, :]
        sp_in = expand(ab, 0) + dtb_ref[...]
        softplus = jnp.maximum(sp_in, 0.0) + jnp.log(1.0 + jnp.exp(-jnp.abs(sp_in)))
        gs[r0:r0 + DN_TILE, :] = -jnp.exp(alog_ref[...]) * softplus
        bs[r0:r0 + DN_TILE, :] = _sigmoid(expand(ab, DN_HEADS))
        acc = jnp.zeros((DN_TILE, cw3), F32)
        for j in range(DN_CONV):
            off = pad + r0 - (DN_CONV - 1) + j
            acc = acc + xpad[off:off + DN_TILE, :] * cw_ref[j:j + 1, :]
        y = _silu(acc)
        q = y[:, 0:DN_WIDTH]
        k = y[:, DN_WIDTH:2 * DN_WIDTH]
        qss = _mm((q * q).astype(BF16), bd)
        kss = _mm((k * k).astype(BF16), bd)
        qs[r0:r0 + DN_TILE, :] = q * lax.rsqrt(qss + EPS) * (HEAD_DIM ** -0.5)
        ks[r0:r0 + DN_TILE, :] = k * lax.rsqrt(kss + EPS)
        vs[r0:r0 + DN_TILE, :] = y[:, 2 * DN_WIDTH:3 * DN_WIDTH]

    ri = lax.broadcasted_iota(jnp.int32, (c, DN_WIDTH), 0)
    ci = lax.broadcasted_iota(jnp.int32, (c, DN_WIDTH), 1) % HEAD_DIM
    causal = ri >= ci
    strict = ri > ci
    eye_cat = (ri == ci).astype(F32)
    r2 = lax.broadcasted_iota(jnp.int32, (c, c), 0)
    c2 = lax.broadcasted_iota(jnp.int32, (c, c), 1)
    lower_ones = (r2 >= c2).astype(BF16)
    all_ones = jnp.ones((c, c), BF16)
    block_mask = (lax.broadcasted_iota(jnp.int32, (DN_WIDTH, DN_WIDTH), 0) // HEAD_DIM
                  == lax.broadcasted_iota(jnp.int32, (DN_WIDTH, DN_WIDTH), 1) // HEAD_DIM)

    def mm_exact_rhs(lhs_bf, x):
        p0, p1, p2 = _split3(x)
        return _mm(lhs_bf, p0) + _mm(lhs_bf, p1) + _mm(lhs_bf, p2)

    def mm_bd(lhs, rhs_cat):
        return _mm(lhs.astype(BF16), _per_head(rhs_cat.astype(BF16), block_mask))

    def prep(m, dst):
        ws, us, qks, qds, kds, gls = dst
        rows = [pl.multiple_of((DN_PREP_CHUNKS * m + cc) * c, c) for cc in range(DN_PREP_CHUNKS)]
        g_cum = [mm_exact_rhs(lower_ones, gs[pl.ds(r, c), :]) for r in rows]
        yield
        g_row = [mm_exact_rhs(all_ones, g * eye_cat) for g in g_cum]
        yield
        decay = [jnp.exp(jnp.where(causal, g - gr, NEG_INF)) for g, gr in zip(g_cum, g_row)]
        kc = [ks[pl.ds(r, c), :] for r in rows]
        qc = [qs[pl.ds(r, c), :] for r in rows]
        beta = [bs[pl.ds(r, c), :] for r in rows]
        kb = [k * b for k, b in zip(kc, beta)]
        aq = [_mm_nt(jnp.concatenate([b_, q_], axis=0).astype(BF16),
                     _per_head(k_.astype(BF16), block_mask))
              for b_, q_, k_ in zip(kb, qc, kc)]
        a_mat = [jnp.where(strict, x[:c] * d, 0.0) for x, d in zip(aq, decay)]
        qk = [x[c:] * d for x, d in zip(aq, decay)]
        p = [eye_cat - a for a in a_mat]
        yield
        pw = [mm_bd(a, a) for a in a_mat]
        for _ in range(4):
            yield
            both = [mm_bd(jnp.concatenate([p_, x], axis=0), x) for p_, x in zip(p, pw)]
            p = [p_ + b_[:c] for p_, b_ in zip(p, both)]
            pw = [b_[c:] for b_ in both]
        yield
        p = [p_ + mm_bd(p_, x) for p_, x in zip(p, pw)]
        yield
        for i, r in enumerate(rows):
            e_g = jnp.exp(g_cum[i])
            sl = slice(i * c, (i + 1) * c)
            ws[sl, :] = mm_bd(p[i], kb[i] * e_g).astype(BF16)
            us[sl, :] = mm_bd(p[i], vs[pl.ds(r, c), :] * beta[i])
            qks[sl, :] = qk[i].astype(BF16)
            qds[sl, :] = (qc[i] * e_g).astype(BF16)
            g_last = g_cum[i][c - 1:c, :]
            kds[sl, :] = (kc[i] * jnp.exp(g_last - g_cum[i])).astype(BF16)
            gls[i * SUBLANES:(i + 1) * SUBLANES, :] = jnp.broadcast_to(jnp.exp(g_last),
                                                                       (SUBLANES, DN_WIDTH))

    def scan_group(m, src):
        ws, us, qks, qds, kds, gls = src
        for i in range(DN_PREP_CHUNKS):
            sl = slice(i * c, (i + 1) * c)
            r = pl.multiple_of((DN_PREP_CHUNKS * m + i) * c, c)
            state = st[...]
            both = _mm(jnp.concatenate([ws[sl, :], qds[sl, :]], axis=0),
                       _per_head(state.astype(BF16), block_mask))
            yield
            v_new = (us[sl, :] - both[:c]).astype(BF16)
            os_[pl.ds(r, c), :] = both[c:] + _mm(qks[sl, :], _per_head(v_new, block_mask))
            kv = _mm_tn(kds[sl, :], v_new)
            upd = kv[(DN_HEADS - 1) * c:DN_HEADS * c, :]
            for h in range(DN_HEADS - 2, -1, -1):
                upd = jnp.where(hid == h, kv[h * c:(h + 1) * c, :], upd)
            yield
            st[...] = state * gls[i * SUBLANES:i * SUBLANES + 1, :] + upd

    alternate = _alternate

    st[...] = jnp.zeros(st.shape, F32)
    n_groups = seq // (DN_PREP_CHUNKS * c)
    assert n_groups % 2 == 0
    alternate(prep(0, set_a))

    def pair(j, carry):
        alternate(prep(2 * j + 1, set_b), scan_group(2 * j, set_a))
        alternate(prep(2 * j + 2, set_a), scan_group(2 * j + 1, set_b))
        return carry

    lax.fori_loop(0, n_groups // 2 - 1, pair, 0)
    alternate(prep(n_groups - 1, set_b), scan_group(n_groups - 2, set_a))
    alternate(scan_group(n_groups - 1, set_b))

    for t in range(n_tiles):
        r0 = t * DN_TILE
        o = os_[r0:r0 + DN_TILE, :]
        z = dn_ref[0, r0:r0 + DN_TILE, cw3:cw3 + DN_WIDTH].astype(F32)
        ms = _mm((o * o).astype(BF16), bd) * (1.0 / HEAD_DIM)
        y_ref[0, r0:r0 + DN_TILE, :] = (o * lax.rsqrt(ms + EPS) * onorm_ref[...] * _silu(z)).astype(BF16)


def _deltanet(dn, ab, conv_w, alog_cat, dtb_cat, onorm_cat, bd):
    b, seq, w = dn.shape
    f32buf = pltpu.VMEM((seq, DN_WIDTH), F32)
    group_rows = DN_PREP_CHUNKS * DN_CHUNK
    group_set = [pltpu.VMEM((group_rows, DN_WIDTH), BF16),
                 pltpu.VMEM((group_rows, DN_WIDTH), F32),
                 pltpu.VMEM((group_rows, DN_WIDTH), BF16),
                 pltpu.VMEM((group_rows, DN_WIDTH), BF16),
                 pltpu.VMEM((group_rows, DN_WIDTH), BF16),
                 pltpu.VMEM((DN_PREP_CHUNKS * SUBLANES, DN_WIDTH), F32)]
    return pl.pallas_call(
        _dn_kernel,
        grid=(b,),
        in_specs=[pl.BlockSpec((1, seq, w), lambda i: (i, 0, 0)),
                  pl.BlockSpec((1, seq, LANES), lambda i: (i, 0, 0))]
                 + [_full_spec(a) for a in (conv_w, alog_cat, dtb_cat, onorm_cat, bd)],
        out_specs=pl.BlockSpec((1, seq, DN_WIDTH), lambda i: (i, 0, 0)),
        out_shape=jax.ShapeDtypeStruct((b, seq, DN_WIDTH), BF16),
        scratch_shapes=[pltpu.VMEM((SUBLANES + seq, 3 * DN_WIDTH), F32),
                        f32buf, f32buf, f32buf, f32buf, f32buf,
                        f32buf,
                        pltpu.VMEM((DN_CHUNK, DN_WIDTH), F32)]
                       + group_set + group_set,
        compiler_params=pltpu.CompilerParams(dimension_semantics=("parallel",),
                                             vmem_limit_bytes=VMEM_LIMIT),
        name="deltanet",
    )(dn, ab, conv_w, alog_cat, dtb_cat, onorm_cat, bd)


def _run_starts(dil, r, i):
    if dil == 16:
        return [(tt * AT_TILE + r * 16, 16) for tt in range(AT_BLOCK * dil // AT_TILE)]
    if dil == 4:
        return [((2 * i + th) * AT_TILE + (4 * s + r) * 16, 16) for th in range(2) for s in range(4)]
    assert dil == 1
    return [((i // 2) * AT_TILE + rr * 16 + 8 * (i % 2), 8) for rr in range(16)]


def _run_order(dil):
    if dil == 16:
        return np.arange(AT_BLOCK)
    if dil == 4:
        th, s, ml = np.meshgrid(np.arange(2), np.arange(4), np.arange(16), indexing="ij")
        return (64 * th + 4 * ml + s).reshape(-1)
    rr, m8 = np.meshgrid(np.arange(16), np.arange(8), indexing="ij")
    return (16 * m8 + rr).reshape(-1)


def _load_runs(ref, p, runs):
    return jnp.concatenate([ref[p, pl.ds(pl.multiple_of(s, n), n), :] for s, n in runs], axis=0)


def _store_runs(ref, p, runs, val):
    off = 0
    for s, n in runs:
        ref[p, pl.ds(pl.multiple_of(s, n), n), :] = val[off:off + n]
        off += n


def _attn_kernel(q_ref, k_ref, v_ref, perm_ref, perm_t_ref, bias_ref, y_ref, qf, kf, vf, acc, ms, ls):
    seq = q_ref.shape[1]
    pairs = AT_HEADS // 2
    perm = perm_ref[...]
    for t in range(seq // AT_TILE):
        r0 = t * AT_TILE
        for src, dst in ((q_ref, qf), (k_ref, kf), (v_ref, vf)):
            rows = _mm(perm, src[0, r0:r0 + AT_TILE, :])
            for p in range(pairs):
                dst[p, r0:r0 + AT_TILE, :] = rows[:, p * LANES:(p + 1) * LANES]

    lane = lax.broadcasted_iota(jnp.int32, (1, LANES), 1)
    key_col = lax.broadcasted_iota(jnp.int32, (1, 2 * AT_BLOCK), 1)
    order = sorted(range(len(WINDOWS)), key=lambda g: -WINDOWS[g][1])
    for step, grp in enumerate(order):
        dil = WINDOWS[grp][1]
        nb = seq // dil // AT_BLOCK
        is_first = step == 0
        is_last = step == len(order) - 1

        def unit(u, dil=dil, nb=nb, grp=grp, is_first=is_first, is_last=is_last):
            r = u // nb
            i = u % nb
            q_runs = _run_starts(dil, r, i)
            p_runs = _run_starts(dil, r, jnp.maximum(i - 1, 0))
            no_prev = jnp.where(jnp.logical_and(i == 0, key_col < AT_BLOCK), NEG_INF, 0.0)
            def head_pair(p):
                q2 = _load_runs(qf, p, q_runs).astype(BF16)
                k2 = jnp.concatenate([_load_runs(kf, p, p_runs), _load_runs(kf, p, q_runs)],
                                     axis=0).astype(BF16)
                v2 = jnp.concatenate([_load_runs(vf, p, p_runs), _load_runs(vf, p, q_runs)],
                                     axis=0).astype(BF16)
                v_ext = jnp.concatenate([v2, jnp.ones(v2.shape, BF16)], axis=1)
                masks = [(lane // HEAD_DIM) == hh for hh in range(2)]
                scores = [_mm_nt(jnp.where(mask, q2, jnp.zeros_like(q2)), k2)
                          + bias_ref[grp, 2 * p + hh] + no_prev for hh, mask in enumerate(masks)]
                yield
                maxes = [jnp.max(s, axis=-1, keepdims=True) for s in scores]
                probs = [jnp.exp(s - m_h).astype(BF16) for s, m_h in zip(scores, maxes)]
                yield
                results = [_mm(pexp, v_ext) for pexp in probs]
                yield
                m_new = jnp.where(masks[1], maxes[1], jnp.broadcast_to(maxes[0], (AT_BLOCK, LANES)))
                o_new = jnp.where(masks[1], results[1][:, :LANES], results[0][:, :LANES])
                l_new = jnp.where(masks[1], results[1][:, LANES:], results[0][:, LANES:])
                if not is_first:
                    m_old = _load_runs(ms, p, q_runs)
                    m_tot = jnp.maximum(m_old, m_new)
                    a_old = jnp.exp(m_old - m_tot)
                    a_new = jnp.exp(m_new - m_tot)
                    l_new = a_old * _load_runs(ls, p, q_runs) + a_new * l_new
                    o_new = a_old * _load_runs(acc, p, q_runs) + a_new * o_new
                    m_new = m_tot
                if is_last:
                    _store_runs(acc, p, q_runs, o_new / l_new)
                else:
                    _store_runs(ms, p, q_runs, m_new)
                    _store_runs(ls, p, q_runs, l_new)
                    _store_runs(acc, p, q_runs, o_new)

            stages = [head_pair(p) for p in range(pairs)]
            for _ in range(3):
                for stage in stages:
                    next(stage)
                yield
            for stage in stages:
                next(stage, None)

        def units(j, carry, unit=unit):
            _alternate(*[unit(AT_UNITS * j + k) for k in range(AT_UNITS)])
            return carry

        lax.fori_loop(0, seq // AT_BLOCK // AT_UNITS, units, 0)

    perm_t = perm_t_ref[...]
    for t in range(seq // AT_TILE):
        r0 = t * AT_TILE
        for p in range(pairs):
            y_ref[0, r0:r0 + AT_TILE, p * LANES:(p + 1) * LANES] = _mm(
                perm_t, acc[p, r0:r0 + AT_TILE, :].astype(BF16)).astype(BF16)


def _tile_permutation():
    t = np.arange(AT_TILE)
    row = (t % MAX_DILATION) * (AT_TILE // MAX_DILATION) + t // MAX_DILATION
    perm = np.zeros((AT_TILE, AT_TILE), np.float32)
    perm[row, t] = 1.0
    return perm


def _attention(aq, ak, av, bias):
    b, seq, w = aq.shape
    assert seq == AT_BLOCK * MAX_DILATION and [d for _, d in WINDOWS] == [1, 4, 16]
    slab = pltpu.VMEM((AT_HEADS // 2, seq, LANES), F32)
    tok = pl.BlockSpec((1, seq, w), lambda i: (i, 0, 0))
    perm = _tile_permutation()
    perm_in = jnp.asarray(perm, BF16)
    perm_out = jnp.asarray(perm.T, BF16)
    return pl.pallas_call(
        _attn_kernel,
        grid=(b,),
        in_specs=[tok, tok, tok, _full_spec(perm_in), _full_spec(perm_out), _full_spec(bias)],
        out_specs=tok,
        out_shape=jax.ShapeDtypeStruct((b, seq, w), BF16),
        scratch_shapes=[slab] * 6,
        compiler_params=pltpu.CompilerParams(dimension_semantics=("parallel",),
                                             vmem_limit_bytes=VMEM_LIMIT),
        name="dilated_attention",
    )(aq, ak, av, perm_in, perm_out, bias)


def _t5_bucket(dist):
    max_exact = N_BUCKETS // 2
    d = np.maximum(dist, 1).astype(np.float32)
    log_bucket = max_exact + (np.log(d / np.float32(max_exact))
                              / np.float32(math.log(MAX_DISTANCE / max_exact))
                              * np.float32(N_BUCKETS - max_exact)).astype(np.int32)
    return np.where(dist < max_exact, dist, np.minimum(log_bucket, N_BUCKETS - 1))


def _bias_tables(rel_bias):
    tabs = []
    for window, dil in WINDOWS:
        n_back = window // dil
        j = _run_order(dil)
        rel = j[:, None] + AT_BLOCK - np.concatenate([j, AT_BLOCK + j])[None, :]
        valid = (rel >= 0) & (rel <= n_back)
        bucket = _t5_bucket(dil * np.clip(rel, 0, n_back)).reshape(-1)
        onehot = (np.arange(N_BUCKETS)[:, None] == bucket[None, :]).astype(np.float32)
        bias = jnp.dot(rel_bias.astype(F32).T, jnp.asarray(onehot, BF16).astype(F32), precision=HIGHEST)
        bias = bias.reshape(AT_HEADS, AT_BLOCK, 2 * AT_BLOCK)
        tabs.append(jnp.where(jnp.asarray(valid)[None], bias, NEG_INF))
    return jnp.stack(tabs)


def _cv_kernel(cu_ref, dw_ref, dwb_ref, g_ref, b_ref, y_ref, ypad):
    seq = cu_ref.shape[1]
    n_tiles = seq // SEQ_TILE
    pad = 4 * SUBLANES
    ypad[0:pad, :] = jnp.zeros((pad, CV_WIDTH), F32)
    for t in range(n_tiles):
        r0 = t * SEQ_TILE
        u = cu_ref[0, r0:r0 + SEQ_TILE, :].astype(F32)
        ypad[pad + r0:pad + r0 + SEQ_TILE, :] = u[:, :CV_WIDTH] * _sigmoid(u[:, CV_WIDTH:])
    for t in range(seq // CV_TILE):
        r0 = t * CV_TILE
        window = ypad[r0:r0 + pad + CV_TILE, :]
        acc = jnp.zeros((CV_TILE, CV_WIDTH), F32) + dwb_ref[...]
        for b in range(SUBLANES):
            rolled = pltpu.roll(window, b, axis=0) if b else window
            for j in range(CV_KERNEL):
                off = pad - (CV_KERNEL - 1) + j
                if (-off) % SUBLANES == b:
                    a8 = off + b
                    acc = acc + rolled[a8:a8 + CV_TILE, :] * dw_ref[j:j + 1, :]
        mu = jnp.mean(acc, axis=-1, keepdims=True)
        cen = acc - mu
        var = jnp.mean(cen * cen, axis=-1, keepdims=True)
        yn = cen * lax.rsqrt(var + EPS) * g_ref[...] + b_ref[...]
        y_ref[0, r0:r0 + CV_TILE, :] = _silu(yn).astype(BF16)


def _conformer_conv(cu, dw, dwb, ln_g, ln_b):
    b, seq, w = cu.shape
    return pl.pallas_call(
        _cv_kernel,
        grid=(b,),
        in_specs=[pl.BlockSpec((1, seq, w), lambda i: (i, 0, 0))]
                 + [_full_spec(a) for a in (dw, dwb, ln_g, ln_b)],
        out_specs=pl.BlockSpec((1, seq, CV_WIDTH), lambda i: (i, 0, 0)),
        out_shape=jax.ShapeDtypeStruct((b, seq, CV_WIDTH), BF16),
        scratch_shapes=[pltpu.VMEM((4 * SUBLANES + seq, CV_WIDTH), F32)],
        compiler_params=pltpu.CompilerParams(dimension_semantics=("parallel",),
                                             vmem_limit_bytes=VMEM_LIMIT),
        name="conformer_conv",
    )(cu, dw, dwb, ln_g, ln_b)


def _out_kernel(x_ref, ydn_ref, yat_ref, ycv_ref, wdn_ref, wat_ref, wcv_ref, g_ref, wr_ref, rb_ref,
                tri_ref, xo_ref, h_ref, route_ref, cnt_ref):
    lane = lax.broadcasted_iota(jnp.int32, (OUT_SLAB, LANES), 1)
    picks = {}

    def row_slab(r0):
        sl = slice(r0, r0 + OUT_SLAB)
        x = (x_ref[sl, :] + _mm(ydn_ref[sl, :], wdn_ref[...]) + _mm(yat_ref[sl, :], wat_ref[...])
             + _mm(ycv_ref[sl, :], wcv_ref[...]))
        xo_ref[sl, :] = x
        yield
        ms = jnp.mean(x * x, axis=-1, keepdims=True)
        h = (x * lax.rsqrt(ms + EPS) * g_ref[...]).astype(BF16)
        h_ref[sl, :] = h
        logits = _mm(h, wr_ref[...]) + rb_ref[...]
        yield
        is_group = (lane >= N_EXPERTS) & (lane < N_EXPERTS + N_GROUPS)
        gl = jnp.where(is_group, logits, NEG_INF)
        gmax = jnp.max(gl, axis=-1, keepdims=True)
        gsel = jnp.min(jnp.where(gl == gmax, lane, 2 * LANES), axis=-1, keepdims=True) - N_EXPERTS
        p_group = 1.0 / jnp.sum(jnp.where(is_group, jnp.exp(gl - gmax), 0.0), axis=-1, keepdims=True)
        yield
        lo = gsel * EXPERTS_PER_GROUP
        in_group = (lane >= lo) & (lane < lo + EXPERTS_PER_GROUP)
        el = jnp.where(in_group, logits, NEG_INF)
        v1 = jnp.max(el, axis=-1, keepdims=True)
        i1 = jnp.min(jnp.where(el == v1, lane, LANES), axis=-1, keepdims=True)
        yield
        el2 = jnp.where(lane == i1, NEG_INF, el)
        v2 = jnp.max(el2, axis=-1, keepdims=True)
        i2 = jnp.min(jnp.where(el2 == v2, lane, LANES), axis=-1, keepdims=True)
        t = jnp.exp(v2 - v1)
        g1 = p_group / (1.0 + t)
        picks[r0] = (i1, i2, g1, g1 * t)

    slabs = list(range(0, ROW_TILE, OUT_SLAB))
    _alternate(*[row_slab(r0) for r0 in slabs])

    both = jnp.concatenate(
        [jnp.where((lane == picks[r0][0]) | (lane == picks[r0][1]), 1.0, 0.0).astype(BF16)
         for r0 in slabs], axis=0)
    before = _mm(tri_ref[...], both)
    cnt_ref[0] = _mm(jnp.ones((SUBLANES, ROW_TILE), BF16), both)
    for r0 in slabs:
        i1, i2, g1, g2 = picks[r0]
        seen = before[r0:r0 + OUT_SLAB, :]
        rank1 = jnp.sum(jnp.where(lane == i1, seen, 0.0), axis=-1, keepdims=True)
        rank2 = jnp.sum(jnp.where(lane == i2, seen, 0.0), axis=-1, keepdims=True)
        route = jnp.where(lane == 0, i1.astype(F32), 0.0)
        route = jnp.where(lane == 1, i2.astype(F32), route)
        route = jnp.where(lane == 2, g1, route)
        route = jnp.where(lane == 3, g2, route)
        route = jnp.where(lane == 4, rank1, route)
        route = jnp.where(lane == 5, rank2, route)
        route_ref[r0:r0 + OUT_SLAB, :] = route


def _out_proj(x2, ydn, yat, ycv, wdn, wat, wcv, g, wr, rb, tri):
    n, d = x2.shape
    row = lambda w: pl.BlockSpec((ROW_TILE, w), lambda i: (i, 0))
    n_tiles = n // ROW_TILE
    return pl.pallas_call(
        _out_kernel,
        grid=(n_tiles,),
        in_specs=[row(d), row(DN_WIDTH), row(AT_WIDTH), row(CV_WIDTH)]
                 + [_full_spec(a) for a in (wdn, wat, wcv, g, wr, rb, tri)],
        out_specs=[row(d), row(d), row(LANES), pl.BlockSpec((1, SUBLANES, LANES), lambda i: (i, 0, 0))],
        out_shape=[jax.ShapeDtypeStruct((n, d), F32), jax.ShapeDtypeStruct((n, d), BF16),
                   jax.ShapeDtypeStruct((n, LANES), F32),
                   jax.ShapeDtypeStruct((n_tiles, SUBLANES, LANES), F32)],
        compiler_params=pltpu.CompilerParams(dimension_semantics=("parallel",),
                                             vmem_limit_bytes=VMEM_LIMIT),
        name="out_proj",
    )(x2, ydn, yat, ycv, wdn, wat, wcv, g, wr, rb, tri)


def _tile_slots(route, loc_row):
    lane = lax.broadcasted_iota(jnp.int32, route.shape, 1).astype(F32)
    slots = []
    for kk in range(TOP_K):
        base = jnp.sum(jnp.where(lane == route[:, kk:kk + 1], loc_row, 0.0), axis=-1, keepdims=True)
        slots.append(base + route[:, 4 + kk:5 + kk])
    return slots


def _chunk_cols(j):
    return (j * MOE_CHUNK + lax.broadcasted_iota(jnp.int32, (1, MOE_CHUNK), 1)).astype(F32)


def _piece(ref, row):
    return ref.at[pl.ds(pl.multiple_of(row, MOE_PIECE), MOE_PIECE), :]


def _dispatch_kernel(dst_ref, np_ref, zdst_ref, zvalid_ref, nu_ref, route_ref, loc_ref, h_ref, xs_ref,
                     srt, zeros, sem, zsem):
    tile = pl.program_id(0)
    n_pieces = np_ref[tile]
    n_blocks = xs_ref.shape[0] // MOE_BLOCK

    def zero_piece(z):
        return pltpu.make_async_copy(_piece(zeros, 0), _piece(xs_ref, zdst_ref[z]), zsem)

    def zero_block(b):
        return pltpu.make_async_copy(
            zeros, xs_ref.at[pl.ds(pl.multiple_of(b * MOE_BLOCK, MOE_BLOCK), MOE_BLOCK), :], zsem)

    @pl.when(tile == 0)
    def _():
        zeros[...] = jnp.zeros(zeros.shape, BF16)
        for wait in (False, True):
            def piece_body(z, carry, wait=wait):
                @pl.when(zvalid_ref[z] != 0)
                def _():
                    zero_piece(z).wait() if wait else zero_piece(z).start()
                return carry

            def block_body(b, carry, wait=wait):
                zero_block(b).wait() if wait else zero_block(b).start()
                return carry

            lax.fori_loop(0, N_EXPERTS * ZERO_PIECES, piece_body, 0)
            lax.fori_loop(nu_ref[0], n_blocks, block_body, 0)

    slot1, slot2 = _tile_slots(route_ref[...], loc_ref[0, 0:1, :])
    lane = lax.broadcasted_iota(jnp.int32, (1, LANES), 1)
    slot_cols = jnp.where(lane == 0, slot1, jnp.where(lane == 1, slot2, 0.0))
    pick = (lax.broadcasted_iota(jnp.int32, (SUBLANES, LANES), 0)
            == lax.broadcasted_iota(jnp.int32, (SUBLANES, LANES), 1)).astype(BF16)
    slot_rows = sum(_mm_nt(pick, piece) for piece in _split3(slot_cols))
    slot1_row = slot_rows[0:1, :]
    slot2_row = slot_rows[1:2, :]
    h = h_ref[...]
    buf = srt.at[tile % 2]

    def chunk(j, carry):
        row = (j * MOE_CHUNK + lax.broadcasted_iota(jnp.int32, (MOE_CHUNK, 1), 0)).astype(F32)
        onehot = jnp.where(row == slot1_row, 1.0, jnp.where(row == slot2_row, 1.0, 0.0)).astype(BF16)
        buf[pl.ds(pl.multiple_of(j * MOE_CHUNK, MOE_CHUNK), MOE_CHUNK), :] = _mm(onehot, h).astype(BF16)
        return carry

    pieces_per_chunk = MOE_CHUNK // MOE_PIECE
    lax.fori_loop(0, (n_pieces + pieces_per_chunk - 1) // pieces_per_chunk, chunk, 0)

    def piece_copy(t, p):
        return pltpu.make_async_copy(_piece(srt.at[t % 2], p * MOE_PIECE),
                                     _piece(xs_ref, dst_ref[t * PIECES_MAX + p]), sem.at[t % 2])

    def start(p, carry):
        piece_copy(tile, p).start()
        return carry

    lax.fori_loop(0, n_pieces, start, 0)

    def wait_tile(t):
        done = pltpu.make_async_copy(_piece(srt.at[t % 2], 0), _piece(xs_ref, 0), sem.at[t % 2])

        def wait(p, carry):
            done.wait()
            return carry
        lax.fori_loop(0, np_ref[t], wait, 0)

    @pl.when(tile > 0)
    def _():
        wait_tile(tile - 1)

    @pl.when(tile == pl.num_programs(0) - 1)
    def _():
        wait_tile(tile)


def _dispatch(plan, route, h, n_slots):
    n, d = h.shape
    n_tiles = n // MOE_TILE
    grid_spec = pltpu.PrefetchScalarGridSpec(
        num_scalar_prefetch=5,
        grid=(n_tiles,),
        in_specs=[pl.BlockSpec((MOE_TILE, LANES), lambda i, *_: (i, 0)),
                  pl.BlockSpec((1, SUBLANES, LANES), lambda i, *_: (i, 0, 0)),
                  pl.BlockSpec((MOE_TILE, d), lambda i, *_: (i, 0))],
        out_specs=pl.BlockSpec(memory_space=pl.ANY),
        scratch_shapes=[pltpu.VMEM((2, TILE_SLOTS, d), BF16), pltpu.VMEM((MOE_BLOCK, d), BF16),
                        pltpu.SemaphoreType.DMA((2,)), pltpu.SemaphoreType.DMA(())],
    )
    return pl.pallas_call(
        _dispatch_kernel,
        grid_spec=grid_spec,
        out_shape=jax.ShapeDtypeStruct((n_slots, d), BF16),
        compiler_params=pltpu.CompilerParams(dimension_semantics=("arbitrary",),
                                             vmem_limit_bytes=VMEM_LIMIT),
        name="moe_dispatch",
    )(plan["dst"], plan["n_pieces"], plan["zdst"], plan["zvalid"], plan["n_used"], route,
      plan["loc"], h)


def _expert_kernel(be_ref, nu_ref, x_ref, wg_ref, wu_ref, wd_ref, y_ref, wg_bf, wu_bf, wd_bf):
    i = pl.program_id(0)
    changed = jnp.logical_or(i == 0, be_ref[i] != be_ref[jnp.maximum(i - 1, 0)])

    @pl.when(jnp.logical_and(changed, i < nu_ref[0]))
    def _():
        wg_bf[...] = wg_ref[0].astype(BF16)
        wu_bf[...] = wu_ref[0].astype(BF16)
        wd_bf[...] = wd_ref[0].astype(BF16)

    @pl.when(i < nu_ref[0])
    def _():
        x = x_ref[...]
        g = _mm(x, wg_bf[...])
        u = _mm(x, wu_bf[...])
        y_ref[...] = _mm((_silu(g) * u).astype(BF16), wd_bf[...]).astype(BF16)

    @pl.when(i >= nu_ref[0])
    def _():
        y_ref[...] = jnp.zeros(y_ref.shape, BF16)


def _experts(plan, xs, wg, wu, wd, layer):
    ns, d = xs.shape
    de = wg.shape[3]
    rows = lambda i, be, nu: (jnp.minimum(i, nu[0] - 1), 0)
    grid_spec = pltpu.PrefetchScalarGridSpec(
        num_scalar_prefetch=2,
        grid=(ns // MOE_BLOCK,),
        in_specs=[pl.BlockSpec((MOE_BLOCK, d), rows),
                  pl.BlockSpec((None, 1, d, de), lambda i, be, nu: (layer, be[i], 0, 0)),
                  pl.BlockSpec((None, 1, d, de), lambda i, be, nu: (layer, be[i], 0, 0)),
                  pl.BlockSpec((None, 1, de, d), lambda i, be, nu: (layer, be[i], 0, 0))],
        out_specs=pl.BlockSpec((MOE_BLOCK, d), lambda i, be, nu: (i, 0)),
        scratch_shapes=[pltpu.VMEM((d, de), BF16), pltpu.VMEM((d, de), BF16),
                        pltpu.VMEM((de, d), BF16)],
    )
    return pl.pallas_call(
        _expert_kernel,
        grid_spec=grid_spec,
        out_shape=jax.ShapeDtypeStruct((ns, d), BF16),
        compiler_params=pltpu.CompilerParams(dimension_semantics=("arbitrary",),
                                             vmem_limit_bytes=VMEM_LIMIT),
        name="moe_experts",
    )(plan["block_expert"], plan["n_used"], xs, wg, wu, wd)


def _combine_kernel(dst_ref, np_ref, route_ref, loc_ref, x_ref, y_ref, o_ref, ysrt, sem):
    tile = pl.program_id(0)
    n_pieces = np_ref[tile]

    @pl.when(tile == 0)
    def _():
        ysrt[...] = jnp.zeros(ysrt.shape, BF16)

    def piece_copy(t, p):
        return pltpu.make_async_copy(_piece(y_ref, dst_ref[t * PIECES_MAX + p]),
                                     _piece(ysrt.at[t % 2], p * MOE_PIECE), sem.at[t % 2])

    def fetch_tile(t):
        def start(p, carry):
            piece_copy(t, p).start()
            return carry
        lax.fori_loop(0, np_ref[t], start, 0)

    @pl.when(tile == 0)
    def _():
        fetch_tile(tile)

    @pl.when(tile + 1 < pl.num_programs(0))
    def _():
        fetch_tile(tile + 1)

    route = route_ref[...]
    slot1, slot2 = _tile_slots(route, loc_ref[0, 0:1, :])
    g1 = route[:, 2:3]
    g2 = route[:, 3:4]
    gates = jnp.concatenate(
        [jnp.where(_chunk_cols(j) == slot1, g1, jnp.where(_chunk_cols(j) == slot2, g2, 0.0)).astype(BF16)
         for j in range(TILE_SLOTS // MOE_CHUNK)], axis=1)

    done = pltpu.make_async_copy(_piece(y_ref, 0), _piece(ysrt.at[tile % 2], 0), sem.at[tile % 2])

    def wait(p, carry):
        done.wait()
        return carry

    lax.fori_loop(0, n_pieces, wait, 0)
    o_ref[...] = x_ref[...] + _mm(gates, ysrt[tile % 2])


def _combine(plan, route, x2, y):
    n, d = x2.shape
    grid_spec = pltpu.PrefetchScalarGridSpec(
        num_scalar_prefetch=2,
        grid=(n // MOE_TILE,),
        in_specs=[pl.BlockSpec((MOE_TILE, LANES), lambda i, *_: (i, 0)),
                  pl.BlockSpec((1, SUBLANES, LANES), lambda i, *_: (i, 0, 0)),
                  pl.BlockSpec((MOE_TILE, d), lambda i, *_: (i, 0)),
                  pl.BlockSpec(memory_space=pl.ANY)],
        out_specs=pl.BlockSpec((MOE_TILE, d), lambda i, *_: (i, 0)),
        scratch_shapes=[pltpu.VMEM((2, TILE_SLOTS, d), BF16), pltpu.SemaphoreType.DMA((2,))],
    )
    return pl.pallas_call(
        _combine_kernel,
        grid_spec=grid_spec,
        out_shape=jax.ShapeDtypeStruct((n, d), F32),
        compiler_params=pltpu.CompilerParams(dimension_semantics=("arbitrary",),
                                             vmem_limit_bytes=VMEM_LIMIT),
        name="moe_combine",
    )(plan["dst"], plan["n_pieces"], route, plan["loc"], x2, y)


def _round_up(v, m):
    return (v + m - 1) // m * m


def _moe_plan(counts, n_slots):
    n_tiles = counts.shape[0]
    cnt = counts[:, 0, :N_EXPERTS].astype(jnp.int32)
    seg = _round_up(cnt, MOE_PIECE)
    loc_end = jnp.cumsum(seg, axis=1)
    loc_start = loc_end - seg
    totals = jnp.sum(seg, axis=0)
    padded = _round_up(totals, MOE_BLOCK)
    pad_end = jnp.cumsum(padded)
    pad_start = pad_end - padded
    seg_start = pad_start[None, :] + jnp.cumsum(seg, axis=0) - seg
    piece_off = jnp.arange(PIECES_MAX, dtype=jnp.int32) * MOE_PIECE
    piece_e = jnp.sum(loc_end[:, None, :] <= piece_off[None, :, None], axis=2)
    hit = piece_e[..., None] == jnp.arange(N_EXPERTS)
    shift = jnp.sum(jnp.where(hit, (seg_start - loc_start)[:, None, :], 0), axis=2)
    valid = piece_off[None, :] < loc_end[:, -1:]
    dst = jnp.where(valid, shift + piece_off[None, :], 0).astype(jnp.int32).reshape(-1)
    zk = jnp.arange(ZERO_PIECES, dtype=jnp.int32)[None, :] * MOE_PIECE
    zvalid = zk < (padded - totals)[:, None]
    zdst = jnp.where(zvalid, (pad_start + totals)[:, None] + zk, 0)
    blk_start = jnp.arange(n_slots // MOE_BLOCK, dtype=jnp.int32) * MOE_BLOCK
    block_expert = jnp.minimum(jnp.sum(pad_end[None, :] <= blk_start[:, None], axis=1), N_EXPERTS - 1)
    loc = jnp.pad(loc_start.astype(F32), ((0, 0), (0, LANES - N_EXPERTS)))
    return {
        "dst": dst,
        "n_pieces": (loc_end[:, -1] // MOE_PIECE).astype(jnp.int32),
        "zdst": zdst.astype(jnp.int32).reshape(-1),
        "zvalid": zvalid.astype(jnp.int32).reshape(-1),
        "loc": jnp.broadcast_to(loc[:, None, :], (n_tiles, SUBLANES, LANES)),
        "block_expert": block_expert.astype(jnp.int32),
        "n_used": (pad_end[-1:] // MOE_BLOCK).astype(jnp.int32),
    }


def _pad_lanes(a, width=LANES):
    return jnp.pad(a, [(0, 0)] * (a.ndim - 1) + [(0, width - a.shape[-1])])


def kernel(x, norm_mix, w_in, dn_conv, dn_a_log, dn_dt_bias, dn_out_norm, at_q_norm, at_k_norm,
           rel_bias, cv_dw, cv_dw_bias, cv_ln_g, cv_ln_b, w_out, norm_ffn, router_group_w,
           router_group_b, router_expert_w, router_expert_b, ex_gate, ex_up, ex_down):
    bsz, seq, d = x.shape
    n_tok = bsz * seq
    depth = w_in.shape[0]
    c_ab = 4 * DN_WIDTH
    c_at = c_ab + 2 * DN_HEADS
    c_cv = c_at + 3 * AT_WIDTH
    bd_at = _block_diag_ones(AT_WIDTH, HEAD_DIM, BF16)
    bd_dn = _block_diag_ones(DN_WIDTH, HEAD_DIM, BF16)
    bias = _bias_tables(rel_bias)
    tri = (jnp.arange(ROW_TILE)[:, None] > jnp.arange(ROW_TILE)[None, :]).astype(BF16)
    per_head_lanes = lambda v: jnp.repeat(v, HEAD_DIM)[None, :]
    n_tiles = n_tok // MOE_TILE
    n_slots = _round_up(TOP_K * n_tok + n_tiles * N_EXPERTS * (MOE_PIECE - 1)
                        + N_EXPERTS * (MOE_BLOCK - 1), MOE_BLOCK)

    x2 = x.reshape(n_tok, d)
    for layer in range(depth):
        w_l = w_in[layer]
        dn, aq, ak, av, cu, ab = _proj(
            x2, norm_mix[layer][None, :],
            w_l[:, :c_ab].astype(BF16), w_l[:, c_at:c_cv].astype(BF16), w_l[:, c_cv:].astype(BF16),
            _pad_lanes(w_l[:, c_ab:c_at]).astype(BF16), bd_at,
            jnp.tile(at_q_norm[layer], AT_HEADS)[None, :] * (HEAD_DIM ** -0.5),
            jnp.tile(at_k_norm[layer], AT_HEADS)[None, :])

        y_dn = _deltanet(dn.reshape(bsz, seq, -1), ab.reshape(bsz, seq, LANES), dn_conv[layer],
                         per_head_lanes(dn_a_log[layer]), per_head_lanes(dn_dt_bias[layer]),
                         jnp.tile(dn_out_norm[layer], DN_HEADS)[None, :], bd_dn)
        y_at = _attention(aq.reshape(bsz, seq, -1), ak.reshape(bsz, seq, -1),
                          av.reshape(bsz, seq, -1), bias)
        y_cv = _conformer_conv(cu.reshape(bsz, seq, -1), cv_dw[layer], cv_dw_bias[layer][None, :],
                               cv_ln_g[layer][None, :], cv_ln_b[layer][None, :])

        wo = w_out[layer].astype(BF16)
        w_r = _pad_lanes(jnp.concatenate([router_expert_w[layer], router_group_w[layer]], axis=1))
        b_r = _pad_lanes(jnp.concatenate([router_expert_b[layer], router_group_b[layer]])[None, :])
        x_mid, h_ffn, route, counts = _out_proj(
            x2, y_dn.reshape(n_tok, DN_WIDTH), y_at.reshape(n_tok, AT_WIDTH),
            y_cv.reshape(n_tok, CV_WIDTH), wo[:DN_WIDTH], wo[DN_WIDTH:DN_WIDTH + AT_WIDTH],
            wo[DN_WIDTH + AT_WIDTH:], norm_ffn[layer][None, :], w_r.astype(BF16), b_r, tri)

        plan = _moe_plan(counts, n_slots)
        xs = _dispatch(plan, route, h_ffn, n_slots)
        y = _experts(plan, xs, ex_gate, ex_up, ex_down, layer)
        x2 = _combine(plan, route, x_mid, y)
    return x2.reshape(bsz, seq, d)
```

```python
import math

import jax
import jax.numpy as jnp
import numpy as np
from jax import lax
from jax.experimental import pallas as pl
from jax.experimental.pallas import tpu as pltpu

F32 = jnp.float32
BF16 = jnp.bfloat16
HIGHEST = lax.Precision.HIGHEST

EPS = 1e-6
NEG_INF = -1e30

HEAD_DIM = 64
DN_HEADS = 4
DN_WIDTH = DN_HEADS * HEAD_DIM
DN_CONV = 4
DN_CHUNK = 64
AT_HEADS = 8
AT_WIDTH = AT_HEADS * HEAD_DIM
AT_BLOCK = 128
WINDOWS = ((128, 1), (512, 4), (2048, 16))
MAX_DILATION = 16
AT_TILE = 256
AT_UNITS = 4
N_BUCKETS = 32
MAX_DISTANCE = 2048
CV_WIDTH = 256
CV_KERNEL = 31
N_GROUPS = 4
EXPERTS_PER_GROUP = 8
N_EXPERTS = N_GROUPS * EXPERTS_PER_GROUP
TOP_K = 2

LANES = 128
SUBLANES = 8
VMEM_LIMIT = 52 * 1024 * 1024

ROW_TILE = 512
SEQ_TILE = 256
CV_TILE = 64
OUT_SLAB = 256
DN_TILE = 64
DN_PREP_CHUNKS = 4
MOE_BLOCK = 512
MOE_TILE = ROW_TILE
MOE_PIECE = 16
MOE_CHUNK = 256
TILE_SLOTS = -(-(TOP_K * MOE_TILE + N_EXPERTS * (MOE_PIECE - 1)) // MOE_CHUNK) * MOE_CHUNK
PIECES_MAX = TILE_SLOTS // MOE_PIECE
ZERO_PIECES = MOE_BLOCK // MOE_PIECE - 1


def _mm(a, b, precision=None):
    return jnp.dot(a, b, preferred_element_type=F32, precision=precision)


def _mm_nt(a, b):
    return lax.dot_general(a, b, (((1,), (1,)), ((), ())), preferred_element_type=F32)


def _mm_tn(a, b):
    return lax.dot_general(a, b, (((0,), (0,)), ((), ())), preferred_element_type=F32)


def _sigmoid(x):
    return 0.5 * jnp.tanh(0.5 * x) + 0.5


def _silu(x):
    return x * _sigmoid(x)


def _split3(x):
    p0 = x.astype(BF16)
    r1 = x - p0.astype(F32)
    p1 = r1.astype(BF16)
    p2 = (r1 - p1.astype(F32)).astype(BF16)
    return p0, p1, p2


def _alternate(*stages):
    live = list(stages)
    while live:
        for gen in list(live):
            if next(gen, live) is live:
                live.remove(gen)


def _full_spec(a):
    nd = a.ndim
    return pl.BlockSpec(a.shape, lambda *_: (0,) * nd)


def _block_diag_ones(width, block, dtype):
    r = jnp.arange(width)[:, None] // block
    c = jnp.arange(width)[None, :] // block
    return (r == c).astype(dtype)


def _proj_kernel(x_ref, g_ref, wdn_ref, wat_ref, wcv_ref, wab_ref, bd_ref, qn_ref, kn_ref,
                 dn_ref, aq_ref, ak_ref, av_ref, cv_ref, ab_ref):
    x = x_ref[...]
    ms = jnp.mean(x * x, axis=-1, keepdims=True)
    h = (x * lax.rsqrt(ms + EPS) * g_ref[...]).astype(BF16)
    dn_ref[...] = _mm(h, wdn_ref[...]).astype(BF16)
    cv_ref[...] = _mm(h, wcv_ref[...]).astype(BF16)
    ab_ref[...] = _mm(h, wab_ref[...])
    at = _mm(h, wat_ref[...])
    q = at[:, 0:AT_WIDTH]
    k = at[:, AT_WIDTH:2 * AT_WIDTH]
    bd = bd_ref[...]
    qms = _mm((q * q).astype(BF16), bd) * (1.0 / HEAD_DIM)
    kms = _mm((k * k).astype(BF16), bd) * (1.0 / HEAD_DIM)
    aq_ref[...] = (q * lax.rsqrt(qms + EPS) * qn_ref[...]).astype(BF16)
    ak_ref[...] = (k * lax.rsqrt(kms + EPS) * kn_ref[...]).astype(BF16)
    av_ref[...] = at[:, 2 * AT_WIDTH:3 * AT_WIDTH].astype(BF16)


def _proj(x2, g, wdn, wat, wcv, wab, bd, qn, kn):
    n, d = x2.shape
    row = lambda w: pl.BlockSpec((ROW_TILE, w), lambda i: (i, 0))
    widths = (wdn.shape[1], AT_WIDTH, AT_WIDTH, AT_WIDTH, wcv.shape[1], LANES)
    dtypes = (BF16, BF16, BF16, BF16, BF16, F32)
    return pl.pallas_call(
        _proj_kernel,
        grid=(n // ROW_TILE,),
        in_specs=[row(d)] + [_full_spec(a) for a in (g, wdn, wat, wcv, wab, bd, qn, kn)],
        out_specs=[row(w) for w in widths],
        out_shape=[jax.ShapeDtypeStruct((n, w), t) for w, t in zip(widths, dtypes)],
        compiler_params=pltpu.CompilerParams(dimension_semantics=("parallel",),
                                             vmem_limit_bytes=VMEM_LIMIT),
        name="proj",
    )(x2, g, wdn, wat, wcv, wab, bd, qn, kn)


def _per_head(x, block_mask):
    return jnp.where(block_mask, jnp.concatenate([x] * DN_HEADS, axis=0), jnp.zeros((), x.dtype))


def _dn_kernel(dn_ref, ab_ref, cw_ref, alog_ref, dtb_ref, onorm_ref, bd_ref, y_ref,
               xpad, qs, ks, vs, gs, bs, os_, st, *group_bufs):
    set_a, set_b = group_bufs[:6], group_bufs[6:]
    seq = dn_ref.shape[1]
    n_tiles = seq // DN_TILE
    cw3 = 3 * DN_WIDTH
    pad = SUBLANES
    c = DN_CHUNK
    bd = bd_ref[...]
    hid = lax.broadcasted_iota(jnp.int32, (1, DN_WIDTH), 1) // HEAD_DIM

    def expand(cols, first):
        out = cols[:, first + DN_HEADS - 1:first + DN_HEADS]
        for h in range(DN_HEADS - 2, -1, -1):
            out = jnp.where(hid == h, cols[:, first + h:first + h + 1], out)
        return out

    xpad[0:pad, :] = jnp.zeros((pad, cw3), F32)
    for t in range(n_tiles):
        r0 = t * DN_TILE
        xpad[pad + r0:pad + r0 + DN_TILE, :] = dn_ref[0, r0:r0 + DN_TILE, 0:cw3].astype(F32)
    for t in range(n_tiles):
        r0 = t * DN_TILE
        ab = ab_ref[0, r0:r0 + DN_TILE, :]
        sp_in = expand(ab, 0) + dtb_ref[...]
        softplus = jnp.maximum(sp_in, 0.0) + jnp.log(1.0 + jnp.exp(-jnp.abs(sp_in)))
        gs[r0:r0 + DN_TILE, :] = -jnp.exp(alog_ref[...]) * softplus
        bs[r0:r0 + DN_TILE, :] = _sigmoid(expand(ab, DN_HEADS))
        acc = jnp.zeros((DN_TILE, cw3), F32)
        for j in range(DN_CONV):
            off = pad + r0 - (DN_CONV - 1) + j
            acc = acc + xpad[off:off + DN_TILE, :] * cw_ref[j:j + 1, :]
        y = _silu(acc)
        q = y[:, 0:DN_WIDTH]
        k = y[:, DN_WIDTH:2 * DN_WIDTH]
        qss = _mm((q * q).astype(BF16), bd)
        kss = _mm((k * k).astype(BF16), bd)
        qs[r0:r0 + DN_TILE, :] = q * lax.rsqrt(qss + EPS) * (HEAD_DIM ** -0.5)
        ks[r0:r0 + DN_TILE, :] = k * lax.rsqrt(kss + EPS)
        vs[r0:r0 + DN_TILE, :] = y[:, 2 * DN_WIDTH:3 * DN_WIDTH]

    ri = lax.broadcasted_iota(jnp.int32, (c, DN_WIDTH), 0)
    ci = lax.broadcasted_iota(jnp.int32, (c, DN_WIDTH), 1) % HEAD_DIM
    causal = ri >= ci
    strict = ri > ci
    eye_cat = (ri == ci).astype(F32)
    r2 = lax.broadcasted_iota(jnp.int32, (c, c), 0)
    c2 = lax.broadcasted_iota(jnp.int32, (c, c), 1)
    lower_ones = (r2 >= c2).astype(BF16)
    all_ones = jnp.ones((c, c), BF16)
    block_mask = (lax.broadcasted_iota(jnp.int32, (DN_WIDTH, DN_WIDTH), 0) // HEAD_DIM
                  == lax.broadcasted_iota(jnp.int32, (DN_WIDTH, DN_WIDTH), 1) // HEAD_DIM)

    def mm_exact_rhs(lhs_bf, x):
        p0, p1, p2 = _split3(x)
        return _mm(lhs_bf, p0) + _mm(lhs_bf, p1) + _mm(lhs_bf, p2)

    def mm_bd(lhs, rhs_cat):
        return _mm(lhs.astype(BF16), _per_head(rhs_cat.astype(BF16), block_mask))

    def prep(m, dst):
        ws, us, qks, qds, kds, gls = dst
        rows = [pl.multiple_of((DN_PREP_CHUNKS * m + cc) * c, c) for cc in range(DN_PREP_CHUNKS)]
        g_cum = [mm_exact_rhs(lower_ones, gs[pl.ds(r, c), :]) for r in rows]
        yield
        g_row = [mm_exact_rhs(all_ones, g * eye_cat) for g in g_cum]
        yield
        decay = [jnp.exp(jnp.where(causal, g - gr, NEG_INF)) for g, gr in zip(g_cum, g_row)]
        kc = [ks[pl.ds(r, c), :] for r in rows]
        qc = [qs[pl.ds(r, c), :] for r in rows]
        beta = [bs[pl.ds(r, c), :] for r in rows]
        kb = [k * b for k, b in zip(kc, beta)]
        aq = [_mm_nt(jnp.concatenate([b_, q_], axis=0).astype(BF16),
                     _per_head(k_.astype(BF16), block_mask))
              for b_, q_, k_ in zip(kb, qc, kc)]
        a_mat = [jnp.where(strict, x[:c] * d, 0.0) for x, d in zip(aq, decay)]
        qk = [x[c:] * d for x, d in zip(aq, decay)]
        p = [eye_cat - a for a in a_mat]
        yield
        pw = [mm_bd(a, a) for a in a_mat]
        for _ in range(4):
            yield
            both = [mm_bd(jnp.concatenate([p_, x], axis=0), x) for p_, x in zip(p, pw)]
            p = [p_ + b_[:c] for p_, b_ in zip(p, both)]
            pw = [b_[c:] for b_ in both]
        yield
        p = [p_ + mm_bd(p_, x) for p_, x in zip(p, pw)]
        yield
        for i, r in enumerate(rows):
            e_g = jnp.exp(g_cum[i])
            sl = slice(i * c, (i + 1) * c)
            ws[sl, :] = mm_bd(p[i], kb[i] * e_g).astype(BF16)
            us[sl, :] = mm_bd(p[i], vs[pl.ds(r, c), :] * beta[i])
            qks[sl, :] = qk[i].astype(BF16)
            qds[sl, :] = (qc[i] * e_g).astype(BF16)
            g_last = g_cum[i][c - 1:c, :]
            kds[sl, :] = (kc[i] * jnp.exp(g_last - g_cum[i])).astype(BF16)
            gls[i * SUBLANES:(i + 1) * SUBLANES, :] = jnp.broadcast_to(jnp.exp(g_last),
                                                                       (SUBLANES, DN_WIDTH))

    def scan_group(m, src):
        ws, us, qks, qds, kds, gls = src
        for i in range(DN_PREP_CHUNKS):
            sl = slice(i * c, (i + 1) * c)
            r = pl.multiple_of((DN_PREP_CHUNKS * m + i) * c, c)
            state = st[...]
            both = _mm(jnp.concatenate([ws[sl, :], qds[sl, :]], axis=0),
                       _per_head(state.astype(BF16), block_mask))
            yield
            v_new = (us[sl, :] - both[:c]).astype(BF16)
            os_[pl.ds(r, c), :] = both[c:] + _mm(qks[sl, :], _per_head(v_new, block_mask))
            kv = _mm_tn(kds[sl, :], v_new)
            upd = kv[(DN_HEADS - 1) * c:DN_HEADS * c, :]
            for h in range(DN_HEADS - 2, -1, -1):
                upd = jnp.where(hid == h, kv[h * c:(h + 1) * c, :], upd)
            yield
            st[...] = state * gls[i * SUBLANES:i * SUBLANES + 1, :] + upd

    alternate = _alternate

    st[...] = jnp.zeros(st.shape, F32)
    n_groups = seq // (DN_PREP_CHUNKS * c)
    assert n_groups % 2 == 0
    alternate(prep(0, set_a))

    def pair(j, carry):
        alternate(prep(2 * j + 1, set_b), scan_group(2 * j, set_a))
        alternate(prep(2 * j + 2, set_a), scan_group(2 * j + 1, set_b))
        return carry

    lax.fori_loop(0, n_groups // 2 - 1, pair, 0)
    alternate(prep(n_groups - 1, set_b), scan_group(n_groups - 2, set_a))
    alternate(scan_group(n_groups - 1, set_b))

    for t in range(n_tiles):
        r0 = t * DN_TILE
        o = os_[r0:r0 + DN_TILE, :]
        z = dn_ref[0, r0:r0 + DN_TILE, cw3:cw3 + DN_WIDTH].astype(F32)
        ms = _mm((o * o).astype(BF16), bd) * (1.0 / HEAD_DIM)
        y_ref[0, r0:r0 + DN_TILE, :] = (o * lax.rsqrt(ms + EPS) * onorm_ref[...] * _silu(z)).astype(BF16)


def _deltanet(dn, ab, conv_w, alog_cat, dtb_cat, onorm_cat, bd):
    b, seq, w = dn.shape
    f32buf = pltpu.VMEM((seq, DN_WIDTH), F32)
    group_rows = DN_PREP_CHUNKS * DN_CHUNK
    group_set = [pltpu.VMEM((group_rows, DN_WIDTH), BF16),
                 pltpu.VMEM((group_rows, DN_WIDTH), F32),
                 pltpu.VMEM((group_rows, DN_WIDTH), BF16),
                 pltpu.VMEM((group_rows, DN_WIDTH), BF16),
                 pltpu.VMEM((group_rows, DN_WIDTH), BF16),
                 pltpu.VMEM((DN_PREP_CHUNKS * SUBLANES, DN_WIDTH), F32)]
    return pl.pallas_call(
        _dn_kernel,
        grid=(b,),
        in_specs=[pl.BlockSpec((1, seq, w), lambda i: (i, 0, 0)),
                  pl.BlockSpec((1, seq, LANES), lambda i: (i, 0, 0))]
                 + [_full_spec(a) for a in (conv_w, alog_cat, dtb_cat, onorm_cat, bd)],
        out_specs=pl.BlockSpec((1, seq, DN_WIDTH), lambda i: (i, 0, 0)),
        out_shape=jax.ShapeDtypeStruct((b, seq, DN_WIDTH), BF16),
        scratch_shapes=[pltpu.VMEM((SUBLANES + seq, 3 * DN_WIDTH), F32),
                        f32buf, f32buf, f32buf, f32buf, f32buf,
                        f32buf,
                        pltpu.VMEM((DN_CHUNK, DN_WIDTH), F32)]
                       + group_set + group_set,
        compiler_params=pltpu.CompilerParams(dimension_semantics=("parallel",),
                                             vmem_limit_bytes=VMEM_LIMIT),
        name="deltanet",
    )(dn, ab, conv_w, alog_cat, dtb_cat, onorm_cat, bd)


def _run_starts(dil, r, i):
    if dil == 16:
        return [(tt * AT_TILE + r * 16, 16) for tt in range(AT_BLOCK * dil // AT_TILE)]
    if dil == 4:
        return [((2 * i + th) * AT_TILE + (4 * s + r) * 16, 16) for th in range(2) for s in range(4)]
    assert dil == 1
    return [((i // 2) * AT_TILE + rr * 16 + 8 * (i % 2), 8) for rr in range(16)]


def _run_order(dil):
    if dil == 16:
        return np.arange(AT_BLOCK)
    if dil == 4:
        th, s, ml = np.meshgrid(np.arange(2), np.arange(4), np.arange(16), indexing="ij")
        return (64 * th + 4 * ml + s).reshape(-1)
    rr, m8 = np.meshgrid(np.arange(16), np.arange(8), indexing="ij")
    return (16 * m8 + rr).reshape(-1)


def _load_runs(ref, p, runs):
    return jnp.concatenate([ref[p, pl.ds(pl.multiple_of(s, n), n), :] for s, n in runs], axis=0)


def _store_runs(ref, p, runs, val):
    off = 0
    for s, n in runs:
        ref[p, pl.ds(pl.multiple_of(s, n), n), :] = val[off:off + n]
        off += n


def _attn_kernel(q_ref, k_ref, v_ref, perm_ref, perm_t_ref, bias_ref, y_ref, qf, kf, vf, acc, ms, ls):
    seq = q_ref.shape[1]
    pairs = AT_HEADS // 2
    perm = perm_ref[...]
    for t in range(seq // AT_TILE):
        r0 = t * AT_TILE
        for src, dst in ((q_ref, qf), (k_ref, kf), (v_ref, vf)):
            rows = _mm(perm, src[0, r0:r0 + AT_TILE, :])
            for p in range(pairs):
                dst[p, r0:r0 + AT_TILE, :] = rows[:, p * LANES:(p + 1) * LANES]

    lane = lax.broadcasted_iota(jnp.int32, (1, LANES), 1)
    key_col = lax.broadcasted_iota(jnp.int32, (1, 2 * AT_BLOCK), 1)
    order = sorted(range(len(WINDOWS)), key=lambda g: -WINDOWS[g][1])
    for step, grp in enumerate(order):
        dil = WINDOWS[grp][1]
        nb = seq // dil // AT_BLOCK
        is_first = step == 0
        is_last = step == len(order) - 1

        def unit(u, dil=dil, nb=nb, grp=grp, is_first=is_first, is_last=is_last):
            r = u // nb
            i = u % nb
            q_runs = _run_starts(dil, r, i)
            p_runs = _run_starts(dil, r, jnp.maximum(i - 1, 0))
            no_prev = jnp.where(jnp.logical_and(i == 0, key_col < AT_BLOCK), NEG_INF, 0.0)
            def head_pair(p):
                q2 = _load_runs(qf, p, q_runs).astype(BF16)
                k2 = jnp.concatenate([_load_runs(kf, p, p_runs), _load_runs(kf, p, q_runs)],
                                     axis=0).astype(BF16)
                v2 = jnp.concatenate([_load_runs(vf, p, p_runs), _load_runs(vf, p, q_runs)],
                                     axis=0).astype(BF16)
                v_ext = jnp.concatenate([v2, jnp.ones(v2.shape, BF16)], axis=1)
                masks = [(lane // HEAD_DIM) == hh for hh in range(2)]
                scores = [_mm_nt(jnp.where(mask, q2, jnp.zeros_like(q2)), k2)
                          + bias_ref[grp, 2 * p + hh] + no_prev for hh, mask in enumerate(masks)]
                yield
                maxes = [jnp.max(s, axis=-1, keepdims=True) for s in scores]
                probs = [jnp.exp(s - m_h).astype(BF16) for s, m_h in zip(scores, maxes)]
                yield
                results = [_mm(pexp, v_ext) for pexp in probs]
                yield
                m_new = jnp.where(masks[1], maxes[1], jnp.broadcast_to(maxes[0], (AT_BLOCK, LANES)))
                o_new = jnp.where(masks[1], results[1][:, :LANES], results[0][:, :LANES])
                l_new = jnp.where(masks[1], results[1][:, LANES:], results[0][:, LANES:])
                if not is_first:
                    m_old = _load_runs(ms, p, q_runs)
                    m_tot = jnp.maximum(m_old, m_new)
                    a_old = jnp.exp(m_old - m_tot)
                    a_new = jnp.exp(m_new - m_tot)
                    l_new = a_old * _load_runs(ls, p, q_runs) + a_new * l_new
                    o_new = a_old * _load_runs(acc, p, q_runs) + a_new * o_new
                    m_new = m_tot
                if is_last:
                    _store_runs(acc, p, q_runs, o_new / l_new)
                else:
                    _store_runs(ms, p, q_runs, m_new)
                    _store_runs(ls, p, q_runs, l_new)
                    _store_runs(acc, p, q_runs, o_new)

            stages = [head_pair(p) for p in range(pairs)]
            for _ in range(3):
                for stage in stages:
                    next(stage)
                yield
            for stage in stages:
                next(stage, None)

        def units(j, carry, unit=unit):
            _alternate(*[unit(AT_UNITS * j + k) for k in range(AT_UNITS)])
            return carry

        lax.fori_loop(0, seq // AT_BLOCK // AT_UNITS, units, 0)

    perm_t = perm_t_ref[...]
    for t in range(seq // AT_TILE):
        r0 = t * AT_TILE
        for p in range(pairs):
            y_ref[0, r0:r0 + AT_TILE, p * LANES:(p + 1) * LANES] = _mm(
                perm_t, acc[p, r0:r0 + AT_TILE, :].astype(BF16)).astype(BF16)


def _tile_permutation():
    t = np.arange(AT_TILE)
    row = (t % MAX_DILATION) * (AT_TILE // MAX_DILATION) + t // MAX_DILATION
    perm = np.zeros((AT_TILE, AT_TILE), np.float32)
    perm[row, t] = 1.0
    return perm


def _attention(aq, ak, av, bias):
    b, seq, w = aq.shape
    assert seq == AT_BLOCK * MAX_DILATION and [d for _, d in WINDOWS] == [1, 4, 16]
    slab = pltpu.VMEM((AT_HEADS // 2, seq, LANES), F32)
    tok = pl.BlockSpec((1, seq, w), lambda i: (i, 0, 0))
    perm = _tile_permutation()
    perm_in = jnp.asarray(perm, BF16)
    perm_out = jnp.asarray(perm.T, BF16)
    return pl.pallas_call(
        _attn_kernel,
        grid=(b,),
        in_specs=[tok, tok, tok, _full_spec(perm_in), _full_spec(perm_out), _full_spec(bias)],
        out_specs=tok,
        out_shape=jax.ShapeDtypeStruct((b, seq, w), BF16),
        scratch_shapes=[slab] * 6,
        compiler_params=pltpu.CompilerParams(dimension_semantics=("parallel",),
                                             vmem_limit_bytes=VMEM_LIMIT),
        name="dilated_attention",
    )(aq, ak, av, perm_in, perm_out, bias)


def _t5_bucket(dist):
    max_exact = N_BUCKETS // 2
    d = np.maximum(dist, 1).astype(np.float32)
    log_bucket = max_exact + (np.log(d / np.float32(max_exact))
                              / np.float32(math.log(MAX_DISTANCE / max_exact))
                              * np.float32(N_BUCKETS - max_exact)).astype(np.int32)
    return np.where(dist < max_exact, dist, np.minimum(log_bucket, N_BUCKETS - 1))


def _bias_tables(rel_bias):
    tabs = []
    for window, dil in WINDOWS:
        n_back = window // dil
        j = _run_order(dil)
        rel = j[:, None] + AT_BLOCK - np.concatenate([j, AT_BLOCK + j])[None, :]
        valid = (rel >= 0) & (rel <= n_back)
        bucket = _t5_bucket(dil * np.clip(rel, 0, n_back)).reshape(-1)
        onehot = (np.arange(N_BUCKETS)[:, None] == bucket[None, :]).astype(np.float32)
        bias = jnp.dot(rel_bias.astype(F32).T, jnp.asarray(onehot, BF16).astype(F32), precision=HIGHEST)
        bias = bias.reshape(AT_HEADS, AT_BLOCK, 2 * AT_BLOCK)
        tabs.append(jnp.where(jnp.asarray(valid)[None], bias, NEG_INF))
    return jnp.stack(tabs)


def _cv_kernel(cu_ref, dw_ref, dwb_ref, g_ref, b_ref, y_ref, ypad):
    seq = cu_ref.shape[1]
    n_tiles = seq // SEQ_TILE
    pad = 4 * SUBLANES
    ypad[0:pad, :] = jnp.zeros((pad, CV_WIDTH), F32)
    for t in range(n_tiles):
        r0 = t * SEQ_TILE
        u = cu_ref[0, r0:r0 + SEQ_TILE, :].astype(F32)
        ypad[pad + r0:pad + r0 + SEQ_TILE, :] = u[:, :CV_WIDTH] * _sigmoid(u[:, CV_WIDTH:])
    for t in range(seq // CV_TILE):
        r0 = t * CV_TILE
        window = ypad[r0:r0 + pad + CV_TILE, :]
        acc = jnp.zeros((CV_TILE, CV_WIDTH), F32) + dwb_ref[...]
        for b in range(SUBLANES):
            rolled = pltpu.roll(window, b, axis=0) if b else window
            for j in range(CV_KERNEL):
                off = pad - (CV_KERNEL - 1) + j
                if (-off) % SUBLANES == b:
                    a8 = off + b
                    acc = acc + rolled[a8:a8 + CV_TILE, :] * dw_ref[j:j + 1, :]
        mu = jnp.mean(acc, axis=-1, keepdims=True)
        cen = acc - mu
        var = jnp.mean(cen * cen, axis=-1, keepdims=True)
        yn = cen * lax.rsqrt(var + EPS) * g_ref[...] + b_ref[...]
        y_ref[0, r0:r0 + CV_TILE, :] = _silu(yn).astype(BF16)


def _conformer_conv(cu, dw, dwb, ln_g, ln_b):
    b, seq, w = cu.shape
    return pl.pallas_call(
        _cv_kernel,
        grid=(b,),
        in_specs=[pl.BlockSpec((1, seq, w), lambda i: (i, 0, 0))]
                 + [_full_spec(a) for a in (dw, dwb, ln_g, ln_b)],
        out_specs=pl.BlockSpec((1, seq, CV_WIDTH), lambda i: (i, 0, 0)),
        out_shape=jax.ShapeDtypeStruct((b, seq, CV_WIDTH), BF16),
        scratch_shapes=[pltpu.VMEM((4 * SUBLANES + seq, CV_WIDTH), F32)],
        compiler_params=pltpu.CompilerParams(dimension_semantics=("parallel",),
                                             vmem_limit_bytes=VMEM_LIMIT),
        name="conformer_conv",
    )(cu, dw, dwb, ln_g, ln_b)


def _out_kernel(x_ref, ydn_ref, yat_ref, ycv_ref, wdn_ref, wat_ref, wcv_ref, g_ref, wr_ref, rb_ref,
                tri_ref, xo_ref, h_ref, route_ref, cnt_ref):
    lane = lax.broadcasted_iota(jnp.int32, (OUT_SLAB, LANES), 1)
    picks = {}

    def row_slab(r0):
        sl = slice(r0, r0 + OUT_SLAB)
        x = (x_ref[sl, :] + _mm(ydn_ref[sl, :], wdn_ref[...]) + _mm(yat_ref[sl, :], wat_ref[...])
             + _mm(ycv_ref[sl, :], wcv_ref[...]))
        xo_ref[sl, :] = x
        yield
        ms = jnp.mean(x * x, axis=-1, keepdims=True)
        h = (x * lax.rsqrt(ms + EPS) * g_ref[...]).astype(BF16)
        h_ref[sl, :] = h
        logits = _mm(h, wr_ref[...]) + rb_ref[...]
        yield
        is_group = (lane >= N_EXPERTS) & (lane < N_EXPERTS + N_GROUPS)
        gl = jnp.where(is_group, logits, NEG_INF)
        gmax = jnp.max(gl, axis=-1, keepdims=True)
        gsel = jnp.min(jnp.where(gl == gmax, lane, 2 * LANES), axis=-1, keepdims=True) - N_EXPERTS
        p_group = 1.0 / jnp.sum(jnp.where(is_group, jnp.exp(gl - gmax), 0.0), axis=-1, keepdims=True)
        yield
        lo = gsel * EXPERTS_PER_GROUP
        in_group = (lane >= lo) & (lane < lo + EXPERTS_PER_GROUP)
        el = jnp.where(in_group, logits, NEG_INF)
        v1 = jnp.max(el, axis=-1, keepdims=True)
        i1 = jnp.min(jnp.where(el == v1, lane, LANES), axis=-1, keepdims=True)
        yield
        el2 = jnp.where(lane == i1, NEG_INF, el)
        v2 = jnp.max(el2, axis=-1, keepdims=True)
        i2 = jnp.min(jnp.where(el2 == v2, lane, LANES), axis=-1, keepdims=True)
        t = jnp.exp(v2 - v1)
        g1 = p_group / (1.0 + t)
        picks[r0] = (i1, i2, g1, g1 * t)

    slabs = list(range(0, ROW_TILE, OUT_SLAB))
    _alternate(*[row_slab(r0) for r0 in slabs])

    both = jnp.concatenate(
        [jnp.where((lane == picks[r0][0]) | (lane == picks[r0][1]), 1.0, 0.0).astype(BF16)
         for r0 in slabs], axis=0)
    before = _mm(tri_ref[...], both)
    cnt_ref[0] = _mm(jnp.ones((SUBLANES, ROW_TILE), BF16), both)
    for r0 in slabs:
        i1, i2, g1, g2 = picks[r0]
        seen = before[r0:r0 + OUT_SLAB, :]
        rank1 = jnp.sum(jnp.where(lane == i1, seen, 0.0), axis=-1, keepdims=True)
        rank2 = jnp.sum(jnp.where(lane == i2, seen, 0.0), axis=-1, keepdims=True)
        route = jnp.where(lane == 0, i1.astype(F32), 0.0)
        route = jnp.where(lane == 1, i2.astype(F32), route)
        route = jnp.where(lane == 2, g1, route)
        route = jnp.where(lane == 3, g2, route)
        route = jnp.where(lane == 4, rank1, route)
        route = jnp.where(lane == 5, rank2, route)
        route_ref[r0:r0 + OUT_SLAB, :] = route


def _out_proj(x2, ydn, yat, ycv, wdn, wat, wcv, g, wr, rb, tri):
    n, d = x2.shape
    row = lambda w: pl.BlockSpec((ROW_TILE, w), lambda i: (i, 0))
    n_tiles = n // ROW_TILE
    return pl.pallas_call(
        _out_kernel,
        grid=(n_tiles,),
        in_specs=[row(d), row(DN_WIDTH), row(AT_WIDTH), row(CV_WIDTH)]
                 + [_full_spec(a) for a in (wdn, wat, wcv, g, wr, rb, tri)],
        out_specs=[row(d), row(d), row(LANES), pl.BlockSpec((1, SUBLANES, LANES), lambda i: (i, 0, 0))],
        out_shape=[jax.ShapeDtypeStruct((n, d), F32), jax.ShapeDtypeStruct((n, d), BF16),
                   jax.ShapeDtypeStruct((n, LANES), F32),
                   jax.ShapeDtypeStruct((n_tiles, SUBLANES, LANES), F32)],
        compiler_params=pltpu.CompilerParams(dimension_semantics=("parallel",),
                                             vmem_limit_bytes=VMEM_LIMIT),
        name="out_proj",
    )(x2, ydn, yat, ycv, wdn, wat, wcv, g, wr, rb, tri)


def _tile_slots(route, loc_row):
    lane = lax.broadcasted_iota(jnp.int32, route.shape, 1).astype(F32)
    slots = []
    for kk in range(TOP_K):
        base = jnp.sum(jnp.where(lane == route[:, kk:kk + 1], loc_row, 0.0), axis=-1, keepdims=True)
        slots.append(base + route[:, 4 + kk:5 + kk])
    return slots


def _chunk_cols(j):
    return (j * MOE_CHUNK + lax.broadcasted_iota(jnp.int32, (1, MOE_CHUNK), 1)).astype(F32)


def _piece(ref, row):
    return ref.at[pl.ds(pl.multiple_of(row, MOE_PIECE), MOE_PIECE), :]


def _dispatch_kernel(dst_ref, np_ref, zdst_ref, zvalid_ref, nu_ref, route_ref, loc_ref, h_ref, xs_ref,
                     srt, zeros, sem, zsem):
    tile = pl.program_id(0)
    n_pieces = np_ref[tile]
    n_blocks = xs_ref.shape[0] // MOE_BLOCK

    def zero_piece(z):
        return pltpu.make_async_copy(_piece(zeros, 0), _piece(xs_ref, zdst_ref[z]), zsem)

    def zero_block(b):
        return pltpu.make_async_copy(
            zeros, xs_ref.at[pl.ds(pl.multiple_of(b * MOE_BLOCK, MOE_BLOCK), MOE_BLOCK), :], zsem)

    @pl.when(tile == 0)
    def _():
        zeros[...] = jnp.zeros(zeros.shape, BF16)
        for wait in (False, True):
            def piece_body(z, carry, wait=wait):
                @pl.when(zvalid_ref[z] != 0)
                def _():
                    zero_piece(z).wait() if wait else zero_piece(z).start()
                return carry

            def block_body(b, carry, wait=wait):
                zero_block(b).wait() if wait else zero_block(b).start()
                return carry

            lax.fori_loop(0, N_EXPERTS * ZERO_PIECES, piece_body, 0)
            lax.fori_loop(nu_ref[0], n_blocks, block_body, 0)

    slot1, slot2 = _tile_slots(route_ref[...], loc_ref[0, 0:1, :])
    lane = lax.broadcasted_iota(jnp.int32, (1, LANES), 1)
    slot_cols = jnp.where(lane == 0, slot1, jnp.where(lane == 1, slot2, 0.0))
    pick = (lax.broadcasted_iota(jnp.int32, (SUBLANES, LANES), 0)
            == lax.broadcasted_iota(jnp.int32, (SUBLANES, LANES), 1)).astype(BF16)
    slot_rows = sum(_mm_nt(pick, piece) for piece in _split3(slot_cols))
    slot1_row = slot_rows[0:1, :]
    slot2_row = slot_rows[1:2, :]
    h = h_ref[...]
    buf = srt.at[tile % 2]

    def chunk(j, carry):
        row = (j * MOE_CHUNK + lax.broadcasted_iota(jnp.int32, (MOE_CHUNK, 1), 0)).astype(F32)
        onehot = jnp.where(row == slot1_row, 1.0, jnp.where(row == slot2_row, 1.0, 0.0)).astype(BF16)
        buf[pl.ds(pl.multiple_of(j * MOE_CHUNK, MOE_CHUNK), MOE_CHUNK), :] = _mm(onehot, h).astype(BF16)
        return carry

    pieces_per_chunk = MOE_CHUNK // MOE_PIECE
    lax.fori_loop(0, (n_pieces + pieces_per_chunk - 1) // pieces_per_chunk, chunk, 0)

    def piece_copy(t, p):
        return pltpu.make_async_copy(_piece(srt.at[t % 2], p * MOE_PIECE),
                                     _piece(xs_ref, dst_ref[t * PIECES_MAX + p]), sem.at[t % 2])

    def start(p, carry):
        piece_copy(tile, p).start()
        return carry

    lax.fori_loop(0, n_pieces, start, 0)

    def wait_tile(t):
        done = pltpu.make_async_copy(_piece(srt.at[t % 2], 0), _piece(xs_ref, 0), sem.at[t % 2])

        def wait(p, carry):
            done.wait()
            return carry
        lax.fori_loop(0, np_ref[t], wait, 0)

    @pl.when(tile > 0)
    def _():
        wait_tile(tile - 1)

    @pl.when(tile == pl.num_programs(0) - 1)
    def _():
        wait_tile(tile)


def _dispatch(plan, route, h, n_slots):
    n, d = h.shape
    n_tiles = n // MOE_TILE
    grid_spec = pltpu.PrefetchScalarGridSpec(
        num_scalar_prefetch=5,
        grid=(n_tiles,),
        in_specs=[pl.BlockSpec((MOE_TILE, LANES), lambda i, *_: (i, 0)),
                  pl.BlockSpec((1, SUBLANES, LANES), lambda i, *_: (i, 0, 0)),
                  pl.BlockSpec((MOE_TILE, d), lambda i, *_: (i, 0))],
        out_specs=pl.BlockSpec(memory_space=pl.ANY),
        scratch_shapes=[pltpu.VMEM((2, TILE_SLOTS, d), BF16), pltpu.VMEM((MOE_BLOCK, d), BF16),
                        pltpu.SemaphoreType.DMA((2,)), pltpu.SemaphoreType.DMA(())],
    )
    return pl.pallas_call(
        _dispatch_kernel,
        grid_spec=grid_spec,
        out_shape=jax.ShapeDtypeStruct((n_slots, d), BF16),
        compiler_params=pltpu.CompilerParams(dimension_semantics=("arbitrary",),
                                             vmem_limit_bytes=VMEM_LIMIT),
        name="moe_dispatch",
    )(plan["dst"], plan["n_pieces"], plan["zdst"], plan["zvalid"], plan["n_used"], route,
      plan["loc"], h)


def _expert_kernel(be_ref, nu_ref, x_ref, wg_ref, wu_ref, wd_ref, y_ref, wg_bf, wu_bf, wd_bf):
    i = pl.program_id(0)
    changed = jnp.logical_or(i == 0, be_ref[i] != be_ref[jnp.maximum(i - 1, 0)])

    @pl.when(jnp.logical_and(changed, i < nu_ref[0]))
    def _():
        wg_bf[...] = wg_ref[0].astype(BF16)
        wu_bf[...] = wu_ref[0].astype(BF16)
        wd_bf[...] = wd_ref[0].astype(BF16)

    @pl.when(i < nu_ref[0])
    def _():
        x = x_ref[...]
        g = _mm(x, wg_bf[...])
        u = _mm(x, wu_bf[...])
        y_ref[...] = _mm((_silu(g) * u).astype(BF16), wd_bf[...]).astype(BF16)

    @pl.when(i >= nu_ref[0])
    def _():
        y_ref[...] = jnp.zeros(y_ref.shape, BF16)


def _experts(plan, xs, wg, wu, wd, layer):
    ns, d = xs.shape
    de = wg.shape[3]
    rows = lambda i, be, nu: (jnp.minimum(i, nu[0] - 1), 0)
    grid_spec = pltpu.PrefetchScalarGridSpec(
        num_scalar_prefetch=2,
        grid=(ns // MOE_BLOCK,),
        in_specs=[pl.BlockSpec((MOE_BLOCK, d), rows),
                  pl.BlockSpec((None, 1, d, de), lambda i, be, nu: (layer, be[i], 0, 0)),
                  pl.BlockSpec((None, 1, d, de), lambda i, be, nu: (layer, be[i], 0, 0)),
                  pl.BlockSpec((None, 1, de, d), lambda i, be, nu: (layer, be[i], 0, 0))],
        out_specs=pl.BlockSpec((MOE_BLOCK, d), lambda i, be, nu: (i, 0)),
        scratch_shapes=[pltpu.VMEM((d, de), BF16), pltpu.VMEM((d, de), BF16),
                        pltpu.VMEM((de, d), BF16)],
    )
    return pl.pallas_call(
        _expert_kernel,
        grid_spec=grid_spec,
        out_shape=jax.ShapeDtypeStruct((ns, d), BF16),
        compiler_params=pltpu.CompilerParams(dimension_semantics=("arbitrary",),
                                             vmem_limit_bytes=VMEM_LIMIT),
        name="moe_experts",
    )(plan["block_expert"], plan["n_used"], xs, wg, wu, wd)


def _combine_kernel(dst_ref, np_ref, route_ref, loc_ref, x_ref, y_ref, o_ref, ysrt, sem):
    tile = pl.program_id(0)
    n_pieces = np_ref[tile]

    @pl.when(tile == 0)
    def _():
        ysrt[...] = jnp.zeros(ysrt.shape, BF16)

    def piece_copy(t, p):
        return pltpu.make_async_copy(_piece(y_ref, dst_ref[t * PIECES_MAX + p]),
                                     _piece(ysrt.at[t % 2], p * MOE_PIECE), sem.at[t % 2])

    def fetch_tile(t):
        def start(p, carry):
            piece_copy(t, p).start()
            return carry
        lax.fori_loop(0, np_ref[t], start, 0)

    @pl.when(tile == 0)
    def _():
        fetch_tile(tile)

    @pl.when(tile + 1 < pl.num_programs(0))
    def _():
        fetch_tile(tile + 1)

    route = route_ref[...]
    slot1, slot2 = _tile_slots(route, loc_ref[0, 0:1, :])
    g1 = route[:, 2:3]
    g2 = route[:, 3:4]
    gates = jnp.concatenate(
        [jnp.where(_chunk_cols(j) == slot1, g1, jnp.where(_chunk_cols(j) == slot2, g2, 0.0)).astype(BF16)
         for j in range(TILE_SLOTS // MOE_CHUNK)], axis=1)

    done = pltpu.make_async_copy(_piece(y_ref, 0), _piece(ysrt.at[tile % 2], 0), sem.at[tile % 2])

    def wait(p, carry):
        done.wait()
        return carry

    lax.fori_loop(0, n_pieces, wait, 0)
    o_ref[...] = x_ref[...] + _mm(gates, ysrt[tile % 2])


def _combine(plan, route, x2, y):
    n, d = x2.shape
    grid_spec = pltpu.PrefetchScalarGridSpec(
        num_scalar_prefetch=2,
        grid=(n // MOE_TILE,),
        in_specs=[pl.BlockSpec((MOE_TILE, LANES), lambda i, *_: (i, 0)),
                  pl.BlockSpec((1, SUBLANES, LANES), lambda i, *_: (i, 0, 0)),
                  pl.BlockSpec((MOE_TILE, d), lambda i, *_: (i, 0)),
                  pl.BlockSpec(memory_space=pl.ANY)],
        out_specs=pl.BlockSpec((MOE_TILE, d), lambda i, *_: (i, 0)),
        scratch_shapes=[pltpu.VMEM((2, TILE_SLOTS, d), BF16), pltpu.SemaphoreType.DMA((2,))],
    )
    return pl.pallas_call(
        _combine_kernel,
        grid_spec=grid_spec,
        out_shape=jax.ShapeDtypeStruct((n, d), F32),
        compiler_params=pltpu.CompilerParams(dimension_semantics=("arbitrary",),
                                             vmem_limit_bytes=VMEM_LIMIT),
        name="moe_combine",
    )(plan["dst"], plan["n_pieces"], route, plan["loc"], x2, y)


def _round_up(v, m):
    return (v + m - 1) // m * m


def _moe_plan(counts, n_slots):
    n_tiles = counts.shape[0]
    cnt = counts[:, 0, :N_EXPERTS].astype(jnp.int32)
    seg = _round_up(cnt, MOE_PIECE)
    loc_end = jnp.cumsum(seg, axis=1)
    loc_start = loc_end - seg
    totals = jnp.sum(seg, axis=0)
    padded = _round_up(totals, MOE_BLOCK)
    pad_end = jnp.cumsum(padded)
    pad_start = pad_end - padded
    seg_start = pad_start[None, :] + jnp.cumsum(seg, axis=0) - seg
    piece_off = jnp.arange(PIECES_MAX, dtype=jnp.int32) * MOE_PIECE
    piece_e = jnp.sum(loc_end[:, None, :] <= piece_off[None, :, None], axis=2)
    hit = piece_e[..., None] == jnp.arange(N_EXPERTS)
    shift = jnp.sum(jnp.where(hit, (seg_start - loc_start)[:, None, :], 0), axis=2)
    valid = piece_off[None, :] < loc_end[:, -1:]
    dst = jnp.where(valid, shift + piece_off[None, :], 0).astype(jnp.int32).reshape(-1)
    zk = jnp.arange(ZERO_PIECES, dtype=jnp.int32)[None, :] * MOE_PIECE
    zvalid = zk < (padded - totals)[:, None]
    zdst = jnp.where(zvalid, (pad_start + totals)[:, None] + zk, 0)
    blk_start = jnp.arange(n_slots // MOE_BLOCK, dtype=jnp.int32) * MOE_BLOCK
    block_expert = jnp.minimum(jnp.sum(pad_end[None, :] <= blk_start[:, None], axis=1), N_EXPERTS - 1)
    loc = jnp.pad(loc_start.astype(F32), ((0, 0), (0, LANES - N_EXPERTS)))
    return {
        "dst": dst,
        "n_pieces": (loc_end[:, -1] // MOE_PIECE).astype(jnp.int32),
        "zdst": zdst.astype(jnp.int32).reshape(-1),
        "zvalid": zvalid.astype(jnp.int32).reshape(-1),
        "loc": jnp.broadcast_to(loc[:, None, :], (n_tiles, SUBLANES, LANES)),
        "block_expert": block_expert.astype(jnp.int32),
        "n_used": (pad_end[-1:] // MOE_BLOCK).astype(jnp.int32),
    }


def _pad_lanes(a, width=LANES):
    return jnp.pad(a, [(0, 0)] * (a.ndim - 1) + [(0, width - a.shape[-1])])


def kernel(x, norm_mix, w_in, dn_conv, dn_a_log, dn_dt_bias, dn_out_norm, at_q_norm, at_k_norm,
           rel_bias, cv_dw, cv_dw_bias, cv_ln_g, cv_ln_b, w_out, norm_ffn, router_group_w,
           router_group_b, router_expert_w, router_expert_b, ex_gate, ex_up, ex_down):
    bsz, seq, d = x.shape
    n_tok = bsz * seq
    depth = w_in.shape[0]
    c_ab = 4 * DN_WIDTH
    c_at = c_ab + 2 * DN_HEADS
    c_cv = c_at + 3 * AT_WIDTH
    bd_at = _block_diag_ones(AT_WIDTH, HEAD_DIM, BF16)
    bd_dn = _block_diag_ones(DN_WIDTH, HEAD_DIM, BF16)
    bias = _bias_tables(rel_bias)
    tri = (jnp.arange(ROW_TILE)[:, None] > jnp.arange(ROW_TILE)[None, :]).astype(BF16)
    per_head_lanes = lambda v: jnp.repeat(v, HEAD_DIM)[None, :]
    n_tiles = n_tok // MOE_TILE
    n_slots = _round_up(TOP_K * n_tok + n_tiles * N_EXPERTS * (MOE_PIECE - 1)
                        + N_EXPERTS * (MOE_BLOCK - 1), MOE_BLOCK)

    x2 = x.reshape(n_tok, d)
    for layer in range(depth):
        w_l = w_in[layer]
        dn, aq, ak, av, cu, ab = _proj(
            x2, norm_mix[layer][None, :],
            w_l[:, :c_ab].astype(BF16), w_l[:, c_at:c_cv].astype(BF16), w_l[:, c_cv:].astype(BF16),
            _pad_lanes(w_l[:, c_ab:c_at]).astype(BF16), bd_at,
            jnp.tile(at_q_norm[layer], AT_HEADS)[None, :] * (HEAD_DIM ** -0.5),
            jnp.tile(at_k_norm[layer], AT_HEADS)[None, :])

        y_dn = _deltanet(dn.reshape(bsz, seq, -1), ab.reshape(bsz, seq, LANES), dn_conv[layer],
                         per_head_lanes(dn_a_log[layer]), per_head_lanes(dn_dt_bias[layer]),
                         jnp.tile(dn_out_norm[layer], DN_HEADS)[None, :], bd_dn)
        y_at = _attention(aq.reshape(bsz, seq, -1), ak.reshape(bsz, seq, -1),
                          av.reshape(bsz, seq, -1), bias)
        y_cv = _conformer_conv(cu.reshape(bsz, seq, -1), cv_dw[layer], cv_dw_bias[layer][None, :],
                               cv_ln_g[layer][None, :], cv_ln_b[layer][None, :])

        wo = w_out[layer].astype(BF16)
        w_r = _pad_lanes(jnp.concatenate([router_expert_w[layer], router_group_w[layer]], axis=1))
        b_r = _pad_lanes(jnp.concatenate([router_expert_b[layer], router_group_b[layer]])[None, :])
        x_mid, h_ffn, route, counts = _out_proj(
            x2, y_dn.reshape(n_tok, DN_WIDTH), y_at.reshape(n_tok, AT_WIDTH),
            y_cv.reshape(n_tok, CV_WIDTH), wo[:DN_WIDTH], wo[DN_WIDTH:DN_WIDTH + AT_WIDTH],
            wo[DN_WIDTH + AT_WIDTH:], norm_ffn[layer][None, :], w_r.astype(BF16), b_r, tri)

        plan = _moe_plan(counts, n_slots)
        xs = _dispatch(plan, route, h_ffn, n_slots)
        y = _experts(plan, xs, ex_gate, ex_up, ex_down, layer)
        x2 = _combine(plan, route, x_mid, y)
    return x2.reshape(bsz, seq, d)
```

```python
import math

import jax
import jax.numpy as jnp
import numpy as np
from jax import lax
from jax.experimental import pallas as pl
from jax.experimental.pallas import tpu as pltpu

F32 = jnp.float32
BF16 = jnp.bfloat16
HIGHEST = lax.Precision.HIGHEST

EPS = 1e-6
NEG_INF = -1e30

HEAD_DIM = 64
DN_HEADS = 4
DN_WIDTH = DN_HEADS * HEAD_DIM
DN_CONV = 4
DN_CHUNK = 64
AT_HEADS = 8
AT_WIDTH = AT_HEADS * HEAD_DIM
AT_BLOCK = 128
WINDOWS = ((128, 1), (512, 4), (2048, 16))
MAX_DILATION = 16
AT_TILE = 256
AT_UNITS = 4
N_BUCKETS = 32
MAX_DISTANCE = 2048
CV_WIDTH = 256
CV_KERNEL = 31
N_GROUPS = 4
EXPERTS_PER_GROUP = 8
N_EXPERTS = N_GROUPS * EXPERTS_PER_GROUP
TOP_K = 2

LANES = 128
SUBLANES = 8
VMEM_LIMIT = 52 * 1024 * 1024

ROW_TILE = 512
SEQ_TILE = 256
CV_TILE = 64
OUT_SLAB = 256
DN_TILE = 64
DN_PREP_CHUNKS = 4
MOE_BLOCK = 512
MOE_TILE = ROW_TILE
MOE_PIECE = 16
MOE_CHUNK = 256
TILE_SLOTS = -(-(TOP_K * MOE_TILE + N_EXPERTS * (MOE_PIECE - 1)) // MOE_CHUNK) * MOE_CHUNK
PIECES_MAX = TILE_SLOTS // MOE_PIECE
ZERO_PIECES = MOE_BLOCK // MOE_PIECE - 1


def _mm(a, b, precision=None):
    return jnp.dot(a, b, preferred_element_type=F32, precision=precision)


def _mm_nt(a, b):
    return lax.dot_general(a, b, (((1,), (1,)), ((), ())), preferred_element_type=F32)


def _mm_tn(a, b):
    return lax.dot_general(a, b, (((0,), (0,)), ((), ())), preferred_element_type=F32)


def _sigmoid(x):
    return 0.5 * jnp.tanh(0.5 * x) + 0.5


def _silu(x):
    return x * _sigmoid(x)


def _split3(x):
    p0 = x.astype(BF16)
    r1 = x - p0.astype(F32)
    p1 = r1.astype(BF16)
    p2 = (r1 - p1.astype(F32)).astype(BF16)
    return p0, p1, p2


def _alternate(*stages):
    live = list(stages)
    while live:
        for gen in list(live):
            if next(gen, live) is live:
                live.remove(gen)


def _full_spec(a):
    nd = a.ndim
    return pl.BlockSpec(a.shape, lambda *_: (0,) * nd)


def _block_diag_ones(width, block, dtype):
    r = jnp.arange(width)[:, None] // block
    c = jnp.arange(width)[None, :] // block
    return (r == c).astype(dtype)


def _proj_kernel(x_ref, g_ref, wdn_ref, wat_ref, wcv_ref, wab_ref, bd_ref, qn_ref, kn_ref,
                 dn_ref, aq_ref, ak_ref, av_ref, cv_ref, ab_ref):
    x = x_ref[...]
    ms = jnp.mean(x * x, axis=-1, keepdims=True)
    h = (x * lax.rsqrt(ms + EPS) * g_ref[...]).astype(BF16)
    dn_ref[...] = _mm(h, wdn_ref[...]).astype(BF16)
    cv_ref[...] = _mm(h, wcv_ref[...]).astype(BF16)
    ab_ref[...] = _mm(h, wab_ref[...])
    at = _mm(h, wat_ref[...])
    q = at[:, 0:AT_WIDTH]
    k = at[:, AT_WIDTH:2 * AT_WIDTH]
    bd = bd_ref[...]
    qms = _mm((q * q).astype(BF16), bd) * (1.0 / HEAD_DIM)
    kms = _mm((k * k).astype(BF16), bd) * (1.0 / HEAD_DIM)
    aq_ref[...] = (q * lax.rsqrt(qms + EPS) * qn_ref[...]).astype(BF16)
    ak_ref[...] = (k * lax.rsqrt(kms + EPS) * kn_ref[...]).astype(BF16)
    av_ref[...] = at[:, 2 * AT_WIDTH:3 * AT_WIDTH].astype(BF16)


def _proj(x2, g, wdn, wat, wcv, wab, bd, qn, kn):
    n, d = x2.shape
    row = lambda w: pl.BlockSpec((ROW_TILE, w), lambda i: (i, 0))
    widths = (wdn.shape[1], AT_WIDTH, AT_WIDTH, AT_WIDTH, wcv.shape[1], LANES)
    dtypes = (BF16, BF16, BF16, BF16, BF16, F32)
    return pl.pallas_call(
        _proj_kernel,
        grid=(n // ROW_TILE,),
        in_specs=[row(d)] + [_full_spec(a) for a in (g, wdn, wat, wcv, wab, bd, qn, kn)],
        out_specs=[row(w) for w in widths],
        out_shape=[jax.ShapeDtypeStruct((n, w), t) for w, t in zip(widths, dtypes)],
        compiler_params=pltpu.CompilerParams(dimension_semantics=("parallel",),
                                             vmem_limit_bytes=VMEM_LIMIT),
        name="proj",
    )(x2, g, wdn, wat, wcv, wab, bd, qn, kn)


def _per_head(x, block_mask):
    return jnp.where(block_mask, jnp.concatenate([x] * DN_HEADS, axis=0), jnp.zeros((), x.dtype))


def _dn_kernel(dn_ref, ab_ref, cw_ref, alog_ref, dtb_ref, onorm_ref, bd_ref, y_ref,
               xpad, qs, ks, vs, gs, bs, os_, st, *group_bufs):
    set_a, set_b = group_bufs[:6], group_bufs[6:]
    seq = dn_ref.shape[1]
    n_tiles = seq // DN_TILE
    cw3 = 3 * DN_WIDTH
    pad = SUBLANES
    c = DN_CHUNK
    bd = bd_ref[...]
    hid = lax.broadcasted_iota(jnp.int32, (1, DN_WIDTH), 1) // HEAD_DIM

    def expand(cols, first):
        out = cols[:, first + DN_HEADS - 1:first + DN_HEADS]
        for h in range(DN_HEADS - 2, -1, -1):
            out = jnp.where(hid == h, cols[:, first + h:first + h + 1], out)
        return out

    xpad[0:pad, :] = jnp.zeros((pad, cw3), F32)
    for t in range(n_tiles):
        r0 = t * DN_TILE
        xpad[pad + r0:pad + r0 + DN_TILE, :] = dn_ref[0, r0:r0 + DN_TILE, 0:cw3].astype(F32)
    for t in range(n_tiles):
        r0 = t * DN_TILE
        ab = ab_ref[0, r0:r0 + DN_TILE, :]
        sp_in = expand(ab, 0) + dtb_ref[...]
        softplus = jnp.maximum(sp_in, 0.0) + jnp.log(1.0 + jnp.exp(-jnp.abs(sp_in)))
        gs[r0:r0 + DN_TILE, :] = -jnp.exp(alog_ref[...]) * softplus
        bs[r0:r0 + DN_TILE, :] = _sigmoid(expand(ab, DN_HEADS))
        acc = jnp.zeros((DN_TILE, cw3), F32)
        for j in range(DN_CONV):
            off = pad + r0 - (DN_CONV - 1) + j
            acc = acc + xpad[off:off + DN_TILE, :] * cw_ref[j:j + 1, :]
        y = _silu(acc)
        q = y[:, 0:DN_WIDTH]
        k = y[:, DN_WIDTH:2 * DN_WIDTH]
        qss = _mm((q * q).astype(BF16), bd)
        kss = _mm((k * k).astype(BF16), bd)
        qs[r0:r0 + DN_TILE, :] = q * lax.rsqrt(qss + EPS) * (HEAD_DIM ** -0.5)
        ks[r0:r0 + DN_TILE, :] = k * lax.rsqrt(kss + EPS)
        vs[r0:r0 + DN_TILE, :] = y[:, 2 * DN_WIDTH:3 * DN_WIDTH]

    ri = lax.broadcasted_iota(jnp.int32, (c, DN_WIDTH), 0)
    ci = lax.broadcasted_iota(jnp.int32, (c, DN_WIDTH), 1) % HEAD_DIM
    causal = ri >= ci
    strict = ri > ci
    eye_cat = (ri == ci).astype(F32)
    r2 = lax.broadcasted_iota(jnp.int32, (c, c), 0)
    c2 = lax.broadcasted_iota(jnp.int32, (c, c), 1)
    lower_ones = (r2 >= c2).astype(BF16)
    all_ones = jnp.ones((c, c), BF16)
    block_mask = (lax.broadcasted_iota(jnp.int32, (DN_WIDTH, DN_WIDTH), 0) // HEAD_DIM
                  == lax.broadcasted_iota(jnp.int32, (DN_WIDTH, DN_WIDTH), 1) // HEAD_DIM)

    def mm_exact_rhs(lhs_bf, x):
        p0, p1, p2 = _split3(x)
        return _mm(lhs_bf, p0) + _mm(lhs_bf, p1) + _mm(lhs_bf, p2)

    def mm_bd(lhs, rhs_cat):
        return _mm(lhs.astype(BF16), _per_head(rhs_cat.astype(BF16), block_mask))

    def prep(m, dst):
        ws, us, qks, qds, kds, gls = dst
        rows = [pl.multiple_of((DN_PREP_CHUNKS * m + cc) * c, c) for cc in range(DN_PREP_CHUNKS)]
        g_cum = [mm_exact_rhs(lower_ones, gs[pl.ds(r, c), :]) for r in rows]
        yield
        g_row = [mm_exact_rhs(all_ones, g * eye_cat) for g in g_cum]
        yield
        decay = [jnp.exp(jnp.where(causal, g - gr, NEG_INF)) for g, gr in zip(g_cum, g_row)]
        kc = [ks[pl.ds(r, c), :] for r in rows]
        qc = [qs[pl.ds(r, c), :] for r in rows]
        beta = [bs[pl.ds(r, c), :] for r in rows]
        kb = [k * b for k, b in zip(kc, beta)]
        aq = [_mm_nt(jnp.concatenate([b_, q_], axis=0).astype(BF16),
                     _per_head(k_.astype(BF16), block_mask))
              for b_, q_, k_ in zip(kb, qc, kc)]
        a_mat = [jnp.where(strict, x[:c] * d, 0.0) for x, d in zip(aq, decay)]
        qk = [x[c:] * d for x, d in zip(aq, decay)]
        p = [eye_cat - a for a in a_mat]
        yield
        pw = [mm_bd(a, a) for a in a_mat]
        for _ in range(4):
            yield
            both = [mm_bd(jnp.concatenate([p_, x], axis=0), x) for p_, x in zip(p, pw)]
            p = [p_ + b_[:c] for p_, b_ in zip(p, both)]
            pw = [b_[c:] for b_ in both]
        yield
        p = [p_ + mm_bd(p_, x) for p_, x in zip(p, pw)]
        yield
        for i, r in enumerate(rows):
            e_g = jnp.exp(g_cum[i])
            sl = slice(i * c, (i + 1) * c)
            ws[sl, :] = mm_bd(p[i], kb[i] * e_g).astype(BF16)
            us[sl, :] = mm_bd(p[i], vs[pl.ds(r, c), :] * beta[i])
            qks[sl, :] = qk[i].astype(BF16)
            qds[sl, :] = (qc[i] * e_g).astype(BF16)
            g_last = g_cum[i][c - 1:c, :]
            kds[sl, :] = (kc[i] * jnp.exp(g_last - g_cum[i])).astype(BF16)
            gls[i * SUBLANES:(i + 1) * SUBLANES, :] = jnp.broadcast_to(jnp.exp(g_last),
                                                                       (SUBLANES, DN_WIDTH))

    def scan_group(m, src):
        ws, us, qks, qds, kds, gls = src
        for i in range(DN_PREP_CHUNKS):
            sl = slice(i * c, (i + 1) * c)
            r = pl.multiple_of((DN_PREP_CHUNKS * m + i) * c, c)
            state = st[...]
            both = _mm(jnp.concatenate([ws[sl, :], qds[sl, :]], axis=0),
                       _per_head(state.astype(BF16), block_mask))
            yield
            v_new = (us[sl, :] - both[:c]).astype(BF16)
            os_[pl.ds(r, c), :] = both[c:] + _mm(qks[sl, :], _per_head(v_new, block_mask))
            kv = _mm_tn(kds[sl, :], v_new)
            upd = kv[(DN_HEADS - 1) * c:DN_HEADS * c, :]
            for h in range(DN_HEADS - 2, -1, -1):
                upd = jnp.where(hid == h, kv[h * c:(h + 1) * c, :], upd)
            yield
            st[...] = state * gls[i * SUBLANES:i * SUBLANES + 1, :] + upd

    alternate = _alternate

    st[...] = jnp.zeros(st.shape, F32)
    n_groups = seq // (DN_PREP_CHUNKS * c)
    assert n_groups % 2 == 0
    alternate(prep(0, set_a))

    def pair(j, carry):
        alternate(prep(2 * j + 1, set_b), scan_group(2 * j, set_a))
        alternate(prep(2 * j + 2, set_a), scan_group(2 * j + 1, set_b))
        return carry

    lax.fori_loop(0, n_groups // 2 - 1, pair, 0)
    alternate(prep(n_groups - 1, set_b), scan_group(n_groups - 2, set_a))
    alternate(scan_group(n_groups - 1, set_b))

    for t in range(n_tiles):
        r0 = t * DN_TILE
        o = os_[r0:r0 + DN_TILE, :]
        z = dn_ref[0, r0:r0 + DN_TILE, cw3:cw3 + DN_WIDTH].astype(F32)
        ms = _mm((o * o).astype(BF16), bd) * (1.0 / HEAD_DIM)
        y_ref[0, r0:r0 + DN_TILE, :] = (o * lax.rsqrt(ms + EPS) * onorm_ref[...] * _silu(z)).astype(BF16)


def _deltanet(dn, ab, conv_w, alog_cat, dtb_cat, onorm_cat, bd):
    b, seq, w = dn.shape
    f32buf = pltpu.VMEM((seq, DN_WIDTH), F32)
    group_rows = DN_PREP_CHUNKS * DN_CHUNK
    group_set = [pltpu.VMEM((group_rows, DN_WIDTH), BF16),
                 pltpu.VMEM((group_rows, DN_WIDTH), F32),
                 pltpu.VMEM((group_rows, DN_WIDTH), BF16),
                 pltpu.VMEM((group_rows, DN_WIDTH), BF16),
                 pltpu.VMEM((group_rows, DN_WIDTH), BF16),
                 pltpu.VMEM((DN_PREP_CHUNKS * SUBLANES, DN_WIDTH), F32)]
    return pl.pallas_call(
        _dn_kernel,
        grid=(b,),
        in_specs=[pl.BlockSpec((1, seq, w), lambda i: (i, 0, 0)),
                  pl.BlockSpec((1, seq, LANES), lambda i: (i, 0, 0))]
                 + [_full_spec(a) for a in (conv_w, alog_cat, dtb_cat, onorm_cat, bd)],
        out_specs=pl.BlockSpec((1, seq, DN_WIDTH), lambda i: (i, 0, 0)),
        out_shape=jax.ShapeDtypeStruct((b, seq, DN_WIDTH), BF16),
        scratch_shapes=[pltpu.VMEM((SUBLANES + seq, 3 * DN_WIDTH), F32),
                        f32buf, f32buf, f32buf, f32buf, f32buf,
                        f32buf,
                        pltpu.VMEM((DN_CHUNK, DN_WIDTH), F32)]
                       + group_set + group_set,
        compiler_params=pltpu.CompilerParams(dimension_semantics=("parallel",),
                                             vmem_limit_bytes=VMEM_LIMIT),
        name="deltanet",
    )(dn, ab, conv_w, alog_cat, dtb_cat, onorm_cat, bd)


def _run_starts(dil, r, i):
    if dil == 16:
        return [(tt * AT_TILE + r * 16, 16) for tt in range(AT_BLOCK * dil // AT_TILE)]
    if dil == 4:
        return [((2 * i + th) * AT_TILE + (4 * s + r) * 16, 16) for th in range(2) for s in range(4)]
    assert dil == 1
    return [((i // 2) * AT_TILE + rr * 16 + 8 * (i % 2), 8) for rr in range(16)]


def _run_order(dil):
    if dil == 16:
        return np.arange(AT_BLOCK)
    if dil == 4:
        th, s, ml = np.meshgrid(np.arange(2), np.arange(4), np.arange(16), indexing="ij")
        return (64 * th + 4 * ml + s).reshape(-1)
    rr, m8 = np.meshgrid(np.arange(16), np.arange(8), indexing="ij")
    return (16 * m8 + rr).reshape(-1)


def _load_runs(ref, p, runs):
    return jnp.concatenate([ref[p, pl.ds(pl.multiple_of(s, n), n), :] for s, n in runs], axis=0)


def _store_runs(ref, p, runs, val):
    off = 0
    for s, n in runs:
        ref[p, pl.ds(pl.multiple_of(s, n), n), :] = val[off:off + n]
        off += n


def _attn_kernel(q_ref, k_ref, v_ref, perm_ref, perm_t_ref, bias_ref, y_ref, qf, kf, vf, acc, ms, ls):
    seq = q_ref.shape[1]
    pairs = AT_HEADS // 2
    perm = perm_ref[...]
    for t in range(seq // AT_TILE):
        r0 = t * AT_TILE
        for src, dst in ((q_ref, qf), (k_ref, kf), (v_ref, vf)):
            rows = _mm(perm, src[0, r0:r0 + AT_TILE, :])
            for p in range(pairs):
                dst[p, r0:r0 + AT_TILE, :] = rows[:, p * LANES:(p + 1) * LANES]

    lane = lax.broadcasted_iota(jnp.int32, (1, LANES), 1)
    key_col = lax.broadcasted_iota(jnp.int32, (1, 2 * AT_BLOCK), 1)
    order = sorted(range(len(WINDOWS)), key=lambda g: -WINDOWS[g][1])
    for step, grp in enumerate(order):
        dil = WINDOWS[grp][1]
        nb = seq // dil // AT_BLOCK
        is_first = step == 0
        is_last = step == len(order) - 1

        def unit(u, dil=dil, nb=nb, grp=grp, is_first=is_first, is_last=is_last):
            r = u // nb
            i = u % nb
            q_runs = _run_starts(dil, r, i)
            p_runs = _run_starts(dil, r, jnp.maximum(i - 1, 0))
            no_prev = jnp.where(jnp.logical_and(i == 0, key_col < AT_BLOCK), NEG_INF, 0.0)
            def head_pair(p):
                q2 = _load_runs(qf, p, q_runs).astype(BF16)
                k2 = jnp.concatenate([_load_runs(kf, p, p_runs), _load_runs(kf, p, q_runs)],
                                     axis=0).astype(BF16)
                v2 = jnp.concatenate([_load_runs(vf, p, p_runs), _load_runs(vf, p, q_runs)],
                                     axis=0).astype(BF16)
                v_ext = jnp.concatenate([v2, jnp.ones(v2.shape, BF16)], axis=1)
                masks = [(lane // HEAD_DIM) == hh for hh in range(2)]
                scores = [_mm_nt(jnp.where(mask, q2, jnp.zeros_like(q2)), k2)
                          + bias_ref[grp, 2 * p + hh] + no_prev for hh, mask in enumerate(masks)]
                yield
                maxes = [jnp.max(s, axis=-1, keepdims=True) for s in scores]
                probs = [jnp.exp(s - m_h).astype(BF16) for s, m_h in zip(scores, maxes)]
                yield
                results = [_mm(pexp, v_ext) for pexp in probs]
                yield
                m_new = jnp.where(masks[1], maxes[1], jnp.broadcast_to(maxes[0], (AT_BLOCK, LANES)))
                o_new = jnp.where(masks[1], results[1][:, :LANES], results[0][:, :LANES])
                l_new = jnp.where(masks[1], results[1][:, LANES:], results[0][:, LANES:])
                if not is_first:
                    m_old = _load_runs(ms, p, q_runs)
                    m_tot = jnp.maximum(m_old, m_new)
                    a_old = jnp.exp(m_old - m_tot)
                    a_new = jnp.exp(m_new - m_tot)
                    l_new = a_old * _load_runs(ls, p, q_runs) + a_new * l_new
                    o_new = a_old * _load_runs(acc, p, q_runs) + a_new * o_new
                    m_new = m_tot
                if is_last:
                    _store_runs(acc, p, q_runs, o_new / l_new)
                else:
                    _store_runs(ms, p, q_runs, m_new)
                    _store_runs(ls, p, q_runs, l_new)
                    _store_runs(acc, p, q_runs, o_new)

            stages = [head_pair(p) for p in range(pairs)]
            for _ in range(3):
                for stage in stages:
                    next(stage)
                yield
            for stage in stages:
                next(stage, None)

        def units(j, carry, unit=unit):
            _alternate(*[unit(AT_UNITS * j + k) for k in range(AT_UNITS)])
            return carry

        lax.fori_loop(0, seq // AT_BLOCK // AT_UNITS, units, 0)

    perm_t = perm_t_ref[...]
    for t in range(seq // AT_TILE):
        r0 = t * AT_TILE
        for p in range(pairs):
            y_ref[0, r0:r0 + AT_TILE, p * LANES:(p + 1) * LANES] = _mm(
                perm_t, acc[p, r0:r0 + AT_TILE, :].astype(BF16)).astype(BF16)


def _tile_permutation():
    t = np.arange(AT_TILE)
    row = (t % MAX_DILATION) * (AT_TILE // MAX_DILATION) + t // MAX_DILATION
    perm = np.zeros((AT_TILE, AT_TILE), np.float32)
    perm[row, t] = 1.0
    return perm


def _attention(aq, ak, av, bias):
    b, seq, w = aq.shape
    assert seq == AT_BLOCK * MAX_DILATION and [d for _, d in WINDOWS] == [1, 4, 16]
    slab = pltpu.VMEM((AT_HEADS // 2, seq, LANES), F32)
    tok = pl.BlockSpec((1, seq, w), lambda i: (i, 0, 0))
    perm = _tile_permutation()
    perm_in = jnp.asarray(perm, BF16)
    perm_out = jnp.asarray(perm.T, BF16)
    return pl.pallas_call(
        _attn_kernel,
        grid=(b,),
        in_specs=[tok, tok, tok, _full_spec(perm_in), _full_spec(perm_out), _full_spec(bias)],
        out_specs=tok,
        out_shape=jax.ShapeDtypeStruct((b, seq, w), BF16),
        scratch_shapes=[slab] * 6,
        compiler_params=pltpu.CompilerParams(dimension_semantics=("parallel",),
                                             vmem_limit_bytes=VMEM_LIMIT),
        name="dilated_attention",
    )(aq, ak, av, perm_in, perm_out, bias)


def _t5_bucket(dist):
    max_exact = N_BUCKETS // 2
    d = np.maximum(dist, 1).astype(np.float32)
    log_bucket = max_exact + (np.log(d / np.float32(max_exact))
                              / np.float32(math.log(MAX_DISTANCE / max_exact))
                              * np.float32(N_BUCKETS - max_exact)).astype(np.int32)
    return np.where(dist < max_exact, dist, np.minimum(log_bucket, N_BUCKETS - 1))


def _bias_tables(rel_bias):
    tabs = []
    for window, dil in WINDOWS:
        n_back = window // dil
        j = _run_order(dil)
        rel = j[:, None] + AT_BLOCK - np.concatenate([j, AT_BLOCK + j])[None, :]
        valid = (rel >= 0) & (rel <= n_back)
        bucket = _t5_bucket(dil * np.clip(rel, 0, n_back)).reshape(-1)
        onehot = (np.arange(N_BUCKETS)[:, None] == bucket[None, :]).astype(np.float32)
        bias = jnp.dot(rel_bias.astype(F32).T, jnp.asarray(onehot, BF16).astype(F32), precision=HIGHEST)
        bias = bias.reshape(AT_HEADS, AT_BLOCK, 2 * AT_BLOCK)
        tabs.append(jnp.where(jnp.asarray(valid)[None], bias, NEG_INF))
    return jnp.stack(tabs)


def _cv_kernel(cu_ref, dw_ref, dwb_ref, g_ref, b_ref, y_ref, ypad):
    seq = cu_ref.shape[1]
    n_tiles = seq // SEQ_TILE
    pad = 4 * SUBLANES
    ypad[0:pad, :] = jnp.zeros((pad, CV_WIDTH), F32)
    for t in range(n_tiles):
        r0 = t * SEQ_TILE
        u = cu_ref[0, r0:r0 + SEQ_TILE, :].astype(F32)
        ypad[pad + r0:pad + r0 + SEQ_TILE, :] = u[:, :CV_WIDTH] * _sigmoid(u[:, CV_WIDTH:])
    for t in range(seq // CV_TILE):
        r0 = t * CV_TILE
        window = ypad[r0:r0 + pad + CV_TILE, :]
        acc = jnp.zeros((CV_TILE, CV_WIDTH), F32) + dwb_ref[...]
        for b in range(SUBLANES):
            rolled = pltpu.roll(window, b, axis=0) if b else window
            for j in range(CV_KERNEL):
                off = pad - (CV_KERNEL - 1) + j
                if (-off) % SUBLANES == b:
                    a8 = off + b
                    acc = acc + rolled[a8:a8 + CV_TILE, :] * dw_ref[j:j + 1, :]
        mu = jnp.mean(acc, axis=-1, keepdims=True)
        cen = acc - mu
        var = jnp.mean(cen * cen, axis=-1, keepdims=True)
        yn = cen * lax.rsqrt(var + EPS) * g_ref[...] + b_ref[...]
        y_ref[0, r0:r0 + CV_TILE, :] = _silu(yn).astype(BF16)


def _conformer_conv(cu, dw, dwb, ln_g, ln_b):
    b, seq, w = cu.shape
    return pl.pallas_call(
        _cv_kernel,
        grid=(b,),
        in_specs=[pl.BlockSpec((1, seq, w), lambda i: (i, 0, 0))]
                 + [_full_spec(a) for a in (dw, dwb, ln_g, ln_b)],
        out_specs=pl.BlockSpec((1, seq, CV_WIDTH), lambda i: (i, 0, 0)),
        out_shape=jax.ShapeDtypeStruct((b, seq, CV_WIDTH), BF16),
        scratch_shapes=[pltpu.VMEM((4 * SUBLANES + seq, CV_WIDTH), F32)],
        compiler_params=pltpu.CompilerParams(dimension_semantics=("parallel",),
                                             vmem_limit_bytes=VMEM_LIMIT),
        name="conformer_conv",
    )(cu, dw, dwb, ln_g, ln_b)


def _out_kernel(x_ref, ydn_ref, yat_ref, ycv_ref, wdn_ref, wat_ref, wcv_ref, g_ref, wr_ref, rb_ref,
                tri_ref, xo_ref, h_ref, route_ref, cnt_ref):
    lane = lax.broadcasted_iota(jnp.int32, (OUT_SLAB, LANES), 1)
    picks = {}

    def row_slab(r0):
        sl = slice(r0, r0 + OUT_SLAB)
        x = (x_ref[sl, :] + _mm(ydn_ref[sl, :], wdn_ref[...]) + _mm(yat_ref[sl, :], wat_ref[...])
             + _mm(ycv_ref[sl, :], wcv_ref[...]))
        xo_ref[sl, :] = x
        yield
        ms = jnp.mean(x * x, axis=-1, keepdims=True)
        h = (x * lax.rsqrt(ms + EPS) * g_ref[...]).astype(BF16)
        h_ref[sl, :] = h
        logits = _mm(h, wr_ref[...]) + rb_ref[...]
        yield
        is_group = (lane >= N_EXPERTS) & (lane < N_EXPERTS + N_GROUPS)
        gl = jnp.where(is_group, logits, NEG_INF)
        gmax = jnp.max(gl, axis=-1, keepdims=True)
        gsel = jnp.min(jnp.where(gl == gmax, lane, 2 * LANES), axis=-1, keepdims=True) - N_EXPERTS
        p_group = 1.0 / jnp.sum(jnp.where(is_group, jnp.exp(gl - gmax), 0.0), axis=-1, keepdims=True)
        yield
        lo = gsel * EXPERTS_PER_GROUP
        in_group = (lane >= lo) & (lane < lo + EXPERTS_PER_GROUP)
        el = jnp.where(in_group, logits, NEG_INF)
        v1 = jnp.max(el, axis=-1, keepdims=True)
        i1 = jnp.min(jnp.where(el == v1, lane, LANES), axis=-1, keepdims=True)
        yield
        el2 = jnp.where(lane == i1, NEG_INF, el)
        v2 = jnp.max(el2, axis=-1, keepdims=True)
        i2 = jnp.min(jnp.where(el2 == v2, lane, LANES), axis=-1, keepdims=True)
        t = jnp.exp(v2 - v1)
        g1 = p_group / (1.0 + t)
        picks[r0] = (i1, i2, g1, g1 * t)

    slabs = list(range(0, ROW_TILE, OUT_SLAB))
    _alternate(*[row_slab(r0) for r0 in slabs])

    both = jnp.concatenate(
        [jnp.where((lane == picks[r0][0]) | (lane == picks[r0][1]), 1.0, 0.0).astype(BF16)
         for r0 in slabs], axis=0)
    before = _mm(tri_ref[...], both)
    cnt_ref[0] = _mm(jnp.ones((SUBLANES, ROW_TILE), BF16), both)
    for r0 in slabs:
        i1, i2, g1, g2 = picks[r0]
        seen = before[r0:r0 + OUT_SLAB, :]
        rank1 = jnp.sum(jnp.where(lane == i1, seen, 0.0), axis=-1, keepdims=True)
        rank2 = jnp.sum(jnp.where(lane == i2, seen, 0.0), axis=-1, keepdims=True)
        route = jnp.where(lane == 0, i1.astype(F32), 0.0)
        route = jnp.where(lane == 1, i2.astype(F32), route)
        route = jnp.where(lane == 2, g1, route)
        route = jnp.where(lane == 3, g2, route)
        route = jnp.where(lane == 4, rank1, route)
        route = jnp.where(lane == 5, rank2, route)
        route_ref[r0:r0 + OUT_SLAB, :] = route


def _out_proj(x2, ydn, yat, ycv, wdn, wat, wcv, g, wr, rb, tri):
    n, d = x2.shape
    row = lambda w: pl.BlockSpec((ROW_TILE, w), lambda i: (i, 0))
    n_tiles = n // ROW_TILE
    return pl.pallas_call(
        _out_kernel,
        grid=(n_tiles,),
        in_specs=[row(d), row(DN_WIDTH), row(AT_WIDTH), row(CV_WIDTH)]
                 + [_full_spec(a) for a in (wdn, wat, wcv, g, wr, rb, tri)],
        out_specs=[row(d), row(d), row(LANES), pl.BlockSpec((1, SUBLANES, LANES), lambda i: (i, 0, 0))],
        out_shape=[jax.ShapeDtypeStruct((n, d), F32), jax.ShapeDtypeStruct((n, d), BF16),
                   jax.ShapeDtypeStruct((n, LANES), F32),
                   jax.ShapeDtypeStruct((n_tiles, SUBLANES, LANES), F32)],
        compiler_params=pltpu.CompilerParams(dimension_semantics=("parallel",),
                                             vmem_limit_bytes=VMEM_LIMIT),
        name="out_proj",
    )(x2, ydn, yat, ycv, wdn, wat, wcv, g, wr, rb, tri)


def _tile_slots(route, loc_row):
    lane = lax.broadcasted_iota(jnp.int32, route.shape, 1).astype(F32)
    slots = []
    for kk in range(TOP_K):
        base = jnp.sum(jnp.where(lane == route[:, kk:kk + 1], loc_row, 0.0), axis=-1, keepdims=True)
        slots.append(base + route[:, 4 + kk:5 + kk])
    return slots


def _chunk_cols(j):
    return (j * MOE_CHUNK + lax.broadcasted_iota(jnp.int32, (1, MOE_CHUNK), 1)).astype(F32)


def _piece(ref, row):
    return ref.at[pl.ds(pl.multiple_of(row, MOE_PIECE), MOE_PIECE), :]


def _dispatch_kernel(dst_ref, np_ref, zdst_ref, zvalid_ref, nu_ref, route_ref, loc_ref, h_ref, xs_ref,
                     srt, zeros, sem, zsem):
    tile = pl.program_id(0)
    n_pieces = np_ref[tile]
    n_blocks = xs_ref.shape[0] // MOE_BLOCK

    def zero_piece(z):
        return pltpu.make_async_copy(_piece(zeros, 0), _piece(xs_ref, zdst_ref[z]), zsem)

    def zero_block(b):
        return pltpu.make_async_copy(
            zeros, xs_ref.at[pl.ds(pl.multiple_of(b * MOE_BLOCK, MOE_BLOCK), MOE_BLOCK), :], zsem)

    @pl.when(tile == 0)
    def _():
        zeros[...] = jnp.zeros(zeros.shape, BF16)
        for wait in (False, True):
            def piece_body(z, carry, wait=wait):
                @pl.when(zvalid_ref[z] != 0)
                def _():
                    zero_piece(z).wait() if wait else zero_piece(z).start()
                return carry

            def block_body(b, carry, wait=wait):
                zero_block(b).wait() if wait else zero_block(b).start()
                return carry

            lax.fori_loop(0, N_EXPERTS * ZERO_PIECES, piece_body, 0)
            lax.fori_loop(nu_ref[0], n_blocks, block_body, 0)

    slot1, slot2 = _tile_slots(route_ref[...], loc_ref[0, 0:1, :])
    lane = lax.broadcasted_iota(jnp.int32, (1, LANES), 1)
    slot_cols = jnp.where(lane == 0, slot1, jnp.where(lane == 1, slot2, 0.0))
    pick = (lax.broadcasted_iota(jnp.int32, (SUBLANES, LANES), 0)
            == lax.broadcasted_iota(jnp.int32, (SUBLANES, LANES), 1)).astype(BF16)
    slot_rows = sum(_mm_nt(pick, piece) for piece in _split3(slot_cols))
    slot1_row = slot_rows[0:1, :]
    slot2_row = slot_rows[1:2, :]
    h = h_ref[...]
    buf = srt.at[tile % 2]

    def chunk(j, carry):
        row = (j * MOE_CHUNK + lax.broadcasted_iota(jnp.int32, (MOE_CHUNK, 1), 0)).astype(F32)
        onehot = jnp.where(row == slot1_row, 1.0, jnp.where(row == slot2_row, 1.0, 0.0)).astype(BF16)
        buf[pl.ds(pl.multiple_of(j * MOE_CHUNK, MOE_CHUNK), MOE_CHUNK), :] = _mm(onehot, h).astype(BF16)
        return carry

    pieces_per_chunk = MOE_CHUNK // MOE_PIECE
    lax.fori_loop(0, (n_pieces + pieces_per_chunk - 1) // pieces_per_chunk, chunk, 0)

    def piece_copy(t, p):
        return pltpu.make_async_copy(_piece(srt.at[t % 2], p * MOE_PIECE),
                                     _piece(xs_ref, dst_ref[t * PIECES_MAX + p]), sem.at[t % 2])

    def start(p, carry):
        piece_copy(tile, p).start()
        return carry

    lax.fori_loop(0, n_pieces, start, 0)

    def wait_tile(t):
        done = pltpu.make_async_copy(_piece(srt.at[t % 2], 0), _piece(xs_ref, 0), sem.at[t % 2])

        def wait(p, carry):
            done.wait()
            return carry
        lax.fori_loop(0, np_ref[t], wait, 0)

    @pl.when(tile > 0)
    def _():
        wait_tile(tile - 1)

    @pl.when(tile == pl.num_programs(0) - 1)
    def _():
        wait_tile(tile)


def _dispatch(plan, route, h, n_slots):
    n, d = h.shape
    n_tiles = n // MOE_TILE
    grid_spec = pltpu.PrefetchScalarGridSpec(
        num_scalar_prefetch=5,
        grid=(n_tiles,),
        in_specs=[pl.BlockSpec((MOE_TILE, LANES), lambda i, *_: (i, 0)),
                  pl.BlockSpec((1, SUBLANES, LANES), lambda i, *_: (i, 0, 0)),
                  pl.BlockSpec((MOE_TILE, d), lambda i, *_: (i, 0))],
        out_specs=pl.BlockSpec(memory_space=pl.ANY),
        scratch_shapes=[pltpu.VMEM((2, TILE_SLOTS, d), BF16), pltpu.VMEM((MOE_BLOCK, d), BF16),
                        pltpu.SemaphoreType.DMA((2,)), pltpu.SemaphoreType.DMA(())],
    )
    return pl.pallas_call(
        _dispatch_kernel,
        grid_spec=grid_spec,
        out_shape=jax.ShapeDtypeStruct((n_slots, d), BF16),
        compiler_params=pltpu.CompilerParams(dimension_semantics=("arbitrary",),
                                             vmem_limit_bytes=VMEM_LIMIT),
        name="moe_dispatch",
    )(plan["dst"], plan["n_pieces"], plan["zdst"], plan["zvalid"], plan["n_used"], route,
      plan["loc"], h)


def _expert_kernel(be_ref, nu_ref, x_ref, wg_ref, wu_ref, wd_ref, y_ref, wg_bf, wu_bf, wd_bf):
    i = pl.program_id(0)
    changed = jnp.logical_or(i == 0, be_ref[i] != be_ref[jnp.maximum(i - 1, 0)])

    @pl.when(jnp.logical_and(changed, i < nu_ref[0]))
    def _():
        wg_bf[...] = wg_ref[0].astype(BF16)
        wu_bf[...] = wu_ref[0].astype(BF16)
        wd_bf[...] = wd_ref[0].astype(BF16)

    @pl.when(i < nu_ref[0])
    def _():
        x = x_ref[...]
        g = _mm(x, wg_bf[...])
        u = _mm(x, wu_bf[...])
        y_ref[...] = _mm((_silu(g) * u).astype(BF16), wd_bf[...]).astype(BF16)

    @pl.when(i >= nu_ref[0])
    def _():
        y_ref[...] = jnp.zeros(y_ref.shape, BF16)


def _experts(plan, xs, wg, wu, wd, layer):
    ns, d = xs.shape
    de = wg.shape[3]
    rows = lambda i, be, nu: (jnp.minimum(i, nu[0] - 1), 0)
    grid_spec = pltpu.PrefetchScalarGridSpec(
        num_scalar_prefetch=2,
        grid=(ns // MOE_BLOCK,),
        in_specs=[pl.BlockSpec((MOE_BLOCK, d), rows),
                  pl.BlockSpec((None, 1, d, de), lambda i, be, nu: (layer, be[i], 0, 0)),
                  pl.BlockSpec((None, 1, d, de), lambda i, be, nu: (layer, be[i], 0, 0)),
                  pl.BlockSpec((None, 1, de, d), lambda i, be, nu: (layer, be[i], 0, 0))],
        out_specs=pl.BlockSpec((MOE_BLOCK, d), lambda i, be, nu: (i, 0)),
        scratch_shapes=[pltpu.VMEM((d, de), BF16), pltpu.VMEM((d, de), BF16),
                        pltpu.VMEM((de, d), BF16)],
    )
    return pl.pallas_call(
        _expert_kernel,
        grid_spec=grid_spec,
        out_shape=jax.ShapeDtypeStruct((ns, d), BF16),
        compiler_params=pltpu.CompilerParams(dimension_semantics=("arbitrary",),
                                             vmem_limit_bytes=VMEM_LIMIT),
        name="moe_experts",
    )(plan["block_expert"], plan["n_used"], xs, wg, wu, wd)


def _combine_kernel(dst_ref, np_ref, route_ref, loc_ref, x_ref, y_ref, o_ref, ysrt, sem):
    tile = pl.program_id(0)
    n_pieces = np_ref[tile]

    @pl.when(tile == 0)
    def _():
        ysrt[...] = jnp.zeros(ysrt.shape, BF16)

    def piece_copy(t, p):
        return pltpu.make_async_copy(_piece(y_ref, dst_ref[t * PIECES_MAX + p]),
                                     _piece(ysrt.at[t % 2], p * MOE_PIECE), sem.at[t % 2])

    def fetch_tile(t):
        def start(p, carry):
            piece_copy(t, p).start()
            return carry
        lax.fori_loop(0, np_ref[t], start, 0)

    @pl.when(tile == 0)
    def _():
        fetch_tile(tile)

    @pl.when(tile + 1 < pl.num_programs(0))
    def _():
        fetch_tile(tile + 1)

    route = route_ref[...]
    slot1, slot2 = _tile_slots(route, loc_ref[0, 0:1, :])
    g1 = route[:, 2:3]
    g2 = route[:, 3:4]
    done = pltpu.make_async_copy(_piece(y_ref, 0), _piece(ysrt.at[tile % 2], 0), sem.at[tile % 2])

    def wait(p, carry):
        done.wait()
        return carry

    lax.fori_loop(0, n_pieces, wait, 0)

    def row_slab(r0):
        sl = slice(r0, r0 + OUT_SLAB)
        gates = jnp.concatenate(
            [jnp.where(_chunk_cols(j) == slot1[sl], g1[sl],
                       jnp.where(_chunk_cols(j) == slot2[sl], g2[sl], 0.0)).astype(BF16)
             for j in range(TILE_SLOTS // MOE_CHUNK)], axis=1)
        yield
        o_ref[sl, :] = x_ref[sl, :] + _mm(gates, ysrt[tile % 2])

    _alternate(*[row_slab(r0) for r0 in range(0, MOE_TILE, OUT_SLAB)])


def _combine(plan, route, x2, y):
    n, d = x2.shape
    grid_spec = pltpu.PrefetchScalarGridSpec(
        num_scalar_prefetch=2,
        grid=(n // MOE_TILE,),
        in_specs=[pl.BlockSpec((MOE_TILE, LANES), lambda i, *_: (i, 0)),
                  pl.BlockSpec((1, SUBLANES, LANES), lambda i, *_: (i, 0, 0)),
                  pl.BlockSpec((MOE_TILE, d), lambda i, *_: (i, 0)),
                  pl.BlockSpec(memory_space=pl.ANY)],
        out_specs=pl.BlockSpec((MOE_TILE, d), lambda i, *_: (i, 0)),
        scratch_shapes=[pltpu.VMEM((2, TILE_SLOTS, d), BF16), pltpu.SemaphoreType.DMA((2,))],
    )
    return pl.pallas_call(
        _combine_kernel,
        grid_spec=grid_spec,
        out_shape=jax.ShapeDtypeStruct((n, d), F32),
        compiler_params=pltpu.CompilerParams(dimension_semantics=("arbitrary",),
                                             vmem_limit_bytes=VMEM_LIMIT),
        name="moe_combine",
    )(plan["dst"], plan["n_pieces"], route, plan["loc"], x2, y)


def _round_up(v, m):
    return (v + m - 1) // m * m


def _moe_plan(counts, n_slots):
    n_tiles = counts.shape[0]
    cnt = counts[:, 0, :N_EXPERTS].astype(jnp.int32)
    seg = _round_up(cnt, MOE_PIECE)
    loc_end = jnp.cumsum(seg, axis=1)
    loc_start = loc_end - seg
    totals = jnp.sum(seg, axis=0)
    padded = _round_up(totals, MOE_BLOCK)
    pad_end = jnp.cumsum(padded)
    pad_start = pad_end - padded
    seg_start = pad_start[None, :] + jnp.cumsum(seg, axis=0) - seg
    piece_off = jnp.arange(PIECES_MAX, dtype=jnp.int32) * MOE_PIECE
    piece_e = jnp.sum(loc_end[:, None, :] <= piece_off[None, :, None], axis=2)
    hit = piece_e[..., None] == jnp.arange(N_EXPERTS)
    shift = jnp.sum(jnp.where(hit, (seg_start - loc_start)[:, None, :], 0), axis=2)
    valid = piece_off[None, :] < loc_end[:, -1:]
    dst = jnp.where(valid, shift + piece_off[None, :], 0).astype(jnp.int32).reshape(-1)
    zk = jnp.arange(ZERO_PIECES, dtype=jnp.int32)[None, :] * MOE_PIECE
    zvalid = zk < (padded - totals)[:, None]
    zdst = jnp.where(zvalid, (pad_start + totals)[:, None] + zk, 0)
    blk_start = jnp.arange(n_slots // MOE_BLOCK, dtype=jnp.int32) * MOE_BLOCK
    block_expert = jnp.minimum(jnp.sum(pad_end[None, :] <= blk_start[:, None], axis=1), N_EXPERTS - 1)
    loc = jnp.pad(loc_start.astype(F32), ((0, 0), (0, LANES - N_EXPERTS)))
    return {
        "dst": dst,
        "n_pieces": (loc_end[:, -1] // MOE_PIECE).astype(jnp.int32),
        "zdst": zdst.astype(jnp.int32).reshape(-1),
        "zvalid": zvalid.astype(jnp.int32).reshape(-1),
        "loc": jnp.broadcast_to(loc[:, None, :], (n_tiles, SUBLANES, LANES)),
        "block_expert": block_expert.astype(jnp.int32),
        "n_used": (pad_end[-1:] // MOE_BLOCK).astype(jnp.int32),
    }


def _pad_lanes(a, width=LANES):
    return jnp.pad(a, [(0, 0)] * (a.ndim - 1) + [(0, width - a.shape[-1])])


def kernel(x, norm_mix, w_in, dn_conv, dn_a_log, dn_dt_bias, dn_out_norm, at_q_norm, at_k_norm,
           rel_bias, cv_dw, cv_dw_bias, cv_ln_g, cv_ln_b, w_out, norm_ffn, router_group_w,
           router_group_b, router_expert_w, router_expert_b, ex_gate, ex_up, ex_down):
    bsz, seq, d = x.shape
    n_tok = bsz * seq
    depth = w_in.shape[0]
    c_ab = 4 * DN_WIDTH
    c_at = c_ab + 2 * DN_HEADS
    c_cv = c_at + 3 * AT_WIDTH
    bd_at = _block_diag_ones(AT_WIDTH, HEAD_DIM, BF16)
    bd_dn = _block_diag_ones(DN_WIDTH, HEAD_DIM, BF16)
    bias = _bias_tables(rel_bias)
    tri = (jnp.arange(ROW_TILE)[:, None] > jnp.arange(ROW_TILE)[None, :]).astype(BF16)
    per_head_lanes = lambda v: jnp.repeat(v, HEAD_DIM)[None, :]
    n_tiles = n_tok // MOE_TILE
    n_slots = _round_up(TOP_K * n_tok + n_tiles * N_EXPERTS * (MOE_PIECE - 1)
                        + N_EXPERTS * (MOE_BLOCK - 1), MOE_BLOCK)

    x2 = x.reshape(n_tok, d)
    for layer in range(depth):
        w_l = w_in[layer]
        dn, aq, ak, av, cu, ab = _proj(
            x2, norm_mix[layer][None, :],
            w_l[:, :c_ab].astype(BF16), w_l[:, c_at:c_cv].astype(BF16), w_l[:, c_cv:].astype(BF16),
            _pad_lanes(w_l[:, c_ab:c_at]).astype(BF16), bd_at,
            jnp.tile(at_q_norm[layer], AT_HEADS)[None, :] * (HEAD_DIM ** -0.5),
            jnp.tile(at_k_norm[layer], AT_HEADS)[None, :])

        y_dn = _deltanet(dn.reshape(bsz, seq, -1), ab.reshape(bsz, seq, LANES), dn_conv[layer],
                         per_head_lanes(dn_a_log[layer]), per_head_lanes(dn_dt_bias[layer]),
                         jnp.tile(dn_out_norm[layer], DN_HEADS)[None, :], bd_dn)
        y_at = _attention(aq.reshape(bsz, seq, -1), ak.reshape(bsz, seq, -1),
                          av.reshape(bsz, seq, -1), bias)
        y_cv = _conformer_conv(cu.reshape(bsz, seq, -1), cv_dw[layer], cv_dw_bias[layer][None, :],
                               cv_ln_g[layer][None, :], cv_ln_b[layer][None, :])

        wo = w_out[layer].astype(BF16)
        w_r = _pad_lanes(jnp.concatenate([router_expert_w[layer], router_group_w[layer]], axis=1))
        b_r = _pad_lanes(jnp.concatenate([router_expert_b[layer], router_group_b[layer]])[None, :])
        x_mid, h_ffn, route, counts = _out_proj(
            x2, y_dn.reshape(n_tok, DN_WIDTH), y_at.reshape(n_tok, AT_WIDTH),
            y_cv.reshape(n_tok, CV_WIDTH), wo[:DN_WIDTH], wo[DN_WIDTH:DN_WIDTH + AT_WIDTH],
            wo[DN_WIDTH + AT_WIDTH:], norm_ffn[layer][None, :], w_r.astype(BF16), b_r, tri)

        plan = _moe_plan(counts, n_slots)
        xs = _dispatch(plan, route, h_ffn, n_slots)
        y = _experts(plan, xs, ex_gate, ex_up, ex_down, layer)
        x2 = _combine(plan, route, x_mid, y)
    return x2.reshape(bsz, seq, d)
```

```python
import math

import jax
import jax.numpy as jnp
import numpy as np
from jax import lax
from jax.experimental import pallas as pl
from jax.experimental.pallas import tpu as pltpu

F32 = jnp.float32
BF16 = jnp.bfloat16
HIGHEST = lax.Precision.HIGHEST

EPS = 1e-6
NEG_INF = -1e30

HEAD_DIM = 64
DN_HEADS = 4
DN_WIDTH = DN_HEADS * HEAD_DIM
DN_CONV = 4
DN_CHUNK = 64
AT_HEADS = 8
AT_WIDTH = AT_HEADS * HEAD_DIM
AT_BLOCK = 128
WINDOWS = ((128, 1), (512, 4), (2048, 16))
MAX_DILATION = 16
AT_TILE = 256
AT_UNITS = 4
N_BUCKETS = 32
MAX_DISTANCE = 2048
CV_WIDTH = 256
CV_KERNEL = 31
N_GROUPS = 4
EXPERTS_PER_GROUP = 8
N_EXPERTS = N_GROUPS * EXPERTS_PER_GROUP
TOP_K = 2

LANES = 128
SUBLANES = 8
VMEM_LIMIT = 52 * 1024 * 1024

ROW_TILE = 512
SEQ_TILE = 256
CV_TILE = 64
OUT_SLAB = 256
DN_TILE = 64
DN_PREP_CHUNKS = 4
MOE_BLOCK = 512
MOE_TILE = ROW_TILE
MOE_PIECE = 16
MOE_CHUNK = 256
TILE_SLOTS = -(-(TOP_K * MOE_TILE + N_EXPERTS * (MOE_PIECE - 1)) // MOE_CHUNK) * MOE_CHUNK
PIECES_MAX = TILE_SLOTS // MOE_PIECE
ZERO_PIECES = MOE_BLOCK // MOE_PIECE - 1


def _mm(a, b, precision=None):
    return jnp.dot(a, b, preferred_element_type=F32, precision=precision)


def _mm_nt(a, b):
    return lax.dot_general(a, b, (((1,), (1,)), ((), ())), preferred_element_type=F32)


def _mm_tn(a, b):
    return lax.dot_general(a, b, (((0,), (0,)), ((), ())), preferred_element_type=F32)


def _sigmoid(x):
    return 0.5 * jnp.tanh(0.5 * x) + 0.5


def _silu(x):
    return x * _sigmoid(x)


def _split3(x):
    p0 = x.astype(BF16)
    r1 = x - p0.astype(F32)
    p1 = r1.astype(BF16)
    p2 = (r1 - p1.astype(F32)).astype(BF16)
    return p0, p1, p2


def _alternate(*stages):
    live = list(stages)
    while live:
        for gen in list(live):
            if next(gen, live) is live:
                live.remove(gen)


def _full_spec(a):
    nd = a.ndim
    return pl.BlockSpec(a.shape, lambda *_: (0,) * nd)


def _block_diag_ones(width, block, dtype):
    r = jnp.arange(width)[:, None] // block
    c = jnp.arange(width)[None, :] // block
    return (r == c).astype(dtype)


def _proj_kernel(x_ref, g_ref, wdn_ref, wat_ref, wcv_ref, wab_ref, bd_ref, qn_ref, kn_ref,
                 dn_ref, aq_ref, ak_ref, av_ref, cv_ref, ab_ref):
    x = x_ref[...]
    ms = jnp.mean(x * x, axis=-1, keepdims=True)
    h = (x * lax.rsqrt(ms + EPS) * g_ref[...]).astype(BF16)
    dn_ref[...] = _mm(h, wdn_ref[...]).astype(BF16)
    cv_ref[...] = _mm(h, wcv_ref[...]).astype(BF16)
    ab_ref[...] = _mm(h, wab_ref[...])
    at = _mm(h, wat_ref[...])
    q = at[:, 0:AT_WIDTH]
    k = at[:, AT_WIDTH:2 * AT_WIDTH]
    bd = bd_ref[...]
    qms = _mm((q * q).astype(BF16), bd) * (1.0 / HEAD_DIM)
    kms = _mm((k * k).astype(BF16), bd) * (1.0 / HEAD_DIM)
    aq_ref[...] = (q * lax.rsqrt(qms + EPS) * qn_ref[...]).astype(BF16)
    ak_ref[...] = (k * lax.rsqrt(kms + EPS) * kn_ref[...]).astype(BF16)
    av_ref[...] = at[:, 2 * AT_WIDTH:3 * AT_WIDTH].astype(BF16)


def _proj(x2, g, wdn, wat, wcv, wab, bd, qn, kn):
    n, d = x2.shape
    row = lambda w: pl.BlockSpec((ROW_TILE, w), lambda i: (i, 0))
    widths = (wdn.shape[1], AT_WIDTH, AT_WIDTH, AT_WIDTH, wcv.shape[1], LANES)
    dtypes = (BF16, BF16, BF16, BF16, BF16, F32)
    return pl.pallas_call(
        _proj_kernel,
        grid=(n // ROW_TILE,),
        in_specs=[row(d)] + [_full_spec(a) for a in (g, wdn, wat, wcv, wab, bd, qn, kn)],
        out_specs=[row(w) for w in widths],
        out_shape=[jax.ShapeDtypeStruct((n, w), t) for w, t in zip(widths, dtypes)],
        compiler_params=pltpu.CompilerParams(dimension_semantics=("parallel",),
                                             vmem_limit_bytes=VMEM_LIMIT),
        name="proj",
    )(x2, g, wdn, wat, wcv, wab, bd, qn, kn)


def _per_head(x, block_mask):
    return jnp.where(block_mask, jnp.concatenate([x] * DN_HEADS, axis=0), jnp.zeros((), x.dtype))


def _dn_kernel(dn_ref, ab_ref, cw_ref, alog_ref, dtb_ref, onorm_ref, bd_ref, y_ref,
               xpad, qs, ks, vs, gs, bs, os_, st, *group_bufs):
    set_a, set_b = group_bufs[:6], group_bufs[6:]
    seq = dn_ref.shape[1]
    n_tiles = seq // DN_TILE
    cw3 = 3 * DN_WIDTH
    pad = SUBLANES
    c = DN_CHUNK
    bd = bd_ref[...]
    hid = lax.broadcasted_iota(jnp.int32, (1, DN_WIDTH), 1) // HEAD_DIM

    def expand(cols, first):
        out = cols[:, first + DN_HEADS - 1:first + DN_HEADS]
        for h in range(DN_HEADS - 2, -1, -1):
            out = jnp.where(hid == h, cols[:, first + h:first + h + 1], out)
        return out

    xpad[0:pad, :] = jnp.zeros((pad, cw3), F32)
    for t in range(n_tiles):
        r0 = t * DN_TILE
        xpad[pad + r0:pad + r0 + DN_TILE, :] = dn_ref[0, r0:r0 + DN_TILE, 0:cw3].astype(F32)
    for t in range(n_tiles):
        r0 = t * DN_TILE
        ab = ab_ref[0, r0:r0 + DN_TILE, :]
        sp_in = expand(ab, 0) + dtb_ref[...]
        softplus = jnp.maximum(sp_in, 0.0) + jnp.log(1.0 + jnp.exp(-jnp.abs(sp_in)))
        gs[r0:r0 + DN_TILE, :] = -jnp.exp(alog_ref[...]) * softplus
        bs[r0:r0 + DN_TILE, :] = _sigmoid(expand(ab, DN_HEADS))
        acc = jnp.zeros((DN_TILE, cw3), F32)
        for j in range(DN_CONV):
            off = pad + r0 - (DN_CONV - 1) + j
            acc = acc + xpad[off:off + DN_TILE, :] * cw_ref[j:j + 1, :]
        y = _silu(acc)
        q = y[:, 0:DN_WIDTH]
        k = y[:, DN_WIDTH:2 * DN_WIDTH]
        qss = _mm((q * q).astype(BF16), bd)
        kss = _mm((k * k).astype(BF16), bd)
        qs[r0:r0 + DN_TILE, :] = q * lax.rsqrt(qss + EPS) * (HEAD_DIM ** -0.5)
        ks[r0:r0 + DN_TILE, :] = k * lax.rsqrt(kss + EPS)
        vs[r0:r0 + DN_TILE, :] = y[:, 2 * DN_WIDTH:3 * DN_WIDTH]

    ri = lax.broadcasted_iota(jnp.int32, (c, DN_WIDTH), 0)
    ci = lax.broadcasted_iota(jnp.int32, (c, DN_WIDTH), 1) % HEAD_DIM
    causal = ri >= ci
    strict = ri > ci
    eye_cat = (ri == ci).astype(F32)
    r2 = lax.broadcasted_iota(jnp.int32, (c, c), 0)
    c2 = lax.broadcasted_iota(jnp.int32, (c, c), 1)
    lower_ones = (r2 >= c2).astype(BF16)
    all_ones = jnp.ones((c, c), BF16)
    block_mask = (lax.broadcasted_iota(jnp.int32, (DN_WIDTH, DN_WIDTH), 0) // HEAD_DIM
                  == lax.broadcasted_iota(jnp.int32, (DN_WIDTH, DN_WIDTH), 1) // HEAD_DIM)

    def mm_exact_rhs(lhs_bf, x):
        p0, p1, p2 = _split3(x)
        return _mm(lhs_bf, p0) + _mm(lhs_bf, p1) + _mm(lhs_bf, p2)

    def mm_bd(lhs, rhs_cat):
        return _mm(lhs.astype(BF16), _per_head(rhs_cat.astype(BF16), block_mask))

    def prep(m, dst):
        ws, us, qks, qds, kds, gls = dst
        rows = [pl.multiple_of((DN_PREP_CHUNKS * m + cc) * c, c) for cc in range(DN_PREP_CHUNKS)]
        g_cum = [mm_exact_rhs(lower_ones, gs[pl.ds(r, c), :]) for r in rows]
        yield
        g_row = [mm_exact_rhs(all_ones, g * eye_cat) for g in g_cum]
        yield
        decay = [jnp.exp(jnp.where(causal, g - gr, NEG_INF)) for g, gr in zip(g_cum, g_row)]
        kc = [ks[pl.ds(r, c), :] for r in rows]
        qc = [qs[pl.ds(r, c), :] for r in rows]
        beta = [bs[pl.ds(r, c), :] for r in rows]
        kb = [k * b for k, b in zip(kc, beta)]
        aq = [_mm_nt(jnp.concatenate([b_, q_], axis=0).astype(BF16),
                     _per_head(k_.astype(BF16), block_mask))
              for b_, q_, k_ in zip(kb, qc, kc)]
        a_mat = [jnp.where(strict, x[:c] * d, 0.0) for x, d in zip(aq, decay)]
        qk = [x[c:] * d for x, d in zip(aq, decay)]
        p = [eye_cat - a for a in a_mat]
        yield
        pw = [mm_bd(a, a) for a in a_mat]
        for _ in range(4):
            yield
            both = [mm_bd(jnp.concatenate([p_, x], axis=0), x) for p_, x in zip(p, pw)]
            p = [p_ + b_[:c] for p_, b_ in zip(p, both)]
            pw = [b_[c:] for b_ in both]
        yield
        p = [p_ + mm_bd(p_, x) for p_, x in zip(p, pw)]
        yield
        for i, r in enumerate(rows):
            e_g = jnp.exp(g_cum[i])
            sl = slice(i * c, (i + 1) * c)
            ws[sl, :] = mm_bd(p[i], kb[i] * e_g).astype(BF16)
            us[sl, :] = mm_bd(p[i], vs[pl.ds(r, c), :] * beta[i])
            qks[sl, :] = qk[i].astype(BF16)
            qds[sl, :] = (qc[i] * e_g).astype(BF16)
            g_last = g_cum[i][c - 1:c, :]
            kds[sl, :] = (kc[i] * jnp.exp(g_last - g_cum[i])).astype(BF16)
            gls[i * SUBLANES:(i + 1) * SUBLANES, :] = jnp.broadcast_to(jnp.exp(g_last),
                                                                       (SUBLANES, DN_WIDTH))

    def scan_group(m, src):
        ws, us, qks, qds, kds, gls = src
        for i in range(DN_PREP_CHUNKS):
            sl = slice(i * c, (i + 1) * c)
            r = pl.multiple_of((DN_PREP_CHUNKS * m + i) * c, c)
            state = st[...]
            both = _mm(jnp.concatenate([ws[sl, :], qds[sl, :]], axis=0),
                       _per_head(state.astype(BF16), block_mask))
            yield
            v_new = (us[sl, :] - both[:c]).astype(BF16)
            os_[pl.ds(r, c), :] = both[c:] + _mm(qks[sl, :], _per_head(v_new, block_mask))
            kv = _mm_tn(kds[sl, :], v_new)
            upd = kv[(DN_HEADS - 1) * c:DN_HEADS * c, :]
            for h in range(DN_HEADS - 2, -1, -1):
                upd = jnp.where(hid == h, kv[h * c:(h + 1) * c, :], upd)
            yield
            st[...] = state * gls[i * SUBLANES:i * SUBLANES + 1, :] + upd

    alternate = _alternate

    st[...] = jnp.zeros(st.shape, F32)
    n_groups = seq // (DN_PREP_CHUNKS * c)
    assert n_groups % 2 == 0
    alternate(prep(0, set_a))

    def pair(j, carry):
        alternate(prep(2 * j + 1, set_b), scan_group(2 * j, set_a))
        alternate(prep(2 * j + 2, set_a), scan_group(2 * j + 1, set_b))
        return carry

    lax.fori_loop(0, n_groups // 2 - 1, pair, 0)
    alternate(prep(n_groups - 1, set_b), scan_group(n_groups - 2, set_a))
    alternate(scan_group(n_groups - 1, set_b))

    for t in range(n_tiles):
        r0 = t * DN_TILE
        o = os_[r0:r0 + DN_TILE, :]
        z = dn_ref[0, r0:r0 + DN_TILE, cw3:cw3 + DN_WIDTH].astype(F32)
        ms = _mm((o * o).astype(BF16), bd) * (1.0 / HEAD_DIM)
        y_ref[0, r0:r0 + DN_TILE, :] = (o * lax.rsqrt(ms + EPS) * onorm_ref[...] * _silu(z)).astype(BF16)


def _deltanet(dn, ab, conv_w, alog_cat, dtb_cat, onorm_cat, bd):
    b, seq, w = dn.shape
    f32buf = pltpu.VMEM((seq, DN_WIDTH), F32)
    group_rows = DN_PREP_CHUNKS * DN_CHUNK
    group_set = [pltpu.VMEM((group_rows, DN_WIDTH), BF16),
                 pltpu.VMEM((group_rows, DN_WIDTH), F32),
                 pltpu.VMEM((group_rows, DN_WIDTH), BF16),
                 pltpu.VMEM((group_rows, DN_WIDTH), BF16),
                 pltpu.VMEM((group_rows, DN_WIDTH), BF16),
                 pltpu.VMEM((DN_PREP_CHUNKS * SUBLANES, DN_WIDTH), F32)]
    return pl.pallas_call(
        _dn_kernel,
        grid=(b,),
        in_specs=[pl.BlockSpec((1, seq, w), lambda i: (i, 0, 0)),
                  pl.BlockSpec((1, seq, LANES), lambda i: (i, 0, 0))]
                 + [_full_spec(a) for a in (conv_w, alog_cat, dtb_cat, onorm_cat, bd)],
        out_specs=pl.BlockSpec((1, seq, DN_WIDTH), lambda i: (i, 0, 0)),
        out_shape=jax.ShapeDtypeStruct((b, seq, DN_WIDTH), BF16),
        scratch_shapes=[pltpu.VMEM((SUBLANES + seq, 3 * DN_WIDTH), F32),
                        f32buf, f32buf, f32buf, f32buf, f32buf,
                        f32buf,
                        pltpu.VMEM((DN_CHUNK, DN_WIDTH), F32)]
                       + group_set + group_set,
        compiler_params=pltpu.CompilerParams(dimension_semantics=("parallel",),
                                             vmem_limit_bytes=VMEM_LIMIT),
        name="deltanet",
    )(dn, ab, conv_w, alog_cat, dtb_cat, onorm_cat, bd)


def _run_starts(dil, r, i):
    if dil == 16:
        return [(tt * AT_TILE + r * 16, 16) for tt in range(AT_BLOCK * dil // AT_TILE)]
    if dil == 4:
        return [((2 * i + th) * AT_TILE + (4 * s + r) * 16, 16) for th in range(2) for s in range(4)]
    assert dil == 1
    return [((i // 2) * AT_TILE + rr * 16 + 8 * (i % 2), 8) for rr in range(16)]


def _run_order(dil):
    if dil == 16:
        return np.arange(AT_BLOCK)
    if dil == 4:
        th, s, ml = np.meshgrid(np.arange(2), np.arange(4), np.arange(16), indexing="ij")
        return (64 * th + 4 * ml + s).reshape(-1)
    rr, m8 = np.meshgrid(np.arange(16), np.arange(8), indexing="ij")
    return (16 * m8 + rr).reshape(-1)


def _load_runs(ref, p, runs):
    return jnp.concatenate([ref[p, pl.ds(pl.multiple_of(s, n), n), :] for s, n in runs], axis=0)


def _store_runs(ref, p, runs, val):
    off = 0
    for s, n in runs:
        ref[p, pl.ds(pl.multiple_of(s, n), n), :] = val[off:off + n]
        off += n


def _attn_kernel(q_ref, k_ref, v_ref, perm_ref, perm_t_ref, bias_ref, y_ref, qf, kf, vf, acc, ms, ls):
    seq = q_ref.shape[1]
    pairs = AT_HEADS // 2
    perm = perm_ref[...]
    for t in range(seq // AT_TILE):
        r0 = t * AT_TILE
        for src, dst in ((q_ref, qf), (k_ref, kf), (v_ref, vf)):
            rows = _mm(perm, src[0, r0:r0 + AT_TILE, :])
            for p in range(pairs):
                dst[p, r0:r0 + AT_TILE, :] = rows[:, p * LANES:(p + 1) * LANES]

    lane = lax.broadcasted_iota(jnp.int32, (1, LANES), 1)
    key_col = lax.broadcasted_iota(jnp.int32, (1, 2 * AT_BLOCK), 1)
    order = sorted(range(len(WINDOWS)), key=lambda g: -WINDOWS[g][1])
    for step, grp in enumerate(order):
        dil = WINDOWS[grp][1]
        nb = seq // dil // AT_BLOCK
        is_first = step == 0
        is_last = step == len(order) - 1

        def unit(u, dil=dil, nb=nb, grp=grp, is_first=is_first, is_last=is_last):
            r = u // nb
            i = u % nb
            q_runs = _run_starts(dil, r, i)
            p_runs = _run_starts(dil, r, jnp.maximum(i - 1, 0))
            no_prev = jnp.where(jnp.logical_and(i == 0, key_col < AT_BLOCK), NEG_INF, 0.0)
            def head_pair(p):
                q2 = _load_runs(qf, p, q_runs).astype(BF16)
                k2 = jnp.concatenate([_load_runs(kf, p, p_runs), _load_runs(kf, p, q_runs)],
                                     axis=0).astype(BF16)
                v2 = jnp.concatenate([_load_runs(vf, p, p_runs), _load_runs(vf, p, q_runs)],
                                     axis=0).astype(BF16)
                v_ext = jnp.concatenate([v2, jnp.ones(v2.shape, BF16)], axis=1)
                masks = [(lane // HEAD_DIM) == hh for hh in range(2)]
                scores = [_mm_nt(jnp.where(mask, q2, jnp.zeros_like(q2)), k2)
                          + bias_ref[grp, 2 * p + hh] + no_prev for hh, mask in enumerate(masks)]
                yield
                maxes = [jnp.max(s, axis=-1, keepdims=True) for s in scores]
                probs = [jnp.exp(s - m_h).astype(BF16) for s, m_h in zip(scores, maxes)]
                yield
                results = [_mm(pexp, v_ext) for pexp in probs]
                yield
                m_new = jnp.where(masks[1], maxes[1], jnp.broadcast_to(maxes[0], (AT_BLOCK, LANES)))
                o_new = jnp.where(masks[1], results[1][:, :LANES], results[0][:, :LANES])
                l_new = jnp.where(masks[1], results[1][:, LANES:], results[0][:, LANES:])
                if not is_first:
                    m_old = _load_runs(ms, p, q_runs)
                    m_tot = jnp.maximum(m_old, m_new)
                    a_old = jnp.exp(m_old - m_tot)
                    a_new = jnp.exp(m_new - m_tot)
                    l_new = a_old * _load_runs(ls, p, q_runs) + a_new * l_new
                    o_new = a_old * _load_runs(acc, p, q_runs) + a_new * o_new
                    m_new = m_tot
                if is_last:
                    _store_runs(acc, p, q_runs, o_new / l_new)
                else:
                    _store_runs(ms, p, q_runs, m_new)
                    _store_runs(ls, p, q_runs, l_new)
                    _store_runs(acc, p, q_runs, o_new)

            stages = [head_pair(p) for p in range(pairs)]
            for _ in range(3):
                for stage in stages:
                    next(stage)
                yield
            for stage in stages:
                next(stage, None)

        def units(j, carry, unit=unit):
            _alternate(*[unit(AT_UNITS * j + k) for k in range(AT_UNITS)])
            return carry

        lax.fori_loop(0, seq // AT_BLOCK // AT_UNITS, units, 0)

    perm_t = perm_t_ref[...]
    for t in range(seq // AT_TILE):
        r0 = t * AT_TILE
        for p in range(pairs):
            y_ref[0, r0:r0 + AT_TILE, p * LANES:(p + 1) * LANES] = _mm(
                perm_t, acc[p, r0:r0 + AT_TILE, :].astype(BF16)).astype(BF16)


def _tile_permutation():
    t = np.arange(AT_TILE)
    row = (t % MAX_DILATION) * (AT_TILE // MAX_DILATION) + t // MAX_DILATION
    perm = np.zeros((AT_TILE, AT_TILE), np.float32)
    perm[row, t] = 1.0
    return perm


def _attention(aq, ak, av, bias):
    b, seq, w = aq.shape
    assert seq == AT_BLOCK * MAX_DILATION and [d for _, d in WINDOWS] == [1, 4, 16]
    slab = pltpu.VMEM((AT_HEADS // 2, seq, LANES), F32)
    tok = pl.BlockSpec((1, seq, w), lambda i: (i, 0, 0))
    perm = _tile_permutation()
    perm_in = jnp.asarray(perm, BF16)
    perm_out = jnp.asarray(perm.T, BF16)
    return pl.pallas_call(
        _attn_kernel,
        grid=(b,),
        in_specs=[tok, tok, tok, _full_spec(perm_in), _full_spec(perm_out), _full_spec(bias)],
        out_specs=tok,
        out_shape=jax.ShapeDtypeStruct((b, seq, w), BF16),
        scratch_shapes=[slab] * 6,
        compiler_params=pltpu.CompilerParams(dimension_semantics=("parallel",),
                                             vmem_limit_bytes=VMEM_LIMIT),
        name="dilated_attention",
    )(aq, ak, av, perm_in, perm_out, bias)


def _t5_bucket(dist):
    max_exact = N_BUCKETS // 2
    d = np.maximum(dist, 1).astype(np.float32)
    log_bucket = max_exact + (np.log(d / np.float32(max_exact))
                              / np.float32(math.log(MAX_DISTANCE / max_exact))
                              * np.float32(N_BUCKETS - max_exact)).astype(np.int32)
    return np.where(dist < max_exact, dist, np.minimum(log_bucket, N_BUCKETS - 1))


def _bias_tables(rel_bias):
    tabs = []
    for window, dil in WINDOWS:
        n_back = window // dil
        j = _run_order(dil)
        rel = j[:, None] + AT_BLOCK - np.concatenate([j, AT_BLOCK + j])[None, :]
        valid = (rel >= 0) & (rel <= n_back)
        bucket = _t5_bucket(dil * np.clip(rel, 0, n_back)).reshape(-1)
        onehot = (np.arange(N_BUCKETS)[:, None] == bucket[None, :]).astype(np.float32)
        bias = jnp.dot(rel_bias.astype(F32).T, jnp.asarray(onehot, BF16).astype(F32), precision=HIGHEST)
        bias = bias.reshape(AT_HEADS, AT_BLOCK, 2 * AT_BLOCK)
        tabs.append(jnp.where(jnp.asarray(valid)[None], bias, NEG_INF))
    return jnp.stack(tabs)


def _cv_kernel(cu_ref, dw_ref, dwb_ref, g_ref, b_ref, y_ref, ypad):
    seq = cu_ref.shape[1]
    n_tiles = seq // SEQ_TILE
    pad = 4 * SUBLANES
    ypad[0:pad, :] = jnp.zeros((pad, CV_WIDTH), F32)
    for t in range(n_tiles):
        r0 = t * SEQ_TILE
        u = cu_ref[0, r0:r0 + SEQ_TILE, :].astype(F32)
        ypad[pad + r0:pad + r0 + SEQ_TILE, :] = u[:, :CV_WIDTH] * _sigmoid(u[:, CV_WIDTH:])
    for t in range(seq // CV_TILE):
        r0 = t * CV_TILE
        window = ypad[r0:r0 + pad + CV_TILE, :]
        acc = jnp.zeros((CV_TILE, CV_WIDTH), F32) + dwb_ref[...]
        for b in range(SUBLANES):
            rolled = pltpu.roll(window, b, axis=0) if b else window
            for j in range(CV_KERNEL):
                off = pad - (CV_KERNEL - 1) + j
                if (-off) % SUBLANES == b:
                    a8 = off + b
                    acc = acc + rolled[a8:a8 + CV_TILE, :] * dw_ref[j:j + 1, :]
        mu = jnp.mean(acc, axis=-1, keepdims=True)
        cen = acc - mu
        var = jnp.mean(cen * cen, axis=-1, keepdims=True)
        yn = cen * lax.rsqrt(var + EPS) * g_ref[...] + b_ref[...]
        y_ref[0, r0:r0 + CV_TILE, :] = _silu(yn).astype(BF16)


def _conformer_conv(cu, dw, dwb, ln_g, ln_b):
    b, seq, w = cu.shape
    return pl.pallas_call(
        _cv_kernel,
        grid=(b,),
        in_specs=[pl.BlockSpec((1, seq, w), lambda i: (i, 0, 0))]
                 + [_full_spec(a) for a in (dw, dwb, ln_g, ln_b)],
        out_specs=pl.BlockSpec((1, seq, CV_WIDTH), lambda i: (i, 0, 0)),
        out_shape=jax.ShapeDtypeStruct((b, seq, CV_WIDTH), BF16),
        scratch_shapes=[pltpu.VMEM((4 * SUBLANES + seq, CV_WIDTH), F32)],
        compiler_params=pltpu.CompilerParams(dimension_semantics=("parallel",),
                                             vmem_limit_bytes=VMEM_LIMIT),
        name="conformer_conv",
    )(cu, dw, dwb, ln_g, ln_b)


def _out_kernel(x_ref, ydn_ref, yat_ref, ycv_ref, wdn_ref, wat_ref, wcv_ref, g_ref, wr_ref, rb_ref,
                tri_ref, xo_ref, h_ref, route_ref, cnt_ref):
    lane = lax.broadcasted_iota(jnp.int32, (OUT_SLAB, LANES), 1)
    picks = {}

    def row_slab(r0):
        sl = slice(r0, r0 + OUT_SLAB)
        x = (x_ref[sl, :] + _mm(ydn_ref[sl, :], wdn_ref[...]) + _mm(yat_ref[sl, :], wat_ref[...])
             + _mm(ycv_ref[sl, :], wcv_ref[...]))
        xo_ref[sl, :] = x
        yield
        ms = jnp.mean(x * x, axis=-1, keepdims=True)
        h = (x * lax.rsqrt(ms + EPS) * g_ref[...]).astype(BF16)
        h_ref[sl, :] = h
        logits = _mm(h, wr_ref[...]) + rb_ref[...]
        yield
        is_group = (lane >= N_EXPERTS) & (lane < N_EXPERTS + N_GROUPS)
        gl = jnp.where(is_group, logits, NEG_INF)
        gmax = jnp.max(gl, axis=-1, keepdims=True)
        gsel = jnp.min(jnp.where(gl == gmax, lane, 2 * LANES), axis=-1, keepdims=True) - N_EXPERTS
        p_group = 1.0 / jnp.sum(jnp.where(is_group, jnp.exp(gl - gmax), 0.0), axis=-1, keepdims=True)
        yield
        lo = gsel * EXPERTS_PER_GROUP
        in_group = (lane >= lo) & (lane < lo + EXPERTS_PER_GROUP)
        el = jnp.where(in_group, logits, NEG_INF)
        v1 = jnp.max(el, axis=-1, keepdims=True)
        i1 = jnp.min(jnp.where(el == v1, lane, LANES), axis=-1, keepdims=True)
        yield
        el2 = jnp.where(lane == i1, NEG_INF, el)
        v2 = jnp.max(el2, axis=-1, keepdims=True)
        i2 = jnp.min(jnp.where(el2 == v2, lane, LANES), axis=-1, keepdims=True)
        t = jnp.exp(v2 - v1)
        g1 = p_group / (1.0 + t)
        picks[r0] = (i1, i2, g1, g1 * t)

    slabs = list(range(0, ROW_TILE, OUT_SLAB))
    _alternate(*[row_slab(r0) for r0 in slabs])

    both = jnp.concatenate(
        [jnp.where((lane == picks[r0][0]) | (lane == picks[r0][1]), 1.0, 0.0).astype(BF16)
         for r0 in slabs], axis=0)
    before = _mm(tri_ref[...], both)
    cnt_ref[0] = _mm(jnp.ones((SUBLANES, ROW_TILE), BF16), both)
    for r0 in slabs:
        i1, i2, g1, g2 = picks[r0]
        seen = before[r0:r0 + OUT_SLAB, :]
        rank1 = jnp.sum(jnp.where(lane == i1, seen, 0.0), axis=-1, keepdims=True)
        rank2 = jnp.sum(jnp.where(lane == i2, seen, 0.0), axis=-1, keepdims=True)
        route = jnp.where(lane == 0, i1.astype(F32), 0.0)
        route = jnp.where(lane == 1, i2.astype(F32), route)
        route = jnp.where(lane == 2, g1, route)
        route = jnp.where(lane == 3, g2, route)
        route = jnp.where(lane == 4, rank1, route)
        route = jnp.where(lane == 5, rank2, route)
        route_ref[r0:r0 + OUT_SLAB, :] = route


def _out_proj(x2, ydn, yat, ycv, wdn, wat, wcv, g, wr, rb, tri):
    n, d = x2.shape
    row = lambda w: pl.BlockSpec((ROW_TILE, w), lambda i: (i, 0))
    n_tiles = n // ROW_TILE
    return pl.pallas_call(
        _out_kernel,
        grid=(n_tiles,),
        in_specs=[row(d), row(DN_WIDTH), row(AT_WIDTH), row(CV_WIDTH)]
                 + [_full_spec(a) for a in (wdn, wat, wcv, g, wr, rb, tri)],
        out_specs=[row(d), row(d), row(LANES), pl.BlockSpec((1, SUBLANES, LANES), lambda i: (i, 0, 0))],
        out_shape=[jax.ShapeDtypeStruct((n, d), F32), jax.ShapeDtypeStruct((n, d), BF16),
                   jax.ShapeDtypeStruct((n, LANES), F32),
                   jax.ShapeDtypeStruct((n_tiles, SUBLANES, LANES), F32)],
        compiler_params=pltpu.CompilerParams(dimension_semantics=("parallel",),
                                             vmem_limit_bytes=VMEM_LIMIT),
        name="out_proj",
    )(x2, ydn, yat, ycv, wdn, wat, wcv, g, wr, rb, tri)


def _tile_slots(route, loc_row):
    lane = lax.broadcasted_iota(jnp.int32, route.shape, 1).astype(F32)
    slots = []
    for kk in range(TOP_K):
        base = jnp.sum(jnp.where(lane == route[:, kk:kk + 1], loc_row, 0.0), axis=-1, keepdims=True)
        slots.append(base + route[:, 4 + kk:5 + kk])
    return slots


def _chunk_cols(j):
    return (j * MOE_CHUNK + lax.broadcasted_iota(jnp.int32, (1, MOE_CHUNK), 1)).astype(F32)


def _piece(ref, row):
    return ref.at[pl.ds(pl.multiple_of(row, MOE_PIECE), MOE_PIECE), :]


def _start_alternating(copy_of, count):
    def pair(pp, carry):
        copy_of(2 * pp).start(priority=0)
        copy_of(2 * pp + 1).start(priority=1)
        return carry

    lax.fori_loop(0, count // 2, pair, 0)

    @pl.when(count % 2 == 1)
    def _():
        copy_of(count - 1).start(priority=0)


def _dispatch_kernel(dst_ref, np_ref, zdst_ref, zvalid_ref, nu_ref, route_ref, loc_ref, h_ref, xs_ref,
                     srt, zeros, sem, zsem):
    tile = pl.program_id(0)
    n_pieces = np_ref[tile]
    n_blocks = xs_ref.shape[0] // MOE_BLOCK

    def zero_piece(z):
        return pltpu.make_async_copy(_piece(zeros, 0), _piece(xs_ref, zdst_ref[z]), zsem)

    def zero_block(b):
        return pltpu.make_async_copy(
            zeros, xs_ref.at[pl.ds(pl.multiple_of(b * MOE_BLOCK, MOE_BLOCK), MOE_BLOCK), :], zsem)

    @pl.when(tile == 0)
    def _():
        zeros[...] = jnp.zeros(zeros.shape, BF16)
        for wait in (False, True):
            def piece_body(z, carry, wait=wait):
                @pl.when(zvalid_ref[z] != 0)
                def _():
                    zero_piece(z).wait() if wait else zero_piece(z).start()
                return carry

            def block_body(b, carry, wait=wait):
                zero_block(b).wait() if wait else zero_block(b).start()
                return carry

            lax.fori_loop(0, N_EXPERTS * ZERO_PIECES, piece_body, 0)
            lax.fori_loop(nu_ref[0], n_blocks, block_body, 0)

    slot1, slot2 = _tile_slots(route_ref[...], loc_ref[0, 0:1, :])
    lane = lax.broadcasted_iota(jnp.int32, (1, LANES), 1)
    slot_cols = jnp.where(lane == 0, slot1, jnp.where(lane == 1, slot2, 0.0))
    pick = (lax.broadcasted_iota(jnp.int32, (SUBLANES, LANES), 0)
            == lax.broadcasted_iota(jnp.int32, (SUBLANES, LANES), 1)).astype(BF16)
    slot_rows = sum(_mm_nt(pick, piece) for piece in _split3(slot_cols))
    slot1_row = slot_rows[0:1, :]
    slot2_row = slot_rows[1:2, :]
    h = h_ref[...]
    buf = srt.at[tile % 2]

    def chunk(j, carry):
        row = (j * MOE_CHUNK + lax.broadcasted_iota(jnp.int32, (MOE_CHUNK, 1), 0)).astype(F32)
        onehot = jnp.where(row == slot1_row, 1.0, jnp.where(row == slot2_row, 1.0, 0.0)).astype(BF16)
        buf[pl.ds(pl.multiple_of(j * MOE_CHUNK, MOE_CHUNK), MOE_CHUNK), :] = _mm(onehot, h).astype(BF16)
        return carry

    pieces_per_chunk = MOE_CHUNK // MOE_PIECE
    lax.fori_loop(0, (n_pieces + pieces_per_chunk - 1) // pieces_per_chunk, chunk, 0)

    def piece_copy(t, p):
        return pltpu.make_async_copy(_piece(srt.at[t % 2], p * MOE_PIECE),
                                     _piece(xs_ref, dst_ref[t * PIECES_MAX + p]), sem.at[t % 2])

    _start_alternating(lambda p: piece_copy(tile, p), n_pieces)

    def wait_tile(t):
        done = pltpu.make_async_copy(_piece(srt.at[t % 2], 0), _piece(xs_ref, 0), sem.at[t % 2])

        def wait(p, carry):
            done.wait()
            return carry
        lax.fori_loop(0, np_ref[t], wait, 0)

    @pl.when(tile > 0)
    def _():
        wait_tile(tile - 1)

    @pl.when(tile == pl.num_programs(0) - 1)
    def _():
        wait_tile(tile)


def _dispatch(plan, route, h, n_slots):
    n, d = h.shape
    n_tiles = n // MOE_TILE
    grid_spec = pltpu.PrefetchScalarGridSpec(
        num_scalar_prefetch=5,
        grid=(n_tiles,),
        in_specs=[pl.BlockSpec((MOE_TILE, LANES), lambda i, *_: (i, 0)),
                  pl.BlockSpec((1, SUBLANES, LANES), lambda i, *_: (i, 0, 0)),
                  pl.BlockSpec((MOE_TILE, d), lambda i, *_: (i, 0))],
        out_specs=pl.BlockSpec(memory_space=pl.ANY),
        scratch_shapes=[pltpu.VMEM((2, TILE_SLOTS, d), BF16), pltpu.VMEM((MOE_BLOCK, d), BF16),
                        pltpu.SemaphoreType.DMA((2,)), pltpu.SemaphoreType.DMA(())],
    )
    return pl.pallas_call(
        _dispatch_kernel,
        grid_spec=grid_spec,
        out_shape=jax.ShapeDtypeStruct((n_slots, d), BF16),
        compiler_params=pltpu.CompilerParams(dimension_semantics=("arbitrary",),
                                             vmem_limit_bytes=VMEM_LIMIT),
        name="moe_dispatch",
    )(plan["dst"], plan["n_pieces"], plan["zdst"], plan["zvalid"], plan["n_used"], route,
      plan["loc"], h)


def _expert_kernel(be_ref, nu_ref, x_ref, wg_ref, wu_ref, wd_ref, y_ref, wg_bf, wu_bf, wd_bf):
    i = pl.program_id(0)
    changed = jnp.logical_or(i == 0, be_ref[i] != be_ref[jnp.maximum(i - 1, 0)])

    @pl.when(jnp.logical_and(changed, i < nu_ref[0]))
    def _():
        wg_bf[...] = wg_ref[0].astype(BF16)
        wu_bf[...] = wu_ref[0].astype(BF16)
        wd_bf[...] = wd_ref[0].astype(BF16)

    @pl.when(i < nu_ref[0])
    def _():
        x = x_ref[...]
        g = _mm(x, wg_bf[...])
        u = _mm(x, wu_bf[...])
        y_ref[...] = _mm((_silu(g) * u).astype(BF16), wd_bf[...]).astype(BF16)

    @pl.when(i >= nu_ref[0])
    def _():
        y_ref[...] = jnp.zeros(y_ref.shape, BF16)


def _experts(plan, xs, wg, wu, wd, layer):
    ns, d = xs.shape
    de = wg.shape[3]
    rows = lambda i, be, nu: (jnp.minimum(i, nu[0] - 1), 0)
    grid_spec = pltpu.PrefetchScalarGridSpec(
        num_scalar_prefetch=2,
        grid=(ns // MOE_BLOCK,),
        in_specs=[pl.BlockSpec((MOE_BLOCK, d), rows),
                  pl.BlockSpec((None, 1, d, de), lambda i, be, nu: (layer, be[i], 0, 0)),
                  pl.BlockSpec((None, 1, d, de), lambda i, be, nu: (layer, be[i], 0, 0)),
                  pl.BlockSpec((None, 1, de, d), lambda i, be, nu: (layer, be[i], 0, 0))],
        out_specs=pl.BlockSpec((MOE_BLOCK, d), lambda i, be, nu: (i, 0)),
        scratch_shapes=[pltpu.VMEM((d, de), BF16), pltpu.VMEM((d, de), BF16),
                        pltpu.VMEM((de, d), BF16)],
    )
    return pl.pallas_call(
        _expert_kernel,
        grid_spec=grid_spec,
        out_shape=jax.ShapeDtypeStruct((ns, d), BF16),
        compiler_params=pltpu.CompilerParams(dimension_semantics=("arbitrary",),
                                             vmem_limit_bytes=VMEM_LIMIT),
        name="moe_experts",
    )(plan["block_expert"], plan["n_used"], xs, wg, wu, wd)


def _combine_kernel(dst_ref, np_ref, route_ref, loc_ref, x_ref, y_ref, o_ref, ysrt, sem):
    tile = pl.program_id(0)
    n_pieces = np_ref[tile]

    @pl.when(tile == 0)
    def _():
        ysrt[...] = jnp.zeros(ysrt.shape, BF16)

    def piece_copy(t, p):
        return pltpu.make_async_copy(_piece(y_ref, dst_ref[t * PIECES_MAX + p]),
                                     _piece(ysrt.at[t % 2], p * MOE_PIECE), sem.at[t % 2])

    def fetch_tile(t):
        _start_alternating(lambda p: piece_copy(t, p), np_ref[t])

    @pl.when(tile == 0)
    def _():
        fetch_tile(tile)

    @pl.when(tile + 1 < pl.num_programs(0))
    def _():
        fetch_tile(tile + 1)

    route = route_ref[...]
    slot1, slot2 = _tile_slots(route, loc_ref[0, 0:1, :])
    g1 = route[:, 2:3]
    g2 = route[:, 3:4]
    done = pltpu.make_async_copy(_piece(y_ref, 0), _piece(ysrt.at[tile % 2], 0), sem.at[tile % 2])

    def wait(p, carry):
        done.wait()
        return carry

    lax.fori_loop(0, n_pieces, wait, 0)

    def row_slab(r0):
        sl = slice(r0, r0 + OUT_SLAB)
        gates = jnp.concatenate(
            [jnp.where(_chunk_cols(j) == slot1[sl], g1[sl],
                       jnp.where(_chunk_cols(j) == slot2[sl], g2[sl], 0.0)).astype(BF16)
             for j in range(TILE_SLOTS // MOE_CHUNK)], axis=1)
        yield
        o_ref[sl, :] = x_ref[sl, :] + _mm(gates, ysrt[tile % 2])

    _alternate(*[row_slab(r0) for r0 in range(0, MOE_TILE, OUT_SLAB)])


def _combine(plan, route, x2, y):
    n, d = x2.shape
    grid_spec = pltpu.PrefetchScalarGridSpec(
        num_scalar_prefetch=2,
        grid=(n // MOE_TILE,),
        in_specs=[pl.BlockSpec((MOE_TILE, LANES), lambda i, *_: (i, 0)),
                  pl.BlockSpec((1, SUBLANES, LANES), lambda i, *_: (i, 0, 0)),
                  pl.BlockSpec((MOE_TILE, d), lambda i, *_: (i, 0)),
                  pl.BlockSpec(memory_space=pl.ANY)],
        out_specs=pl.BlockSpec((MOE_TILE, d), lambda i, *_: (i, 0)),
        scratch_shapes=[pltpu.VMEM((2, TILE_SLOTS, d), BF16), pltpu.SemaphoreType.DMA((2,))],
    )
    return pl.pallas_call(
        _combine_kernel,
        grid_spec=grid_spec,
        out_shape=jax.ShapeDtypeStruct((n, d), F32),
        compiler_params=pltpu.CompilerParams(dimension_semantics=("arbitrary",),
                                             vmem_limit_bytes=VMEM_LIMIT),
        name="moe_combine",
    )(plan["dst"], plan["n_pieces"], route, plan["loc"], x2, y)


def _round_up(v, m):
    return (v + m - 1) // m * m


def _moe_plan(counts, n_slots):
    n_tiles = counts.shape[0]
    cnt = counts[:, 0, :N_EXPERTS].astype(jnp.int32)
    seg = _round_up(cnt, MOE_PIECE)
    loc_end = jnp.cumsum(seg, axis=1)
    loc_start = loc_end - seg
    totals = jnp.sum(seg, axis=0)
    padded = _round_up(totals, MOE_BLOCK)
    pad_end = jnp.cumsum(padded)
    pad_start = pad_end - padded
    seg_start = pad_start[None, :] + jnp.cumsum(seg, axis=0) - seg
    piece_off = jnp.arange(PIECES_MAX, dtype=jnp.int32) * MOE_PIECE
    piece_e = jnp.sum(loc_end[:, None, :] <= piece_off[None, :, None], axis=2)
    hit = piece_e[..., None] == jnp.arange(N_EXPERTS)
    shift = jnp.sum(jnp.where(hit, (seg_start - loc_start)[:, None, :], 0), axis=2)
    valid = piece_off[None, :] < loc_end[:, -1:]
    dst = jnp.where(valid, shift + piece_off[None, :], 0).astype(jnp.int32).reshape(-1)
    zk = jnp.arange(ZERO_PIECES, dtype=jnp.int32)[None, :] * MOE_PIECE
    zvalid = zk < (padded - totals)[:, None]
    zdst = jnp.where(zvalid, (pad_start + totals)[:, None] + zk, 0)
    blk_start = jnp.arange(n_slots // MOE_BLOCK, dtype=jnp.int32) * MOE_BLOCK
    block_expert = jnp.minimum(jnp.sum(pad_end[None, :] <= blk_start[:, None], axis=1), N_EXPERTS - 1)
    loc = jnp.pad(loc_start.astype(F32), ((0, 0), (0, LANES - N_EXPERTS)))
    return {
        "dst": dst,
        "n_pieces": (loc_end[:, -1] // MOE_PIECE).astype(jnp.int32),
        "zdst": zdst.astype(jnp.int32).reshape(-1),
        "zvalid": zvalid.astype(jnp.int32).reshape(-1),
        "loc": jnp.broadcast_to(loc[:, None, :], (n_tiles, SUBLANES, LANES)),
        "block_expert": block_expert.astype(jnp.int32),
        "n_used": (pad_end[-1:] // MOE_BLOCK).astype(jnp.int32),
    }


def _pad_lanes(a, width=LANES):
    return jnp.pad(a, [(0, 0)] * (a.ndim - 1) + [(0, width - a.shape[-1])])


def kernel(x, norm_mix, w_in, dn_conv, dn_a_log, dn_dt_bias, dn_out_norm, at_q_norm, at_k_norm,
           rel_bias, cv_dw, cv_dw_bias, cv_ln_g, cv_ln_b, w_out, norm_ffn, router_group_w,
           router_group_b, router_expert_w, router_expert_b, ex_gate, ex_up, ex_down):
    bsz, seq, d = x.shape
    n_tok = bsz * seq
    depth = w_in.shape[0]
    c_ab = 4 * DN_WIDTH
    c_at = c_ab + 2 * DN_HEADS
    c_cv = c_at + 3 * AT_WIDTH
    bd_at = _block_diag_ones(AT_WIDTH, HEAD_DIM, BF16)
    bd_dn = _block_diag_ones(DN_WIDTH, HEAD_DIM, BF16)
    bias = _bias_tables(rel_bias)
    tri = (jnp.arange(ROW_TILE)[:, None] > jnp.arange(ROW_TILE)[None, :]).astype(BF16)
    per_head_lanes = lambda v: jnp.repeat(v, HEAD_DIM)[None, :]
    n_tiles = n_tok // MOE_TILE
    n_slots = _round_up(TOP_K * n_tok + n_tiles * N_EXPERTS * (MOE_PIECE - 1)
                        + N_EXPERTS * (MOE_BLOCK - 1), MOE_BLOCK)

    x2 = x.reshape(n_tok, d)
    for layer in range(depth):
        w_l = w_in[layer]
        dn, aq, ak, av, cu, ab = _proj(
            x2, norm_mix[layer][None, :],
            w_l[:, :c_ab].astype(BF16), w_l[:, c_at:c_cv].astype(BF16), w_l[:, c_cv:].astype(BF16),
            _pad_lanes(w_l[:, c_ab:c_at]).astype(BF16), bd_at,
            jnp.tile(at_q_norm[layer], AT_HEADS)[None, :] * (HEAD_DIM ** -0.5),
            jnp.tile(at_k_norm[layer], AT_HEADS)[None, :])

        y_dn = _deltanet(dn.reshape(bsz, seq, -1), ab.reshape(bsz, seq, LANES), dn_conv[layer],
                         per_head_lanes(dn_a_log[layer]), per_head_lanes(dn_dt_bias[layer]),
                         jnp.tile(dn_out_norm[layer], DN_HEADS)[None, :], bd_dn)
        y_at = _attention(aq.reshape(bsz, seq, -1), ak.reshape(bsz, seq, -1),
                          av.reshape(bsz, seq, -1), bias)
        y_cv = _conformer_conv(cu.reshape(bsz, seq, -1), cv_dw[layer], cv_dw_bias[layer][None, :],
                               cv_ln_g[layer][None, :], cv_ln_b[layer][None, :])

        wo = w_out[layer].astype(BF16)
        w_r = _pad_lanes(jnp.concatenate([router_expert_w[layer], router_group_w[layer]], axis=1))
        b_r = _pad_lanes(jnp.concatenate([router_expert_b[layer], router_group_b[layer]])[None, :])
        x_mid, h_ffn, route, counts = _out_proj(
            x2, y_dn.reshape(n_tok, DN_WIDTH), y_at.reshape(n_tok, AT_WIDTH),
            y_cv.reshape(n_tok, CV_WIDTH), wo[:DN_WIDTH], wo[DN_WIDTH:DN_WIDTH + AT_WIDTH],
            wo[DN_WIDTH + AT_WIDTH:], norm_ffn[layer][None, :], w_r.astype(BF16), b_r, tri)

        plan = _moe_plan(counts, n_slots)
        xs = _dispatch(plan, route, h_ffn, n_slots)
        y = _experts(plan, xs, ex_gate, ex_up, ex_down, layer)
        x2 = _combine(plan, route, x_mid, y)
    return x2.reshape(bsz, seq, d)
```
